```python
import math
import jax, jax.numpy as jnp
from jax import lax
import numpy as np

D_MODEL = 2048
BATCH = 4
SEQ = 8192
DEPTH = 4
DEC_BATCH = 2
DEC_SEQ = 8192
PAST_LEN = 128

HEAD_DIM = 128
A_HEADS = 8
A_KV = 2
B_HEADS = 8
B_KV = 2
C_WIDTH = 1024
C_BLOCKS = 8
C_BLOCK_W = C_WIDTH // C_BLOCKS
CONV_W = 4
LRU_C = 8.0
MIX_W = 1024
N_BRANCH = 3
FF = 4 * D_MODEL
WINDOW = 128
Q_BLOCK = 128
GRID_W = 64
ROPE_THETA = 10000.0
EPS = 1e-6

SPLIT_SIZES = (A_HEADS * HEAD_DIM, A_KV * HEAD_DIM, A_KV * HEAD_DIM,
               B_HEADS * HEAD_DIM, B_KV * HEAD_DIM, B_KV * HEAD_DIM,
               C_WIDTH, C_WIDTH, N_BRANCH * D_MODEL)
IN_W = 1024 + 256 + 256 + 1024 + 256 + 256 + 1024 + 1024 + 3 * 2048

kernel_name = "hybrid_gated_parallel_encoder"


def _rms_norm(x, g):
    xf = x.astype(jnp.float32)
    y = xf * lax.rsqrt(jnp.mean(jnp.square(xf), axis=-1, keepdims=True) + EPS)
    return (y * g.astype(jnp.float32)).astype(x.dtype)


def _rope_angles_1d(s):
    t = jnp.arange(s, dtype=jnp.float32)
    inv = ROPE_THETA ** (-jnp.arange(0, HEAD_DIM, 2, dtype=jnp.float32) / HEAD_DIM)
    return t[:, None] * inv[None, :]


def _rope_angles_axial(s):
    rows = s // GRID_W
    row = jnp.repeat(jnp.arange(rows, dtype=jnp.float32), GRID_W)
    col = jnp.tile(jnp.arange(GRID_W, dtype=jnp.float32), rows)
    half = HEAD_DIM // 2
    inv = ROPE_THETA ** (-jnp.arange(0, half, 2, dtype=jnp.float32) / half)
    return jnp.concatenate([row[:, None] * inv[None, :], col[:, None] * inv[None, :]], axis=-1)


def _apply_rope(x, ang):
    cos = jnp.cos(ang)[None, :, None, :]
    sin = jnp.sin(ang)[None, :, None, :]
    x1, x2 = jnp.split(x.astype(jnp.float32), 2, axis=-1)
    return jnp.concatenate([x1 * cos - x2 * sin, x2 * cos + x1 * sin], axis=-1).astype(x.dtype)


def _global_attention(q, k, v):
    b, s, h, d = q.shape
    kvh = k.shape[2]
    g = h // kvh
    nb = s // Q_BLOCK
    scale = d ** -0.5
    qb = q.reshape(b, nb, Q_BLOCK, kvh, g, d).transpose(1, 0, 2, 3, 4, 5)

    def block(qi):
        sc = jnp.einsum('bqhgd,bkhd->bhgqk', qi, k, preferred_element_type=jnp.float32) * scale
        p = jax.nn.softmax(sc, axis=-1)
        return jnp.einsum('bhgqk,bkhd->bqhgd', p.astype(v.dtype), v)

    o = lax.map(block, qb)
    return o.transpose(1, 0, 2, 3, 4, 5).reshape(b, s, h * d)


def _window_attention(q, k, v, sink):
    b, s, h, d = q.shape
    kvh = k.shape[2]
    g = h // kvh
    nb = s // Q_BLOCK
    scale = d ** -0.5
    pad = ((0, 0), (Q_BLOCK, Q_BLOCK), (0, 0), (0, 0))
    kp = jnp.pad(k, pad)
    vp = jnp.pad(v, pad)

    def bands(t):
        return jnp.concatenate(
            [t[:, o * Q_BLOCK: o * Q_BLOCK + s].reshape(b, nb, Q_BLOCK, kvh, d) for o in range(3)], axis=2)

    kb = bands(kp)
    vb = bands(vp)
    qb = q.reshape(b, nb, Q_BLOCK, kvh, g, d)
    sc = jnp.einsum('bnqhgd,bnkhd->bnhgqk', qb, kb, preferred_element_type=jnp.float32) * scale
    blk = jnp.arange(nb)[:, None, None] * Q_BLOCK
    qpos = blk + jnp.arange(Q_BLOCK)[None, :, None]
    kpos = blk - Q_BLOCK + jnp.arange(3 * Q_BLOCK)[None, None, :]
    valid = (jnp.abs(qpos - kpos) <= WINDOW) & (kpos >= 0) & (kpos < s)
    sc = jnp.where(valid[None, :, None, None], sc, -jnp.inf)
    sk = sink.astype(jnp.float32).reshape(kvh, g)[None, None, :, :, None, None]
    m = jnp.maximum(jnp.max(sc, axis=-1, keepdims=True), sk)
    e = jnp.exp(sc - m)
    p = e / (jnp.sum(e, axis=-1, keepdims=True) + jnp.exp(sk - m))
    o = jnp.einsum('bnhgqk,bnkhd->bnqhgd', p.astype(v.dtype), vb)
    return o.reshape(b, s, h * d)


def _lru_combine(c1, c2):
    a1, b1 = c1
    a2, b2 = c2
    return a1 * a2, a2 * b1 + b2


def _rglru_branch(xc, yc, conv_w, conv_b, gate_r_w, gate_r_b, gate_i_w, gate_i_b, lru_lambda):
    b, s, c = xc.shape
    left = CONV_W // 2
    xp = jnp.pad(xc, ((0, 0), (left, CONV_W - 1 - left), (0, 0)))
    u = conv_b
    for j in range(CONV_W):
        u = u + xp[:, j:j + s] * conv_w[j]
    ub = u.reshape(b, s, C_BLOCKS, C_BLOCK_W)
    uf = u.astype(jnp.float32)

    def direction(dd, reverse):
        r = jax.nn.sigmoid((jnp.einsum('bsnc,nce->bsne', ub, gate_r_w[dd]).reshape(b, s, c)
                            + gate_r_b[dd]).astype(jnp.float32))
        i = jax.nn.sigmoid((jnp.einsum('bsnc,nce->bsne', ub, gate_i_w[dd]).reshape(b, s, c)
                            + gate_i_b[dd]).astype(jnp.float32))
        log_a = LRU_C * r * jax.nn.log_sigmoid(lru_lambda[dd].astype(jnp.float32))
        a = jnp.exp(log_a)
        drive = jnp.sqrt(-jnp.expm1(2.0 * log_a)) * (i * uf)
        _, hs = lax.associative_scan(_lru_combine, (a, drive), axis=1, reverse=reverse)
        return hs

    hsum = direction(0, False) + direction(1, True)
    return (hsum * jax.nn.gelu(yc.astype(jnp.float32), approximate=True)).astype(xc.dtype)


def _mixer(h, p):
    b, s, _ = h.shape
    proj = h @ p['w_in']
    parts = []
    off = 0
    for sz in SPLIT_SIZES:
        parts.append(proj[..., off:off + sz])
        off += sz
    qa, ka, va, qb, kb, vb, xc, yc, gates = parts

    ang_ax = _rope_angles_axial(s)
    qa = _apply_rope(_rms_norm(qa.reshape(b, s, A_HEADS, HEAD_DIM), p['q_norm_a']), ang_ax)
    ka = _apply_rope(_rms_norm(ka.reshape(b, s, A_KV, HEAD_DIM), p['k_norm_a']), ang_ax)
    oa = _global_attention(qa, ka, va.reshape(b, s, A_KV, HEAD_DIM))

    ang_1d = _rope_angles_1d(s)
    qb = _apply_rope(qb.reshape(b, s, B_HEADS, HEAD_DIM), ang_1d)
    kb = _apply_rope(kb.reshape(b, s, B_KV, HEAD_DIM), ang_1d)
    ob = _window_attention(qb, kb, vb.reshape(b, s, B_KV, HEAD_DIM), p['sink_b'])

    oc = _rglru_branch(xc, yc, p['conv_w'], p['conv_b'], p['gate_r_w'], p['gate_r_b'],
                       p['gate_i_w'], p['gate_i_b'], p['lru_lambda'])

    g = jax.nn.sigmoid(gates.astype(jnp.float32)).astype(h.dtype).reshape(b, s, N_BRANCH, D_MODEL)
    wb = p['w_branch']
    merged = (g[:, :, 0] * (oa @ wb[0]) + g[:, :, 1] * (ob @ wb[1]) + g[:, :, 2] * (oc @ wb[2]))
    return merged @ p['w_out']


def _mlp(h, w_in, w_out):
    return jnp.square(jax.nn.relu(h @ w_in)) @ w_out


def _trunk(x, layers):
    for p in layers:
        x = x + _rms_norm(_mixer(_rms_norm(x, p['norm_mix_pre']), p), p['norm_mix_post'])
        x = x + _rms_norm(_mlp(_rms_norm(x, p['norm_ffn_pre']), p['w_ffn_in'], p['w_ffn_out']),
                          p['norm_ffn_post'])
    return x


def setup_inputs(seed: int = 0) -> dict:
    key = jax.random.key(seed)
    ks = jax.random.split(key, 24)
    f32 = jnp.float32
    nrm = lambda k, shape, scale: jax.random.normal(k, shape, f32) * scale
    gain = lambda k, shape: 1.0 + 0.05 * jax.random.normal(k, shape, f32)
    a_c = jax.random.uniform(ks[14], (DEPTH, 2, C_WIDTH), f32, 0.9, 0.999)
    a0 = a_c ** (1.0 / LRU_C)
    lru_lambda = jnp.log(a0) - jnp.log1p(-a0)
    return {
        'x_prompt': jax.random.normal(ks[0], (BATCH, SEQ, D_MODEL), f32),
        'x_sample': jax.random.normal(ks[1], (DEC_BATCH, DEC_SEQ, D_MODEL), f32),
        'norm_mix_pre': gain(ks[2], (DEPTH, D_MODEL)),
        'norm_mix_post': gain(ks[3], (DEPTH, D_MODEL)),
        'norm_ffn_pre': gain(ks[4], (DEPTH, D_MODEL)),
        'norm_ffn_post': gain(ks[5], (DEPTH, D_MODEL)),
        'w_in': nrm(ks[6], (DEPTH, D_MODEL, IN_W), D_MODEL ** -0.5),
        'q_norm_a': gain(ks[7], (DEPTH, HEAD_DIM)),
        'k_norm_a': gain(ks[8], (DEPTH, HEAD_DIM)),
        'sink_b': nrm(ks[9], (DEPTH, B_HEADS), 0.5),
        'conv_w': nrm(ks[10], (DEPTH, CONV_W, C_WIDTH), CONV_W ** -0.5),
        'conv_b': nrm(ks[11], (DEPTH, C_WIDTH), 0.01),
        'gate_r_w': nrm(ks[12], (DEPTH, 2, C_BLOCKS, C_BLOCK_W, C_BLOCK_W), C_BLOCK_W ** -0.5),
        'gate_r_b': nrm(ks[13], (DEPTH, 2, C_WIDTH), 0.01),
        'gate_i_w': nrm(ks[15], (DEPTH, 2, C_BLOCKS, C_BLOCK_W, C_BLOCK_W), C_BLOCK_W ** -0.5),
        'gate_i_b': nrm(ks[16], (DEPTH, 2, C_WIDTH), 0.01),
        'lru_lambda': lru_lambda,
        'w_branch': nrm(ks[17], (DEPTH, N_BRANCH, MIX_W, D_MODEL), MIX_W ** -0.5),
        'w_out': nrm(ks[18], (DEPTH, D_MODEL, D_MODEL), D_MODEL ** -0.5),
        'w_ffn_in': nrm(ks[19], (DEPTH, D_MODEL, FF), D_MODEL ** -0.5),
        'w_ffn_out': nrm(ks[20], (DEPTH, FF, D_MODEL), FF ** -0.5),
    }


def reference(x_prompt, x_sample, norm_mix_pre, norm_mix_post, norm_ffn_pre, norm_ffn_post, w_in,
              q_norm_a, k_norm_a, sink_b, conv_w, conv_b, gate_r_w, gate_r_b, gate_i_w, gate_i_b,
              lru_lambda, w_branch, w_out, w_ffn_in, w_ffn_out):
    layers = [dict(norm_mix_pre=norm_mix_pre[l], norm_mix_post=norm_mix_post[l],
                   norm_ffn_pre=norm_ffn_pre[l], norm_ffn_post=norm_ffn_post[l],
                   w_in=w_in[l], q_norm_a=q_norm_a[l], k_norm_a=k_norm_a[l], sink_b=sink_b[l],
                   conv_w=conv_w[l], conv_b=conv_b[l], gate_r_w=gate_r_w[l], gate_r_b=gate_r_b[l],
                   gate_i_w=gate_i_w[l], gate_i_b=gate_i_b[l], lru_lambda=lru_lambda[l],
                   w_branch=w_branch[l], w_out=w_out[l], w_ffn_in=w_ffn_in[l], w_ffn_out=w_ffn_out[l])
              for l in range(DEPTH)]
    y_prompt = _trunk(x_prompt, layers)
    y_sample = _trunk(x_sample, layers)
    return (y_prompt, y_sample)
```

```python
import functools
import math

import jax
import jax.numpy as jnp
from jax import lax
from jax.experimental import pallas as pl
from jax.experimental.pallas import tpu as pltpu

F32 = jnp.float32
BF16 = jnp.bfloat16

HEAD_DIM = 128
A_HEADS = 8
A_KV = 2
B_HEADS = 8
B_KV = 2
C_WIDTH = 1024
C_BLOCKS = 8
CONV_W = 4
LRU_C = 8.0
MIX_W = 1024
N_BRANCH = 3
WINDOW = 128
GRID_W = 64
ROPE_THETA = 10000.0
EPS = 1e-6
NEG_BIG = -1e30

VMEM_LIMIT_BYTES = 56 * 1024 * 1024


def _params(*sem):
    return pltpu.CompilerParams(dimension_semantics=sem, vmem_limit_bytes=VMEM_LIMIT_BYTES)


def _tile(n, pref):
    t = min(n, pref)
    assert n % t == 0, (n, t)
    return t


def _rms(x, g):
    return x * lax.rsqrt(jnp.mean(x * x, axis=-1, keepdims=True) + EPS) * g


def _rope(x, cos, sin_signed):
    return x * cos + pltpu.roll(x, HEAD_DIM // 2, axis=1) * sin_signed


def _prenorm_kernel(x_ref, g_ref, h_ref):
    h_ref[...] = _rms(x_ref[...], g_ref[...]).astype(BF16)


def _prenorm(x, g):
    t, d = x.shape
    tm = _tile(t, 512)
    return pl.pallas_call(
        _prenorm_kernel,
        grid=(t // tm,),
        in_specs=[pl.BlockSpec((tm, d), lambda i: (i, 0)),
                  pl.BlockSpec((1, d), lambda i: (0, 0))],
        out_specs=pl.BlockSpec((tm, d), lambda i: (i, 0)),
        out_shape=jax.ShapeDtypeStruct((t, d), BF16),
        compiler_params=_params("parallel"),
        name="prenorm",
    )(x, g.reshape(1, d))


def _proj_rope_kernel(h_ref, w_ref, g_ref, cos_ref, sin_ref, o_ref, *, normed):
    acc = jnp.dot(h_ref[...], w_ref[...], preferred_element_type=F32)
    cos = cos_ref[...]
    sin = sin_ref[...]
    for hd in range(acc.shape[1] // HEAD_DIM):
        sl = slice(hd * HEAD_DIM, (hd + 1) * HEAD_DIM)
        xh = acc[:, sl]
        if normed:
            xh = _rms(xh, g_ref[:, sl])
        o_ref[:, sl] = _rope(xh, cos, sin).astype(BF16)


def _proj_rope(h, w, gains, cos, sin, seq, normed):
    t, d = h.shape
    n = w.shape[1]
    tm = _tile(seq, 512)
    nsb = seq // tm
    return pl.pallas_call(
        functools.partial(_proj_rope_kernel, normed=normed),
        grid=(t // tm,),
        in_specs=[pl.BlockSpec((tm, d), lambda i: (i, 0)),
                  pl.BlockSpec((d, n), lambda i: (0, 0)),
                  pl.BlockSpec((1, n), lambda i: (0, 0)),
                  pl.BlockSpec((tm, HEAD_DIM), lambda i: (i % nsb, 0)),
                  pl.BlockSpec((tm, HEAD_DIM), lambda i: (i % nsb, 0))],
        out_specs=pl.BlockSpec((tm, n), lambda i: (i, 0)),
        out_shape=jax.ShapeDtypeStruct((t, n), BF16),
        compiler_params=_params("parallel"),
        name="proj_rope_normed" if normed else "proj_rope",
    )(h, w, gains, cos, sin)


def _proj_kernel(h_ref, w_ref, o_ref, *, act):
    acc = jnp.dot(h_ref[...], w_ref[...], preferred_element_type=F32)
    if act == "sigmoid":
        acc = jax.nn.sigmoid(acc)
    elif act == "relu2":
        acc = jnp.square(jnp.maximum(acc, 0.0))
    o_ref[...] = acc.astype(o_ref.dtype)


def _proj(h, w, act, out_dtype, name):
    t, d = h.shape
    n = w.shape[1]
    tm = _tile(t, 512)
    tn = _tile(n, 2048)
    return pl.pallas_call(
        functools.partial(_proj_kernel, act=act),
        grid=(n // tn, t // tm),
        in_specs=[pl.BlockSpec((tm, d), lambda j, i: (i, 0)),
                  pl.BlockSpec((d, tn), lambda j, i: (0, j))],
        out_specs=pl.BlockSpec((tm, tn), lambda j, i: (i, j)),
        out_shape=jax.ShapeDtypeStruct((t, n), out_dtype),
        compiler_params=_params("parallel", "parallel"),
        name=name,
    )(h, w)


def _attn_global_kernel(q_ref, k_ref, v_ref, o_ref, m_ref, l_ref, acc_ref, *, tq, tk, n_rep):
    seq = k_ref.shape[0]
    scale = HEAD_DIM ** -0.5
    q = jnp.concatenate([q_ref[:, g * HEAD_DIM:(g + 1) * HEAD_DIM] for g in range(n_rep)], axis=0)
    m_ref[...] = jnp.full(m_ref.shape, NEG_BIG, F32)
    l_ref[...] = jnp.zeros(l_ref.shape, F32)
    acc_ref[...] = jnp.zeros(acc_ref.shape, F32)

    def body(c, carry):
        off = pl.multiple_of(c * tk, tk)
        k = k_ref[pl.ds(off, tk), :]
        v = v_ref[pl.ds(off, tk), :]
        s = lax.dot_general(q, k, (((1,), (1,)), ((), ())), preferred_element_type=F32) * scale
        m_prev = m_ref[...]
        m_new = jnp.maximum(m_prev, jnp.max(s, axis=-1, keepdims=True))
        alpha = jnp.exp(m_prev - m_new)
        p = jnp.exp(s - m_new)
        l_ref[...] = alpha * l_ref[...] + jnp.sum(p, axis=-1, keepdims=True)
        acc_ref[...] = alpha * acc_ref[...] + jnp.dot(p.astype(BF16), v, preferred_element_type=F32)
        m_ref[...] = m_new
        return carry

    lax.fori_loop(0, seq // tk, body, 0)
    o = acc_ref[...] / l_ref[...]
    for g in range(n_rep):
        o_ref[:, g * HEAD_DIM:(g + 1) * HEAD_DIM] = o[g * tq:(g + 1) * tq].astype(BF16)


def _attn_global(qk, v, nseq, seq):
    t = qk.shape[0]
    n_rep = A_HEADS // A_KV
    tq = _tile(seq, 256)
    tk = _tile(seq, 512)
    nqb = seq // tq
    gw = n_rep * HEAD_DIM
    return pl.pallas_call(
        functools.partial(_attn_global_kernel, tq=tq, tk=tk, n_rep=n_rep),
        grid=(nseq, A_KV, nqb),
        in_specs=[pl.BlockSpec((tq, gw), lambda b, h, i: (b * nqb + i, h)),
                  pl.BlockSpec((seq, HEAD_DIM), lambda b, h, i: (b, A_HEADS + h)),
                  pl.BlockSpec((seq, HEAD_DIM), lambda b, h, i: (b, h))],
        out_specs=pl.BlockSpec((tq, gw), lambda b, h, i: (b * nqb + i, h)),
        out_shape=jax.ShapeDtypeStruct((t, A_HEADS * HEAD_DIM), BF16),
        scratch_shapes=[pltpu.VMEM((n_rep * tq, 1), F32),
                        pltpu.VMEM((n_rep * tq, 1), F32),
                        pltpu.VMEM((n_rep * tq, HEAD_DIM), F32)],
        compiler_params=_params("parallel", "parallel", "parallel"),
        name="attn_global",
    )(qk, qk, v)


def _attn_window_kernel(sink_ref, q_ref, k_ref, v_ref, o_ref, *, tq, n_rep):
    seq = k_ref.shape[0]
    wk = min(seq, tq + 2 * WINDOW)
    scale = HEAD_DIM ** -0.5
    h = pl.program_id(1)
    i = pl.program_id(2)
    start = pl.multiple_of(jnp.clip(i * tq - WINDOW, 0, seq - wk), HEAD_DIM)
    k = k_ref[pl.ds(start, wk), :]
    v = v_ref[pl.ds(start, wk), :]
    qpos = i * tq + lax.broadcasted_iota(jnp.int32, (tq, wk), 0)
    kpos = start + lax.broadcasted_iota(jnp.int32, (tq, wk), 1)
    valid = jnp.abs(qpos - kpos) <= WINDOW
    for g in range(n_rep):
        sl = slice(g * HEAD_DIM, (g + 1) * HEAD_DIM)
        s = lax.dot_general(q_ref[:, sl], k, (((1,), (1,)), ((), ())), preferred_element_type=F32) * scale
        s = jnp.where(valid, s, NEG_BIG)
        sk = sink_ref[h * n_rep + g]
        m = jnp.maximum(jnp.max(s, axis=-1, keepdims=True), sk)
        e = jnp.exp(s - m)
        p = e / (jnp.sum(e, axis=-1, keepdims=True) + jnp.exp(sk - m))
        o_ref[:, sl] = jnp.dot(p.astype(BF16), v, preferred_element_type=F32).astype(BF16)


def _attn_window(qk, v, sink, nseq, seq):
    t = qk.shape[0]
    n_rep = B_HEADS // B_KV
    tq = _tile(seq, 256)
    nqb = seq // tq
    gw = n_rep * HEAD_DIM
    return pl.pallas_call(
        functools.partial(_attn_window_kernel, tq=tq, n_rep=n_rep),
        grid=(nseq, B_KV, nqb),
        in_specs=[pl.BlockSpec(memory_space=pltpu.SMEM),
                  pl.BlockSpec((tq, gw), lambda b, h, i: (b * nqb + i, h)),
                  pl.BlockSpec((seq, HEAD_DIM), lambda b, h, i: (b, B_HEADS + h)),
                  pl.BlockSpec((seq, HEAD_DIM), lambda b, h, i: (b, A_KV + h))],
        out_specs=pl.BlockSpec((tq, gw), lambda b, h, i: (b * nqb + i, h)),
        out_shape=jax.ShapeDtypeStruct((t, B_HEADS * HEAD_DIM), BF16),
        compiler_params=_params("parallel", "parallel", "parallel"),
        name="attn_window",
    )(sink, qk, qk, v)


def _group_scan(a, b, reverse):
    rows = a.shape[0]
    a3 = a.reshape(rows // 8, 8, a.shape[1])
    b3 = b.reshape(rows // 8, 8, b.shape[1])
    sub = lax.broadcasted_iota(jnp.int32, a3.shape, 1)
    for sh in (1, 2, 4):
        if reverse:
            a_sh = pltpu.roll(a3, 8 - sh, axis=1)
            b_sh = pltpu.roll(b3, 8 - sh, axis=1)
            ok = sub < 8 - sh
        else:
            a_sh = pltpu.roll(a3, sh, axis=1)
            b_sh = pltpu.roll(b3, sh, axis=1)
            ok = sub >= sh
        b3 = jnp.where(ok, a3 * b_sh + b3, b3)
        a3 = jnp.where(ok, a3 * a_sh, a3)
    return a3.reshape(a.shape), b3.reshape(b.shape)


def _rglru_kernel(x_ref, y_ref, cw_ref, cb_ref, wr_ref, br_ref, wi_ref, bi_ref, lam_ref, o_ref,
                  hf_ref, a_ref, b_ref, *, chunk):
    seq = x_ref.shape[0]
    nchunk = seq // chunk
    ngroup = chunk // 8
    left = CONV_W // 2
    lam = lam_ref[...]
    log_sig = jnp.minimum(lam, 0.0) - jnp.log1p(jnp.exp(-jnp.abs(lam)))

    def conv(c):
        t0 = pl.multiple_of(c * chunk, chunk)
        lo = x_ref[pl.ds(jnp.maximum(t0 - 8, 0), 8), :]
        lo = jnp.where(c > 0, lo, 0.0)
        hi = x_ref[pl.ds(jnp.minimum(t0 + chunk, seq - 8), 8), :]
        hi = jnp.where(c < nchunk - 1, hi, 0.0)
        win = jnp.concatenate([lo, x_ref[pl.ds(t0, chunk), :], hi], axis=0)
        u = cb_ref[...]
        for j in range(CONV_W):
            st = 8 - left + j
            u = u + win[st:st + chunk] * cw_ref[j:j + 1, :]
        return t0, u

    def recurrence_terms(u, d):
        ub = u.astype(BF16)
        r = jax.nn.sigmoid(jnp.dot(ub, wr_ref[d], preferred_element_type=F32) + br_ref[d:d + 1, :])
        gi = jax.nn.sigmoid(jnp.dot(ub, wi_ref[d], preferred_element_type=F32) + bi_ref[d:d + 1, :])
        log_a = LRU_C * r * log_sig[d:d + 1, :]
        a = jnp.exp(log_a)
        th = jnp.tanh(log_a)
        drive = jnp.sqrt(-2.0 * th / (1.0 - th)) * (gi * u)
        return a, drive

    def fwd_chunk(c, carry):
        t0, u = conv(c)
        a, drive = recurrence_terms(u, 0)
        a_ref[...], b_ref[...] = _group_scan(a, drive, reverse=False)

        def grp(g, h_in):
            r0 = pl.multiple_of(g * 8, 8)
            h = a_ref[pl.ds(r0, 8), :] * h_in + b_ref[pl.ds(r0, 8), :]
            hf_ref[pl.ds(t0 + r0, 8), :] = h
            return h[7:8, :]

        return lax.fori_loop(0, ngroup, grp, carry, unroll=8)

    lax.fori_loop(0, nchunk, fwd_chunk, jnp.zeros((1, x_ref.shape[1]), F32))

    def bwd_chunk(cc, carry):
        c = nchunk - 1 - cc
        t0, u = conv(c)
        a, drive = recurrence_terms(u, 1)
        a_ref[...], b_ref[...] = _group_scan(a, drive, reverse=True)

        def grp(gg, h_in):
            r0 = pl.multiple_of((ngroup - 1 - gg) * 8, 8)
            h = a_ref[pl.ds(r0, 8), :] * h_in + b_ref[pl.ds(r0, 8), :]
            b_ref[pl.ds(r0, 8), :] = h
            return h[0:1, :]

        carry = lax.fori_loop(0, ngroup, grp, carry, unroll=8)
        hsum = hf_ref[pl.ds(t0, chunk), :] + b_ref[...]
        gate = jax.nn.gelu(y_ref[pl.ds(t0, chunk), :], approximate=True)
        o_ref[pl.ds(t0, chunk), :] = (hsum * gate).astype(BF16)
        return carry

    lax.fori_loop(0, nchunk, bwd_chunk, jnp.zeros((1, x_ref.shape[1]), F32))


def _rglru(xy, conv_w, conv_b, wr, br, wi, bi, lam, nseq, seq):
    t = xy.shape[0]
    bw = C_WIDTH // C_BLOCKS
    chunk = _tile(seq, 512)
    vec = lambda rows: pl.BlockSpec((rows, bw), lambda b, n: (0, n))
    mat = pl.BlockSpec((2, None, bw, bw), lambda b, n: (0, n, 0, 0))
    return pl.pallas_call(
        functools.partial(_rglru_kernel, chunk=chunk),
        grid=(nseq, C_BLOCKS),
        in_specs=[pl.BlockSpec((seq, bw), lambda b, n: (b, n)),
                  pl.BlockSpec((seq, bw), lambda b, n: (b, C_BLOCKS + n)),
                  vec(CONV_W), vec(1), mat, vec(2), mat, vec(2), vec(2)],
        out_specs=pl.BlockSpec((seq, bw), lambda b, n: (b, n)),
        out_shape=jax.ShapeDtypeStruct((t, C_WIDTH), BF16),
        scratch_shapes=[pltpu.VMEM((seq, bw), F32),
                        pltpu.VMEM((chunk, bw), F32),
                        pltpu.VMEM((chunk, bw), F32)],
        compiler_params=_params("parallel", "parallel"),
        name="rglru",
    )(xy, xy, conv_w, conv_b.reshape(1, C_WIDTH), wr, br, wi, bi, lam)


def _merge_kernel(oa_ref, ob_ref, oc_ref, g0_ref, g1_ref, g2_ref, wb_ref, o_ref):
    acc = g0_ref[...] * jnp.dot(oa_ref[...], wb_ref[0], preferred_element_type=F32)
    acc = acc + g1_ref[...] * jnp.dot(ob_ref[...], wb_ref[1], preferred_element_type=F32)
    acc = acc + g2_ref[...] * jnp.dot(oc_ref[...], wb_ref[2], preferred_element_type=F32)
    o_ref[...] = acc.astype(BF16)


def _merge(oa, ob, oc, gates, wb):
    t, kw = oa.shape
    d = wb.shape[2]
    tm = _tile(t, 512)
    tn = _tile(d, 1024)
    ncb = d // tn
    o_spec = pl.BlockSpec((tm, kw), lambda j, i: (i, 0))
    g_spec = lambda br: pl.BlockSpec((tm, tn), lambda j, i: (i, br * ncb + j))
    return pl.pallas_call(
        _merge_kernel,
        grid=(ncb, t // tm),
        in_specs=[o_spec, o_spec, o_spec, g_spec(0), g_spec(1), g_spec(2),
                  pl.BlockSpec((N_BRANCH, kw, tn), lambda j, i: (0, 0, j))],
        out_specs=pl.BlockSpec((tm, tn), lambda j, i: (i, j)),
        out_shape=jax.ShapeDtypeStruct((t, d), BF16),
        compiler_params=_params("parallel", "parallel"),
        name="merge",
    )(oa, ob, oc, gates, gates, gates, wb)


def _residual_epilogue(y, x_ref, gpost_ref, gnext_ref, xo_ref, h_ref):
    xn = x_ref[...] + _rms(y, gpost_ref[...])
    xo_ref[...] = xn
    if h_ref is not None:
        h_ref[...] = _rms(xn, gnext_ref[...]).astype(BF16)


def _out_kernel(m_ref, w_ref, x_ref, gpost_ref, gnext_ref, xo_ref, h_ref):
    y = jnp.dot(m_ref[...], w_ref[...], preferred_element_type=F32)
    _residual_epilogue(y, x_ref, gpost_ref, gnext_ref, xo_ref, h_ref)


def _out_proj(merged, w, x, gpost, gnext):
    t, d = x.shape
    tm = _tile(t, 256)
    row = pl.BlockSpec((tm, d), lambda i: (i, 0))
    gain = pl.BlockSpec((1, d), lambda i: (0, 0))
    return pl.pallas_call(
        _out_kernel,
        grid=(t // tm,),
        in_specs=[row, pl.BlockSpec((d, d), lambda i: (0, 0)), row, gain, gain],
        out_specs=[row, row],
        out_shape=[jax.ShapeDtypeStruct((t, d), F32), jax.ShapeDtypeStruct((t, d), BF16)],
        compiler_params=_params("parallel"),
        name="out_proj",
    )(merged, w, x, gpost.reshape(1, d), gnext.reshape(1, d))


def _ffn_out_kernel(hid_ref, w_ref, x_ref, gpost_ref, gnext_ref, xo_ref, *rest):
    h_ref = rest[0] if len(rest) == 2 else None
    acc_ref = rest[-1]
    k = pl.program_id(1)

    @pl.when(k == 0)
    def _():
        acc_ref[...] = jnp.zeros(acc_ref.shape, F32)

    acc_ref[...] += jnp.dot(hid_ref[...], w_ref[...], preferred_element_type=F32)

    @pl.when(k == pl.num_programs(1) - 1)
    def _():
        _residual_epilogue(acc_ref[...], x_ref, gpost_ref, gnext_ref, xo_ref, h_ref)


def _ffn_out(hid, w, x, gpost, gnext):
    t, d = x.shape
    ff = hid.shape[1]
    tm = _tile(t, 512)
    tk = _tile(ff, 1024)
    row = pl.BlockSpec((tm, d), lambda i, k: (i, 0))
    gain = pl.BlockSpec((1, d), lambda i, k: (0, 0))
    with_next = gnext is not None
    out_specs = [row, row] if with_next else [row]
    out_shape = [jax.ShapeDtypeStruct((t, d), F32)] + ([jax.ShapeDtypeStruct((t, d), BF16)] if with_next else [])
    gn = gnext if with_next else gpost
    outs = pl.pallas_call(
        _ffn_out_kernel,
        grid=(t // tm, ff // tk),
        in_specs=[pl.BlockSpec((tm, tk), lambda i, k: (i, k)),
                  pl.BlockSpec((tk, d), lambda i, k: (k, 0)),
                  row, gain, gain],
        out_specs=out_specs,
        out_shape=out_shape,
        scratch_shapes=[pltpu.VMEM((tm, d), F32)],
        compiler_params=_params("parallel", "arbitrary"),
        name="ffn_out",
    )(hid, w, x, gpost.reshape(1, d), gn.reshape(1, d))
    return (outs[0], outs[1]) if with_next else (outs[0], None)


def _rope_tables(ang):
    cos = jnp.cos(ang)
    sin = jnp.sin(ang)
    return jnp.concatenate([cos, cos], axis=-1), jnp.concatenate([-sin, sin], axis=-1)


def _angles_1d(s):
    t = jnp.arange(s, dtype=F32)
    inv = ROPE_THETA ** (-jnp.arange(0, HEAD_DIM, 2, dtype=F32) / HEAD_DIM)
    return t[:, None] * inv[None, :]


def _angles_axial(s):
    rows = s // GRID_W
    row = jnp.repeat(jnp.arange(rows, dtype=F32), GRID_W)
    col = jnp.tile(jnp.arange(GRID_W, dtype=F32), rows)
    half = HEAD_DIM // 2
    inv = ROPE_THETA ** (-jnp.arange(0, half, 2, dtype=F32) / half)
    return jnp.concatenate([row[:, None] * inv[None, :], col[:, None] * inv[None, :]], axis=-1)


def kernel(x_prompt, x_sample, norm_mix_pre, norm_mix_post, norm_ffn_pre, norm_ffn_post, w_in, q_norm_a, k_norm_a, sink_b, conv_w, conv_b, gate_r_w, gate_r_b, gate_i_w, gate_i_b, lru_lambda, w_branch, w_out, w_ffn_in, w_ffn_out):
    nb_p, seq, d = x_prompt.shape
    nb_s, seq_s, _ = x_sample.shape
    assert seq == seq_s
    nseq = nb_p + nb_s
    depth = w_in.shape[0]
    x = jnp.concatenate([x_prompt, x_sample], axis=0).reshape(nseq * seq, d)

    cos_ax, sin_ax = _rope_tables(_angles_axial(seq))
    cos_1d, sin_1d = _rope_tables(_angles_1d(seq))

    qa0 = 0
    ka0 = qa0 + A_HEADS * HEAD_DIM
    va0 = ka0 + A_KV * HEAD_DIM
    qb0 = va0 + A_KV * HEAD_DIM
    kb0 = qb0 + B_HEADS * HEAD_DIM
    vb0 = kb0 + B_KV * HEAD_DIM
    xc0 = vb0 + B_KV * HEAD_DIM
    g0 = xc0 + 2 * C_WIDTH
    cols = lambda w, a, b: w[:, a:b]

    h = _prenorm(x, norm_mix_pre[0])
    for l in range(depth):
        wl = w_in[l]
        w_qka = jnp.concatenate([cols(wl, qa0, ka0), cols(wl, ka0, va0)], axis=1).astype(BF16)
        w_qkb = jnp.concatenate([cols(wl, qb0, kb0), cols(wl, kb0, vb0)], axis=1).astype(BF16)
        w_v = jnp.concatenate([cols(wl, va0, qb0), cols(wl, vb0, xc0)], axis=1).astype(BF16)
        w_xy = cols(wl, xc0, g0).astype(BF16)
        w_g = wl[:, g0:].astype(BF16)
        gains_a = jnp.concatenate([jnp.tile(q_norm_a[l], A_HEADS), jnp.tile(k_norm_a[l], A_KV)]).reshape(1, -1)

        qk_a = _proj_rope(h, w_qka, gains_a, cos_ax, sin_ax, seq, normed=True)
        qk_b = _proj_rope(h, w_qkb, gains_a, cos_1d, sin_1d, seq, normed=False)
        v_ab = _proj(h, w_v, None, BF16, "proj_v")
        xy = _proj(h, w_xy, None, F32, "proj_xy")
        gates = _proj(h, w_g, "sigmoid", F32, "proj_gates")

        oa = _attn_global(qk_a, v_ab, nseq, seq)
        ob = _attn_window(qk_b, v_ab, sink_b[l], nseq, seq)
        oc = _rglru(xy, conv_w[l], conv_b[l], gate_r_w[l].astype(BF16), gate_r_b[l],
                    gate_i_w[l].astype(BF16), gate_i_b[l], lru_lambda[l], nseq, seq)

        merged = _merge(oa, ob, oc, gates, w_branch[l].astype(BF16))
        x, h2 = _out_proj(merged, w_out[l].astype(BF16), x, norm_mix_post[l], norm_ffn_pre[l])
        hid = _proj(h2, w_ffn_in[l].astype(BF16), "relu2", BF16, "ffn_in")
        gnext = norm_mix_pre[l + 1] if l + 1 < depth else None
        x, h = _ffn_out(hid, w_ffn_out[l].astype(BF16), x, norm_ffn_post[l], gnext)

    y = x.reshape(nseq, seq, d)
    return (y[:nb_p], y[nb_p:])
```

```python
import functools
import math

import jax
import jax.numpy as jnp
from jax import lax
from jax.experimental import pallas as pl
from jax.experimental.pallas import tpu as pltpu

F32 = jnp.float32
BF16 = jnp.bfloat16

HEAD_DIM = 128
A_HEADS = 8
A_KV = 2
B_HEADS = 8
B_KV = 2
C_WIDTH = 1024
C_BLOCKS = 8
CONV_W = 4
LRU_C = 8.0
MIX_W = 1024
N_BRANCH = 3
WINDOW = 128
GRID_W = 64
ROPE_THETA = 10000.0
EPS = 1e-6
NEG_BIG = -1e30

VMEM_LIMIT_BYTES = 56 * 1024 * 1024


def _params(*sem):
    return pltpu.CompilerParams(dimension_semantics=sem, vmem_limit_bytes=VMEM_LIMIT_BYTES)


def _tile(n, pref):
    t = min(n, pref)
    assert n % t == 0, (n, t)
    return t


def _rms(x, g):
    return x * lax.rsqrt(jnp.mean(x * x, axis=-1, keepdims=True) + EPS) * g


def _rope(x, cos, sin_signed):
    return x * cos + pltpu.roll(x, HEAD_DIM // 2, axis=1) * sin_signed


def _prenorm_kernel(x_ref, g_ref, h_ref):
    h_ref[...] = _rms(x_ref[...], g_ref[...]).astype(BF16)


def _prenorm(x, g):
    t, d = x.shape
    tm = _tile(t, 512)
    return pl.pallas_call(
        _prenorm_kernel,
        grid=(t // tm,),
        in_specs=[pl.BlockSpec((tm, d), lambda i: (i, 0)),
                  pl.BlockSpec((1, d), lambda i: (0, 0))],
        out_specs=pl.BlockSpec((tm, d), lambda i: (i, 0)),
        out_shape=jax.ShapeDtypeStruct((t, d), BF16),
        compiler_params=_params("parallel"),
        name="prenorm",
    )(x, g.reshape(1, d))


def _proj_rope_kernel(h_ref, w_ref, g_ref, cos_ref, sin_ref, o_ref, *, normed):
    acc = jnp.dot(h_ref[...], w_ref[...], preferred_element_type=F32)
    cos = cos_ref[...]
    sin = sin_ref[...]
    for hd in range(acc.shape[1] // HEAD_DIM):
        sl = slice(hd * HEAD_DIM, (hd + 1) * HEAD_DIM)
        xh = acc[:, sl]
        if normed:
            xh = _rms(xh, g_ref[:, sl])
        o_ref[:, sl] = _rope(xh, cos, sin).astype(BF16)


def _proj_rope(h, w, gains, cos, sin, seq, normed):
    t, d = h.shape
    n = w.shape[1]
    tm = _tile(seq, 512)
    nsb = seq // tm
    return pl.pallas_call(
        functools.partial(_proj_rope_kernel, normed=normed),
        grid=(t // tm,),
        in_specs=[pl.BlockSpec((tm, d), lambda i: (i, 0)),
                  pl.BlockSpec((d, n), lambda i: (0, 0)),
                  pl.BlockSpec((1, n), lambda i: (0, 0)),
                  pl.BlockSpec((tm, HEAD_DIM), lambda i: (i % nsb, 0)),
                  pl.BlockSpec((tm, HEAD_DIM), lambda i: (i % nsb, 0))],
        out_specs=pl.BlockSpec((tm, n), lambda i: (i, 0)),
        out_shape=jax.ShapeDtypeStruct((t, n), BF16),
        compiler_params=_params("parallel"),
        name="proj_rope_normed" if normed else "proj_rope",
    )(h, w, gains, cos, sin)


def _proj_kernel(h_ref, w_ref, o_ref, *, act):
    acc = jnp.dot(h_ref[...], w_ref[...], preferred_element_type=F32)
    if act == "sigmoid":
        acc = jax.nn.sigmoid(acc)
    elif act == "relu2":
        acc = jnp.square(jnp.maximum(acc, 0.0))
    o_ref[...] = acc.astype(o_ref.dtype)


def _proj(h, w, act, out_dtype, name):
    t, d = h.shape
    n = w.shape[1]
    tm = _tile(t, 512)
    tn = _tile(n, 2048)
    return pl.pallas_call(
        functools.partial(_proj_kernel, act=act),
        grid=(n // tn, t // tm),
        in_specs=[pl.BlockSpec((tm, d), lambda j, i: (i, 0)),
                  pl.BlockSpec((d, tn), lambda j, i: (0, j))],
        out_specs=pl.BlockSpec((tm, tn), lambda j, i: (i, j)),
        out_shape=jax.ShapeDtypeStruct((t, n), out_dtype),
        compiler_params=_params("parallel", "parallel"),
        name=name,
    )(h, w)


def _attn_global_kernel(q_ref, k_ref, v_ref, o_ref, m_ref, acc_ref, s0_ref, s1_ref, *, tq, tk, n_rep):
    seq = k_ref.shape[0]
    nk = seq // tk
    c2 = HEAD_DIM ** -0.5 * math.log2(math.e)
    q = jnp.concatenate([q_ref[:, g * HEAD_DIM:(g + 1) * HEAD_DIM] for g in range(n_rep)], axis=0)
    m_ref[...] = jnp.full(m_ref.shape, NEG_BIG, F32)
    acc_ref[...] = jnp.zeros(acc_ref.shape, F32)

    def scores(c, s_ref):
        off = pl.multiple_of(c * tk, tk)
        s_ref[...] = lax.dot_general(q, k_ref[pl.ds(off, tk), :], (((1,), (1,)), ((), ())),
                                     preferred_element_type=F32)

    def update(c, s_ref):
        off = pl.multiple_of(c * tk, tk)
        v = v_ref[pl.ds(off, tk), :]
        v_ones = jnp.concatenate([v, jnp.ones_like(v)], axis=1)
        s = s_ref[...]
        m_prev = m_ref[...]
        m_new = jnp.maximum(m_prev, jnp.max(s, axis=-1, keepdims=True))
        alpha = jnp.exp2((m_prev - m_new) * c2)
        p = jnp.exp2((s - jnp.concatenate([m_new] * (tk // HEAD_DIM), axis=1)) * c2)
        acc_ref[...] = (jnp.concatenate([alpha, alpha], axis=1) * acc_ref[...]
                        + jnp.dot(p.astype(BF16), v_ones, preferred_element_type=F32))
        m_ref[...] = m_new

    scores(0, s0_ref)

    def pair(i, carry):
        c = 2 * i
        scores(c + 1, s1_ref)
        update(c, s0_ref)
        scores(jnp.minimum(c + 2, nk - 1), s0_ref)
        update(c + 1, s1_ref)
        return carry

    lax.fori_loop(0, nk // 2, pair, 0)
    acc = acc_ref[...]
    o = acc[:, :HEAD_DIM] / acc[:, HEAD_DIM:]
    for g in range(n_rep):
        o_ref[:, g * HEAD_DIM:(g + 1) * HEAD_DIM] = o[g * tq:(g + 1) * tq].astype(BF16)


def _attn_global(qk, v, nseq, seq):
    t = qk.shape[0]
    n_rep = A_HEADS // A_KV
    tq = _tile(seq, 256)
    tk = _tile(seq // 2, 512)
    assert (seq // tk) % 2 == 0
    nqb = seq // tq
    gw = n_rep * HEAD_DIM
    return pl.pallas_call(
        functools.partial(_attn_global_kernel, tq=tq, tk=tk, n_rep=n_rep),
        grid=(nseq, A_KV, nqb),
        in_specs=[pl.BlockSpec((tq, gw), lambda b, h, i: (b * nqb + i, h)),
                  pl.BlockSpec((seq, HEAD_DIM), lambda b, h, i: (b, A_HEADS + h)),
                  pl.BlockSpec((seq, HEAD_DIM), lambda b, h, i: (b, h))],
        out_specs=pl.BlockSpec((tq, gw), lambda b, h, i: (b * nqb + i, h)),
        out_shape=jax.ShapeDtypeStruct((t, A_HEADS * HEAD_DIM), BF16),
        scratch_shapes=[pltpu.VMEM((n_rep * tq, HEAD_DIM), F32),
                        pltpu.VMEM((n_rep * tq, 2 * HEAD_DIM), F32),
                        pltpu.VMEM((n_rep * tq, tk), F32),
                        pltpu.VMEM((n_rep * tq, tk), F32)],
        compiler_params=_params("parallel", "parallel", "parallel"),
        name="attn_global",
    )(qk, qk, v)


def _attn_window_kernel(sink_ref, q_ref, k_ref, v_ref, o_ref, *, tq, n_rep):
    seq = k_ref.shape[0]
    wk = min(seq, tq + 2 * WINDOW)
    c2 = HEAD_DIM ** -0.5 * math.log2(math.e)
    inv_scale = HEAD_DIM ** 0.5
    h = pl.program_id(1)
    i = pl.program_id(2)
    start = pl.multiple_of(jnp.clip(i * tq - WINDOW, 0, seq - wk), HEAD_DIM)
    k = k_ref[pl.ds(start, wk), :]
    v = v_ref[pl.ds(start, wk), :]
    v_ones = jnp.concatenate([v, jnp.ones_like(v)], axis=1)
    qpos = i * tq + lax.broadcasted_iota(jnp.int32, (tq, wk), 0)
    kpos = start + lax.broadcasted_iota(jnp.int32, (tq, wk), 1)
    valid = jnp.abs(qpos - kpos) <= WINDOW
    q = jnp.concatenate([q_ref[:, g * HEAD_DIM:(g + 1) * HEAD_DIM] for g in range(n_rep)], axis=0)
    s = lax.dot_general(q, k, (((1,), (1,)), ((), ())), preferred_element_type=F32)
    s = jnp.where(jnp.concatenate([valid] * n_rep, axis=0), s, NEG_BIG)
    sink = jnp.concatenate([jnp.full((tq, HEAD_DIM), sink_ref[h * n_rep + g] * inv_scale, F32)
                            for g in range(n_rep)], axis=0)
    m = jnp.maximum(jnp.max(s, axis=-1, keepdims=True), sink)
    p = jnp.exp2((s - jnp.concatenate([m] * (wk // HEAD_DIM), axis=1)) * c2)
    acc = jnp.dot(p.astype(BF16), v_ones, preferred_element_type=F32)
    o = acc[:, :HEAD_DIM] / (acc[:, HEAD_DIM:] + jnp.exp2((sink - m) * c2))
    for g in range(n_rep):
        o_ref[:, g * HEAD_DIM:(g + 1) * HEAD_DIM] = o[g * tq:(g + 1) * tq].astype(BF16)


def _attn_window(qk, v, sink, nseq, seq):
    t = qk.shape[0]
    n_rep = B_HEADS // B_KV
    tq = _tile(seq, 256)
    nqb = seq // tq
    gw = n_rep * HEAD_DIM
    return pl.pallas_call(
        functools.partial(_attn_window_kernel, tq=tq, n_rep=n_rep),
        grid=(nseq, B_KV, nqb),
        in_specs=[pl.BlockSpec(memory_space=pltpu.SMEM),
                  pl.BlockSpec((tq, gw), lambda b, h, i: (b * nqb + i, h)),
                  pl.BlockSpec((seq, HEAD_DIM), lambda b, h, i: (b, B_HEADS + h)),
                  pl.BlockSpec((seq, HEAD_DIM), lambda b, h, i: (b, A_KV + h))],
        out_specs=pl.BlockSpec((tq, gw), lambda b, h, i: (b * nqb + i, h)),
        out_shape=jax.ShapeDtypeStruct((t, B_HEADS * HEAD_DIM), BF16),
        compiler_params=_params("parallel", "parallel", "parallel"),
        name="attn_window",
    )(sink, qk, qk, v)


def _group_scan(a, b, reverse):
    rows = a.shape[0]
    a3 = a.reshape(rows // 8, 8, a.shape[1])
    b3 = b.reshape(rows // 8, 8, b.shape[1])
    sub = lax.broadcasted_iota(jnp.int32, a3.shape, 1)
    for sh in (1, 2, 4):
        if reverse:
            a_sh = pltpu.roll(a3, 8 - sh, axis=1)
            b_sh = pltpu.roll(b3, 8 - sh, axis=1)
            ok = sub < 8 - sh
        else:
            a_sh = pltpu.roll(a3, sh, axis=1)
            b_sh = pltpu.roll(b3, sh, axis=1)
            ok = sub >= sh
        b3 = jnp.where(ok, a3 * b_sh + b3, b3)
        a3 = jnp.where(ok, a3 * a_sh, a3)
    return a3.reshape(a.shape), b3.reshape(b.shape)


def _rglru_kernel(x_ref, y_ref, cw_ref, cb_ref, wr_ref, br_ref, wi_ref, bi_ref, lam_ref, o_ref,
                  hf_ref, u_ref, a_ref, b_ref, *, chunk):
    seq = x_ref.shape[0]
    nchunk = seq // chunk
    ngroup = chunk // 8
    left = CONV_W // 2
    lam = lam_ref[...]
    log_sig = jnp.minimum(lam, 0.0) - jnp.log1p(jnp.exp(-jnp.abs(lam)))

    def conv(c):
        t0 = pl.multiple_of(c * chunk, chunk)
        lo = x_ref[pl.ds(jnp.maximum(t0 - 8, 0), 8), :]
        lo = jnp.where(c > 0, lo, 0.0)
        hi = x_ref[pl.ds(jnp.minimum(t0 + chunk, seq - 8), 8), :]
        hi = jnp.where(c < nchunk - 1, hi, 0.0)
        win = jnp.concatenate([lo, x_ref[pl.ds(t0, chunk), :], hi], axis=0)
        u = cb_ref[...]
        for j in range(CONV_W):
            st = 8 - left + j
            u = u + win[st:st + chunk] * cw_ref[j:j + 1, :]
        return t0, u

    def sigmoid(z):
        return 0.5 * jnp.tanh(0.5 * z) + 0.5

    def recurrence_terms(u, d):
        ub = u.astype(BF16)
        r = sigmoid(jnp.dot(ub, wr_ref[d], preferred_element_type=F32) + br_ref[d:d + 1, :])
        gi = sigmoid(jnp.dot(ub, wi_ref[d], preferred_element_type=F32) + bi_ref[d:d + 1, :])
        log_a = LRU_C * r * log_sig[d:d + 1, :]
        a = jnp.exp(log_a)
        th = jnp.tanh(log_a)
        z = -2.0 * th / (1.0 - th)
        drive = z * lax.rsqrt(jnp.maximum(z, 1e-30)) * (gi * u)
        return a, drive

    def fwd_chunk(c, carry):
        t0, u = conv(c)
        u_ref[pl.ds(t0, chunk), :] = u
        a, drive = recurrence_terms(u, 0)
        a_ref[...], b_ref[...] = _group_scan(a, drive, reverse=False)

        def grp(g, h_in):
            r0 = pl.multiple_of(g * 8, 8)
            h = a_ref[pl.ds(r0, 8), :] * h_in + b_ref[pl.ds(r0, 8), :]
            hf_ref[pl.ds(t0 + r0, 8), :] = h
            return h[7:8, :]

        return lax.fori_loop(0, ngroup, grp, carry, unroll=8)

    lax.fori_loop(0, nchunk, fwd_chunk, jnp.zeros((1, x_ref.shape[1]), F32))

    def bwd_chunk(cc, carry):
        c = nchunk - 1 - cc
        t0 = pl.multiple_of(c * chunk, chunk)
        a, drive = recurrence_terms(u_ref[pl.ds(t0, chunk), :], 1)
        a_ref[...], b_ref[...] = _group_scan(a, drive, reverse=True)

        def grp(gg, h_in):
            r0 = pl.multiple_of((ngroup - 1 - gg) * 8, 8)
            h = a_ref[pl.ds(r0, 8), :] * h_in + b_ref[pl.ds(r0, 8), :]
            b_ref[pl.ds(r0, 8), :] = h
            return h[0:1, :]

        carry = lax.fori_loop(0, ngroup, grp, carry, unroll=8)
        hsum = hf_ref[pl.ds(t0, chunk), :] + b_ref[...]
        gate = jax.nn.gelu(y_ref[pl.ds(t0, chunk), :], approximate=True)
        o_ref[pl.ds(t0, chunk), :] = (hsum * gate).astype(BF16)
        return carry

    lax.fori_loop(0, nchunk, bwd_chunk, jnp.zeros((1, x_ref.shape[1]), F32))


def _rglru(xy, conv_w, conv_b, wr, br, wi, bi, lam, nseq, seq):
    t = xy.shape[0]
    bw = C_WIDTH // C_BLOCKS
    chunk = _tile(seq, 512)
    vec = lambda rows: pl.BlockSpec((rows, bw), lambda b, n: (0, n))
    mat = pl.BlockSpec((2, None, bw, bw), lambda b, n: (0, n, 0, 0))
    return pl.pallas_call(
        functools.partial(_rglru_kernel, chunk=chunk),
        grid=(nseq, C_BLOCKS),
        in_specs=[pl.BlockSpec((seq, bw), lambda b, n: (b, n)),
                  pl.BlockSpec((seq, bw), lambda b, n: (b, C_BLOCKS + n)),
                  vec(CONV_W), vec(1), mat, vec(2), mat, vec(2), vec(2)],
        out_specs=pl.BlockSpec((seq, bw), lambda b, n: (b, n)),
        out_shape=jax.ShapeDtypeStruct((t, C_WIDTH), BF16),
        scratch_shapes=[pltpu.VMEM((seq, bw), F32),
                        pltpu.VMEM((seq, bw), F32),
                        pltpu.VMEM((chunk, bw), F32),
                        pltpu.VMEM((chunk, bw), F32)],
        compiler_params=_params("parallel", "parallel"),
        name="rglru",
    )(xy, xy, conv_w, conv_b.reshape(1, C_WIDTH), wr, br, wi, bi, lam)


def _merge_kernel(oa_ref, ob_ref, oc_ref, g0_ref, g1_ref, g2_ref, wb_ref, o_ref):
    acc = g0_ref[...] * jnp.dot(oa_ref[...], wb_ref[0], preferred_element_type=F32)
    acc = acc + g1_ref[...] * jnp.dot(ob_ref[...], wb_ref[1], preferred_element_type=F32)
    acc = acc + g2_ref[...] * jnp.dot(oc_ref[...], wb_ref[2], preferred_element_type=F32)
    o_ref[...] = acc.astype(BF16)


def _merge(oa, ob, oc, gates, wb):
    t, kw = oa.shape
    d = wb.shape[2]
    tm = _tile(t, 512)
    tn = _tile(d, 1024)
    ncb = d // tn
    o_spec = pl.BlockSpec((tm, kw), lambda j, i: (i, 0))
    g_spec = lambda br: pl.BlockSpec((tm, tn), lambda j, i: (i, br * ncb + j))
    return pl.pallas_call(
        _merge_kernel,
        grid=(ncb, t // tm),
        in_specs=[o_spec, o_spec, o_spec, g_spec(0), g_spec(1), g_spec(2),
                  pl.BlockSpec((N_BRANCH, kw, tn), lambda j, i: (0, 0, j))],
        out_specs=pl.BlockSpec((tm, tn), lambda j, i: (i, j)),
        out_shape=jax.ShapeDtypeStruct((t, d), BF16),
        compiler_params=_params("parallel", "parallel"),
        name="merge",
    )(oa, ob, oc, gates, gates, gates, wb)


def _residual_epilogue(y, x_ref, gpost_ref, gnext_ref, xo_ref, h_ref):
    xn = x_ref[...] + _rms(y, gpost_ref[...])
    xo_ref[...] = xn
    if h_ref is not None:
        h_ref[...] = _rms(xn, gnext_ref[...]).astype(BF16)


def _out_kernel(m_ref, w_ref, x_ref, gpost_ref, gnext_ref, xo_ref, h_ref):
    y = jnp.dot(m_ref[...], w_ref[...], preferred_element_type=F32)
    _residual_epilogue(y, x_ref, gpost_ref, gnext_ref, xo_ref, h_ref)


def _out_proj(merged, w, x, gpost, gnext):
    t, d = x.shape
    tm = _tile(t, 512)
    row = pl.BlockSpec((tm, d), lambda i: (i, 0))
    gain = pl.BlockSpec((1, d), lambda i: (0, 0))
    return pl.pallas_call(
        _out_kernel,
        grid=(t // tm,),
        in_specs=[row, pl.BlockSpec((d, d), lambda i: (0, 0)), row, gain, gain],
        out_specs=[row, row],
        out_shape=[jax.ShapeDtypeStruct((t, d), F32), jax.ShapeDtypeStruct((t, d), BF16)],
        compiler_params=_params("parallel"),
        name="out_proj",
    )(merged, w, x, gpost.reshape(1, d), gnext.reshape(1, d))


def _ffn_out_kernel(hid_ref, w_ref, x_ref, gpost_ref, gnext_ref, xo_ref, *rest):
    h_ref = rest[0] if rest else None
    k = pl.program_id(1)

    @pl.when(k == 0)
    def _():
        xo_ref[...] = jnp.dot(hid_ref[...], w_ref[...], preferred_element_type=F32)

    @pl.when(k > 0)
    def _():
        xo_ref[...] += jnp.dot(hid_ref[...], w_ref[...], preferred_element_type=F32)

    @pl.when(k == pl.num_programs(1) - 1)
    def _():
        _residual_epilogue(xo_ref[...], x_ref, gpost_ref, gnext_ref, xo_ref, h_ref)


def _ffn_out(hid, w, x, gpost, gnext):
    t, d = x.shape
    ff = hid.shape[1]
    tm = _tile(t, 1024)
    tk = _tile(ff, 512)
    row = pl.BlockSpec((tm, d), lambda i, k: (i, 0))
    gain = pl.BlockSpec((1, d), lambda i, k: (0, 0))
    with_next = gnext is not None
    out_specs = [row, row] if with_next else [row]
    out_shape = [jax.ShapeDtypeStruct((t, d), F32)] + ([jax.ShapeDtypeStruct((t, d), BF16)] if with_next else [])
    gn = gnext if with_next else gpost
    outs = pl.pallas_call(
        _ffn_out_kernel,
        grid=(t // tm, ff // tk),
        in_specs=[pl.BlockSpec((tm, tk), lambda i, k: (i, k)),
                  pl.BlockSpec((tk, d), lambda i, k: (k, 0)),
                  row, gain, gain],
        out_specs=out_specs,
        out_shape=out_shape,
        compiler_params=_params("parallel", "arbitrary"),
        name="ffn_out",
    )(hid, w, x, gpost.reshape(1, d), gn.reshape(1, d))
    return (outs[0], outs[1]) if with_next else (outs[0], None)


def _rope_tables(ang):
    cos = jnp.cos(ang)
    sin = jnp.sin(ang)
    return jnp.concatenate([cos, cos], axis=-1), jnp.concatenate([-sin, sin], axis=-1)


def _angles_1d(s):
    t = jnp.arange(s, dtype=F32)
    inv = ROPE_THETA ** (-jnp.arange(0, HEAD_DIM, 2, dtype=F32) / HEAD_DIM)
    return t[:, None] * inv[None, :]


def _angles_axial(s):
    rows = s // GRID_W
    row = jnp.repeat(jnp.arange(rows, dtype=F32), GRID_W)
    col = jnp.tile(jnp.arange(GRID_W, dtype=F32), rows)
    half = HEAD_DIM // 2
    inv = ROPE_THETA ** (-jnp.arange(0, half, 2, dtype=F32) / half)
    return jnp.concatenate([row[:, None] * inv[None, :], col[:, None] * inv[None, :]], axis=-1)


def kernel(x_prompt, x_sample, norm_mix_pre, norm_mix_post, norm_ffn_pre, norm_ffn_post, w_in, q_norm_a, k_norm_a, sink_b, conv_w, conv_b, gate_r_w, gate_r_b, gate_i_w, gate_i_b, lru_lambda, w_branch, w_out, w_ffn_in, w_ffn_out):
    nb_p, seq, d = x_prompt.shape
    nb_s, seq_s, _ = x_sample.shape
    assert seq == seq_s
    nseq = nb_p + nb_s
    depth = w_in.shape[0]
    x = jnp.concatenate([x_prompt, x_sample], axis=0).reshape(nseq * seq, d)

    cos_ax, sin_ax = _rope_tables(_angles_axial(seq))
    cos_1d, sin_1d = _rope_tables(_angles_1d(seq))

    qa0 = 0
    ka0 = qa0 + A_HEADS * HEAD_DIM
    va0 = ka0 + A_KV * HEAD_DIM
    qb0 = va0 + A_KV * HEAD_DIM
    kb0 = qb0 + B_HEADS * HEAD_DIM
    vb0 = kb0 + B_KV * HEAD_DIM
    xc0 = vb0 + B_KV * HEAD_DIM
    g0 = xc0 + 2 * C_WIDTH
    cols = lambda w, a, b: w[:, a:b]

    h = _prenorm(x, norm_mix_pre[0])
    for l in range(depth):
        wl = w_in[l]
        w_qka = jnp.concatenate([cols(wl, qa0, ka0), cols(wl, ka0, va0)], axis=1).astype(BF16)
        w_qkb = jnp.concatenate([cols(wl, qb0, kb0), cols(wl, kb0, vb0)], axis=1).astype(BF16)
        w_v = jnp.concatenate([cols(wl, va0, qb0), cols(wl, vb0, xc0)], axis=1).astype(BF16)
        w_xy = cols(wl, xc0, g0).astype(BF16)
        w_g = wl[:, g0:].astype(BF16)
        gains_a = jnp.concatenate([jnp.tile(q_norm_a[l], A_HEADS), jnp.tile(k_norm_a[l], A_KV)]).reshape(1, -1)

        qk_a = _proj_rope(h, w_qka, gains_a, cos_ax, sin_ax, seq, normed=True)
        qk_b = _proj_rope(h, w_qkb, gains_a, cos_1d, sin_1d, seq, normed=False)
        v_ab = _proj(h, w_v, None, BF16, "proj_v")
        xy = _proj(h, w_xy, None, F32, "proj_xy")
        gates = _proj(h, w_g, "sigmoid", F32, "proj_gates")

        oa = _attn_global(qk_a, v_ab, nseq, seq)
        ob = _attn_window(qk_b, v_ab, sink_b[l], nseq, seq)
        oc = _rglru(xy, conv_w[l], conv_b[l], gate_r_w[l].astype(BF16), gate_r_b[l],
                    gate_i_w[l].astype(BF16), gate_i_b[l], lru_lambda[l], nseq, seq)

        merged = _merge(oa, ob, oc, gates, w_branch[l].astype(BF16))
        x, h2 = _out_proj(merged, w_out[l].astype(BF16), x, norm_mix_post[l], norm_ffn_pre[l])
        hid = _proj(h2, w_ffn_in[l].astype(BF16), "relu2", BF16, "ffn_in")
        gnext = norm_mix_pre[l + 1] if l + 1 < depth else None
        x, h = _ffn_out(hid, w_ffn_out[l].astype(BF16), x, norm_ffn_post[l], gnext)

    y = x.reshape(nseq, seq, d)
    return (y[:nb_p], y[nb_p:])
```

```python
import functools
import math

import jax
import jax.numpy as jnp
from jax import lax
from jax.experimental import pallas as pl
from jax.experimental.pallas import tpu as pltpu

F32 = jnp.float32
BF16 = jnp.bfloat16

HEAD_DIM = 128
A_HEADS = 8
A_KV = 2
B_HEADS = 8
B_KV = 2
C_WIDTH = 1024
C_BLOCKS = 8
CONV_W = 4
LRU_C = 8.0
MIX_W = 1024
N_BRANCH = 3
WINDOW = 128
GRID_W = 64
ROPE_THETA = 10000.0
EPS = 1e-6
NEG_BIG = -1e30

VMEM_LIMIT_BYTES = 56 * 1024 * 1024


def _params(*sem):
    return pltpu.CompilerParams(dimension_semantics=sem, vmem_limit_bytes=VMEM_LIMIT_BYTES)


def _tile(n, pref):
    t = min(n, pref)
    assert n % t == 0, (n, t)
    return t


def _rms(x, g):
    return x * lax.rsqrt(jnp.mean(x * x, axis=-1, keepdims=True) + EPS) * g


def _rope(x, cos, sin_signed):
    return x * cos + pltpu.roll(x, HEAD_DIM // 2, axis=1) * sin_signed


def _prenorm_kernel(x_ref, g_ref, h_ref):
    h_ref[...] = _rms(x_ref[...], g_ref[...]).astype(BF16)


def _prenorm(x, g):
    t, d = x.shape
    tm = _tile(t, 512)
    return pl.pallas_call(
        _prenorm_kernel,
        grid=(t // tm,),
        in_specs=[pl.BlockSpec((tm, d), lambda i: (i, 0)),
                  pl.BlockSpec((1, d), lambda i: (0, 0))],
        out_specs=pl.BlockSpec((tm, d), lambda i: (i, 0)),
        out_shape=jax.ShapeDtypeStruct((t, d), BF16),
        compiler_params=_params("parallel"),
        name="prenorm",
    )(x, g.reshape(1, d))


def _proj_rope_kernel(h_ref, w_ref, g_ref, cos_ref, sin_ref, o_ref, *, normed):
    hv = h_ref[...]
    cos = cos_ref[...]
    sin = sin_ref[...]
    for c0 in range(0, o_ref.shape[1], 2 * HEAD_DIM):
        acc = jnp.dot(hv, w_ref[:, c0:c0 + 2 * HEAD_DIM], preferred_element_type=F32)
        for hd in range(2):
            sl = slice(c0 + hd * HEAD_DIM, c0 + (hd + 1) * HEAD_DIM)
            xh = acc[:, hd * HEAD_DIM:(hd + 1) * HEAD_DIM]
            if normed:
                xh = _rms(xh, g_ref[:, sl])
            o_ref[:, sl] = _rope(xh, cos, sin).astype(BF16)


def _proj_rope(h, w, gains, cos, sin, seq, normed):
    t, d = h.shape
    n = w.shape[1]
    tm = _tile(seq, 512)
    nsb = seq // tm
    return pl.pallas_call(
        functools.partial(_proj_rope_kernel, normed=normed),
        grid=(t // tm,),
        in_specs=[pl.BlockSpec((tm, d), lambda i: (i, 0)),
                  pl.BlockSpec((d, n), lambda i: (0, 0)),
                  pl.BlockSpec((1, n), lambda i: (0, 0)),
                  pl.BlockSpec((tm, HEAD_DIM), lambda i: (i % nsb, 0)),
                  pl.BlockSpec((tm, HEAD_DIM), lambda i: (i % nsb, 0))],
        out_specs=pl.BlockSpec((tm, n), lambda i: (i, 0)),
        out_shape=jax.ShapeDtypeStruct((t, n), BF16),
        compiler_params=_params("parallel"),
        name="proj_rope_normed" if normed else "proj_rope",
    )(h, w, gains, cos, sin)


def _proj_kernel(h_ref, w_ref, o_ref, *, act):
    acc = jnp.dot(h_ref[...], w_ref[...], preferred_element_type=F32)
    if act == "sigmoid":
        acc = jax.nn.sigmoid(acc)
    elif act == "relu2":
        acc = jnp.square(jnp.maximum(acc, 0.0))
    o_ref[...] = acc.astype(o_ref.dtype)


def _proj(h, w, act, out_dtype, name):
    t, d = h.shape
    n = w.shape[1]
    tm = _tile(t, 512)
    tn = _tile(n, 2048)
    return pl.pallas_call(
        functools.partial(_proj_kernel, act=act),
        grid=(n // tn, t // tm),
        in_specs=[pl.BlockSpec((tm, d), lambda j, i: (i, 0)),
                  pl.BlockSpec((d, tn), lambda j, i: (0, j))],
        out_specs=pl.BlockSpec((tm, tn), lambda j, i: (i, j)),
        out_shape=jax.ShapeDtypeStruct((t, n), out_dtype),
        compiler_params=_params("parallel", "parallel"),
        name=name,
    )(h, w)


def _attn_global_kernel(q_ref, k_ref, v_ref, o_ref, m_ref, acc_ref, s0_ref, s1_ref, *, tq, tk, n_rep):
    seq = k_ref.shape[0]
    nk = seq // tk
    c2 = HEAD_DIM ** -0.5 * math.log2(math.e)
    q = jnp.concatenate([q_ref[:, g * HEAD_DIM:(g + 1) * HEAD_DIM] for g in range(n_rep)], axis=0)
    m_ref[...] = jnp.full(m_ref.shape, NEG_BIG, F32)
    acc_ref[...] = jnp.zeros(acc_ref.shape, F32)

    def scores(c, s_ref):
        off = pl.multiple_of(c * tk, tk)
        s_ref[...] = lax.dot_general(q, k_ref[pl.ds(off, tk), :], (((1,), (1,)), ((), ())),
                                     preferred_element_type=F32)

    def update(c, s_ref):
        off = pl.multiple_of(c * tk, tk)
        v = v_ref[pl.ds(off, tk), :]
        v_ones = jnp.concatenate([v, jnp.ones_like(v)], axis=1)
        s = s_ref[...]
        m_prev = m_ref[...]
        m_new = jnp.maximum(m_prev, jnp.max(s, axis=-1, keepdims=True))
        alpha = jnp.exp2((m_prev - m_new) * c2)
        p = jnp.exp2((s - jnp.concatenate([m_new] * (tk // HEAD_DIM), axis=1)) * c2)
        acc_ref[...] = (jnp.concatenate([alpha, alpha], axis=1) * acc_ref[...]
                        + jnp.dot(p.astype(BF16), v_ones, preferred_element_type=F32))
        m_ref[...] = m_new

    scores(0, s0_ref)

    def pair(i, carry):
        c = 2 * i
        scores(c + 1, s1_ref)
        update(c, s0_ref)
        scores(jnp.minimum(c + 2, nk - 1), s0_ref)
        update(c + 1, s1_ref)
        return carry

    lax.fori_loop(0, nk // 2, pair, 0)
    acc = acc_ref[...]
    o = acc[:, :HEAD_DIM] / acc[:, HEAD_DIM:]
    for g in range(n_rep):
        o_ref[:, g * HEAD_DIM:(g + 1) * HEAD_DIM] = o[g * tq:(g + 1) * tq].astype(BF16)


def _attn_global(qk, v, nseq, seq):
    t = qk.shape[0]
    n_rep = A_HEADS // A_KV
    tq = _tile(seq, 256)
    tk = _tile(seq // 2, 512)
    assert (seq // tk) % 2 == 0
    nqb = seq // tq
    gw = n_rep * HEAD_DIM
    return pl.pallas_call(
        functools.partial(_attn_global_kernel, tq=tq, tk=tk, n_rep=n_rep),
        grid=(nseq, A_KV, nqb),
        in_specs=[pl.BlockSpec((tq, gw), lambda b, h, i: (b * nqb + i, h)),
                  pl.BlockSpec((seq, HEAD_DIM), lambda b, h, i: (b, A_HEADS + h)),
                  pl.BlockSpec((seq, HEAD_DIM), lambda b, h, i: (b, h))],
        out_specs=pl.BlockSpec((tq, gw), lambda b, h, i: (b * nqb + i, h)),
        out_shape=jax.ShapeDtypeStruct((t, A_HEADS * HEAD_DIM), BF16),
        scratch_shapes=[pltpu.VMEM((n_rep * tq, HEAD_DIM), F32),
                        pltpu.VMEM((n_rep * tq, 2 * HEAD_DIM), F32),
                        pltpu.VMEM((n_rep * tq, tk), F32),
                        pltpu.VMEM((n_rep * tq, tk), F32)],
        compiler_params=_params("parallel", "parallel", "parallel"),
        name="attn_global",
    )(qk, qk, v)


def _attn_window_kernel(sink_ref, q_ref, k_ref, v_ref, o_ref, *, tq, nsub, n_rep):
    seq = k_ref.shape[0]
    wk = min(seq, tq + 2 * WINDOW)
    c2 = HEAD_DIM ** -0.5 * math.log2(math.e)
    inv_scale = HEAD_DIM ** 0.5
    h = pl.program_id(1)
    i = pl.program_id(2)
    sink = jnp.concatenate([jnp.full((tq, HEAD_DIM), sink_ref[h * n_rep + g] * inv_scale, F32)
                            for g in range(n_rep)], axis=0)
    row = lax.broadcasted_iota(jnp.int32, (tq, wk), 0)
    col = lax.broadcasted_iota(jnp.int32, (tq, wk), 1)
    for j in range(nsub):
        q0 = (i * nsub + j) * tq
        start = pl.multiple_of(jnp.clip(q0 - WINDOW, 0, seq - wk), HEAD_DIM)
        k = k_ref[pl.ds(start, wk), :]
        v = v_ref[pl.ds(start, wk), :]
        v_ones = jnp.concatenate([v, jnp.ones_like(v)], axis=1)
        valid = jnp.abs(row - col + (q0 - start)) <= WINDOW
        q = jnp.concatenate([q_ref[j * tq:(j + 1) * tq, g * HEAD_DIM:(g + 1) * HEAD_DIM]
                             for g in range(n_rep)], axis=0)
        s = lax.dot_general(q, k, (((1,), (1,)), ((), ())), preferred_element_type=F32)
        s = jnp.where(jnp.concatenate([valid] * n_rep, axis=0), s, NEG_BIG)
        m = jnp.maximum(jnp.max(s, axis=-1, keepdims=True), sink)
        p = jnp.exp2((s - jnp.concatenate([m] * (wk // HEAD_DIM), axis=1)) * c2)
        acc = jnp.dot(p.astype(BF16), v_ones, preferred_element_type=F32)
        o = acc[:, :HEAD_DIM] / (acc[:, HEAD_DIM:] + jnp.exp2((sink - m) * c2))
        for g in range(n_rep):
            o_ref[j * tq:(j + 1) * tq, g * HEAD_DIM:(g + 1) * HEAD_DIM] = o[g * tq:(g + 1) * tq].astype(BF16)


def _attn_window(qk, v, sink, nseq, seq):
    t = qk.shape[0]
    n_rep = B_HEADS // B_KV
    tq = _tile(seq, 256)
    nsub = 2 if seq % (2 * tq) == 0 else 1
    tb = nsub * tq
    nqb = seq // tb
    gw = n_rep * HEAD_DIM
    return pl.pallas_call(
        functools.partial(_attn_window_kernel, tq=tq, nsub=nsub, n_rep=n_rep),
        grid=(nseq, B_KV, nqb),
        in_specs=[pl.BlockSpec(memory_space=pltpu.SMEM),
                  pl.BlockSpec((tb, gw), lambda b, h, i: (b * nqb + i, h)),
                  pl.BlockSpec((seq, HEAD_DIM), lambda b, h, i: (b, B_HEADS + h)),
                  pl.BlockSpec((seq, HEAD_DIM), lambda b, h, i: (b, A_KV + h))],
        out_specs=pl.BlockSpec((tb, gw), lambda b, h, i: (b * nqb + i, h)),
        out_shape=jax.ShapeDtypeStruct((t, B_HEADS * HEAD_DIM), BF16),
        compiler_params=_params("parallel", "parallel", "parallel"),
        name="attn_window",
    )(sink, qk, qk, v)


N_SEG = 8
LIN_PAD = 8


def _segment_plan(seq):
    m = -(-seq // (4 * N_SEG))
    m += 1 - m % 2
    while True:
        for dv in range(1, m + 1):
            if m % dv == 0 and (16 <= 4 * dv <= 64 or (dv == m and 4 * dv <= 96)):
                assert (N_SEG - 1) * 4 * m < seq <= N_SEG * 4 * m
                return 4 * m, 4 * dv
        m += 2


def _rglru_kernel(x_ref, y_ref, cw_ref, cb_ref, wr_ref, br_ref, wi_ref, bi_ref, lam_ref, o_ref,
                  lin_ref, hf_ref, pf_ref, hb_ref, pb_ref, *, pitch, chunk, lchunk):
    seq, bw = x_ref.shape
    nchunk = pitch // chunk
    rows = chunk * N_SEG
    left = CONV_W // 2
    last_valid = seq - (N_SEG - 1) * pitch
    lin_rows = lin_ref.shape[0]

    lin_ref[0:LIN_PAD, :] = jnp.zeros((LIN_PAD, bw), F32)
    lin_ref[LIN_PAD + seq:lin_rows, :] = jnp.zeros((lin_rows - LIN_PAD - seq, bw), F32)

    def copy_in(c, carry):
        t0 = pl.multiple_of(c * lchunk, lchunk)
        lin_ref[pl.ds(LIN_PAD + t0, lchunk), :] = x_ref[pl.ds(t0, lchunk), :]
        return carry

    lax.fori_loop(0, seq // lchunk, copy_in, 0)

    lam = lam_ref[...]
    log_sig = jnp.minimum(lam, 0.0) - jnp.log1p(jnp.exp(-jnp.abs(lam)))

    def conv(g0):
        u = cb_ref[...]
        for j in range(CONV_W):
            xt = jnp.concatenate(
                [lin_ref[pl.ds(LIN_PAD + g0 + g + j - left, N_SEG, stride=pitch), :] for g in range(chunk)],
                axis=0)
            u = u + xt * cw_ref[j:j + 1, :]
        return u

    def sigmoid(z):
        return 0.5 * jnp.tanh(0.5 * z) + 0.5

    def recurrence_terms(u, d):
        ub = u.astype(BF16)
        r = sigmoid(jnp.dot(ub, wr_ref[d], preferred_element_type=F32) + br_ref[d:d + 1, :])
        gi = sigmoid(jnp.dot(ub, wi_ref[d], preferred_element_type=F32) + bi_ref[d:d + 1, :])
        log_a = LRU_C * r * log_sig[d:d + 1, :]
        a = jnp.exp(log_a)
        th = jnp.tanh(log_a)
        z = -2.0 * th / (1.0 - th)
        drive = z * lax.rsqrt(jnp.maximum(z, 1e-30)) * (gi * u)
        return a, drive

    def scan(a, b, h, p, reverse):
        hs = [None] * chunk
        ps = [None] * chunk
        for g in (range(chunk - 1, -1, -1) if reverse else range(chunk)):
            ag = a[g * N_SEG:(g + 1) * N_SEG]
            h = ag * h + b[g * N_SEG:(g + 1) * N_SEG]
            p = ag * p
            hs[g] = h
            ps[g] = p
        return jnp.concatenate(hs, axis=0), jnp.concatenate(ps, axis=0), h, p

    row = lax.broadcasted_iota(jnp.int32, (rows, bw), 0)
    in_last_segment = (row & (N_SEG - 1)) == N_SEG - 1
    group = row >> 3

    def step(c, carry):
        hf, pf, hb, pb = carry
        g0 = c * chunk
        a, drive = recurrence_terms(conv(g0), 0)
        hh, pp, hf, pf = scan(a, drive, hf, pf, reverse=False)
        r0 = pl.multiple_of(g0 * N_SEG, N_SEG)
        hf_ref[pl.ds(r0, rows), :] = hh
        pf_ref[pl.ds(r0, rows), :] = pp

        g1 = (nchunk - 1 - c) * chunk
        a, drive = recurrence_terms(conv(g1), 1)
        drive = jnp.where(in_last_segment & (group >= last_valid - g1), 0.0, drive)
        hh, pp, hb, pb = scan(a, drive, hb, pb, reverse=True)
        r1 = pl.multiple_of(g1 * N_SEG, N_SEG)
        hb_ref[pl.ds(r1, rows), :] = hh
        pb_ref[pl.ds(r1, rows), :] = pp
        return hf, pf, hb, pb

    zero = jnp.zeros((N_SEG, bw), F32)
    one = jnp.ones((N_SEG, bw), F32)
    hf, pf, hb, pb = lax.fori_loop(0, nchunk, step, (zero, one, zero, one), unroll=2)

    cin = [zero[0:1]]
    for r in range(1, N_SEG):
        cin.append(hf[r - 1:r] + pf[r - 1:r] * cin[r - 1])
    cinb = [zero[0:1]]
    for r in range(N_SEG - 2, -1, -1):
        cinb.append(hb[r + 1:r + 2] + pb[r + 1:r + 2] * cinb[-1])
    cin_f = jnp.concatenate([jnp.concatenate(cin, axis=0)] * chunk, axis=0)
    cin_b = jnp.concatenate([jnp.concatenate(cinb[::-1], axis=0)] * chunk, axis=0)

    def combine(c, carry):
        g0 = c * chunk
        r0 = pl.multiple_of(g0 * N_SEG, N_SEG)
        hsum = ((hf_ref[pl.ds(r0, rows), :] + pf_ref[pl.ds(r0, rows), :] * cin_f)
                + (hb_ref[pl.ds(r0, rows), :] + pb_ref[pl.ds(r0, rows), :] * cin_b))
        for g in range(chunk):
            lin_ref[pl.ds(LIN_PAD + g0 + g, N_SEG, stride=pitch), :] = hsum[g * N_SEG:(g + 1) * N_SEG]
        return carry

    lax.fori_loop(0, nchunk, combine, 0)

    def gate_out(c, carry):
        t0 = pl.multiple_of(c * lchunk, lchunk)
        gate = jax.nn.gelu(y_ref[pl.ds(t0, lchunk), :], approximate=True)
        o_ref[pl.ds(t0, lchunk), :] = (lin_ref[pl.ds(LIN_PAD + t0, lchunk), :] * gate).astype(BF16)
        return carry

    lax.fori_loop(0, seq // lchunk, gate_out, 0)


def _rglru(xy, conv_w, conv_b, wr, br, wi, bi, lam, nseq, seq):
    t = xy.shape[0]
    bw = C_WIDTH // C_BLOCKS
    pitch, chunk = _segment_plan(seq)
    lchunk = _tile(seq, 512)
    vec = lambda rows: pl.BlockSpec((rows, bw), lambda b, n: (0, n))
    mat = pl.BlockSpec((2, None, bw, bw), lambda b, n: (0, n, 0, 0))
    seg_major = pltpu.VMEM((pitch * N_SEG, bw), F32)
    return pl.pallas_call(
        functools.partial(_rglru_kernel, pitch=pitch, chunk=chunk, lchunk=lchunk),
        grid=(nseq, C_BLOCKS),
        in_specs=[pl.BlockSpec((seq, bw), lambda b, n: (b, n)),
                  pl.BlockSpec((seq, bw), lambda b, n: (b, C_BLOCKS + n)),
                  vec(CONV_W), vec(1), mat, vec(2), mat, vec(2), vec(2)],
        out_specs=pl.BlockSpec((seq, bw), lambda b, n: (b, n)),
        out_shape=jax.ShapeDtypeStruct((t, C_WIDTH), BF16),
        scratch_shapes=[pltpu.VMEM((LIN_PAD + pitch * N_SEG + 8, bw), F32),
                        seg_major, seg_major, seg_major, seg_major],
        compiler_params=_params("parallel", "parallel"),
        name="rglru",
    )(xy, xy, conv_w, conv_b.reshape(1, C_WIDTH), wr, br, wi, bi, lam)


def _merge_kernel(oa_ref, ob_ref, oc_ref, g0_ref, g1_ref, g2_ref, wb_ref, o_ref):
    acc = g0_ref[...] * jnp.dot(oa_ref[...], wb_ref[0], preferred_element_type=F32)
    acc = acc + g1_ref[...] * jnp.dot(ob_ref[...], wb_ref[1], preferred_element_type=F32)
    acc = acc + g2_ref[...] * jnp.dot(oc_ref[...], wb_ref[2], preferred_element_type=F32)
    o_ref[...] = acc.astype(BF16)


def _merge(oa, ob, oc, gates, wb):
    t, kw = oa.shape
    d = wb.shape[2]
    tm = _tile(t, 512)
    tn = _tile(d, 1024)
    ncb = d // tn
    o_spec = pl.BlockSpec((tm, kw), lambda j, i: (i, 0))
    g_spec = lambda br: pl.BlockSpec((tm, tn), lambda j, i: (i, br * ncb + j))
    return pl.pallas_call(
        _merge_kernel,
        grid=(ncb, t // tm),
        in_specs=[o_spec, o_spec, o_spec, g_spec(0), g_spec(1), g_spec(2),
                  pl.BlockSpec((N_BRANCH, kw, tn), lambda j, i: (0, 0, j))],
        out_specs=pl.BlockSpec((tm, tn), lambda j, i: (i, j)),
        out_shape=jax.ShapeDtypeStruct((t, d), BF16),
        compiler_params=_params("parallel", "parallel"),
        name="merge",
    )(oa, ob, oc, gates, gates, gates, wb)


def _residual_epilogue(y, x_ref, gpost_ref, gnext_ref, xo_ref, h_ref):
    xn = x_ref[...] + _rms(y, gpost_ref[...])
    xo_ref[...] = xn
    if h_ref is not None:
        h_ref[...] = _rms(xn, gnext_ref[...]).astype(BF16)


def _out_kernel(m_ref, w_ref, x_ref, gpost_ref, gnext_ref, xo_ref, h_ref):
    y = jnp.dot(m_ref[...], w_ref[...], preferred_element_type=F32)
    _residual_epilogue(y, x_ref, gpost_ref, gnext_ref, xo_ref, h_ref)


def _out_proj(merged, w, x, gpost, gnext):
    t, d = x.shape
    tm = _tile(t, 512)
    row = pl.BlockSpec((tm, d), lambda i: (i, 0))
    gain = pl.BlockSpec((1, d), lambda i: (0, 0))
    return pl.pallas_call(
        _out_kernel,
        grid=(t // tm,),
        in_specs=[row, pl.BlockSpec((d, d), lambda i: (0, 0)), row, gain, gain],
        out_specs=[row, row],
        out_shape=[jax.ShapeDtypeStruct((t, d), F32), jax.ShapeDtypeStruct((t, d), BF16)],
        compiler_params=_params("parallel"),
        name="out_proj",
    )(merged, w, x, gpost.reshape(1, d), gnext.reshape(1, d))


def _ffn_out_kernel(hid_ref, w_ref, x_ref, gpost_ref, gnext_ref, xo_ref, *rest):
    h_ref = rest[0] if rest else None
    k = pl.program_id(1)

    @pl.when(k == 0)
    def _():
        xo_ref[...] = jnp.dot(hid_ref[...], w_ref[...], preferred_element_type=F32)

    @pl.when(k > 0)
    def _():
        xo_ref[...] += jnp.dot(hid_ref[...], w_ref[...], preferred_element_type=F32)

    @pl.when(k == pl.num_programs(1) - 1)
    def _():
        _residual_epilogue(xo_ref[...], x_ref, gpost_ref, gnext_ref, xo_ref, h_ref)


def _ffn_out(hid, w, x, gpost, gnext, row_start=0, nrows=None):
    d = x.shape[1]
    t = x.shape[0] if nrows is None else nrows
    ff = hid.shape[1]
    tm = _tile(math.gcd(row_start, t), 1024)
    tk = _tile(ff, 512)
    i0 = row_start // tm
    row_in = pl.BlockSpec((tm, d), lambda i, k: (i0 + i, 0))
    row = pl.BlockSpec((tm, d), lambda i, k: (i, 0))
    gain = pl.BlockSpec((1, d), lambda i, k: (0, 0))
    with_next = gnext is not None
    out_specs = [row, row] if with_next else [row]
    out_shape = [jax.ShapeDtypeStruct((t, d), F32)] + ([jax.ShapeDtypeStruct((t, d), BF16)] if with_next else [])
    gn = gnext if with_next else gpost
    outs = pl.pallas_call(
        _ffn_out_kernel,
        grid=(t // tm, ff // tk),
        in_specs=[pl.BlockSpec((tm, tk), lambda i, k: (i0 + i, k)),
                  pl.BlockSpec((tk, d), lambda i, k: (k, 0)),
                  row_in, gain, gain],
        out_specs=out_specs,
        out_shape=out_shape,
        compiler_params=_params("parallel", "arbitrary"),
        name="ffn_out",
    )(hid, w, x, gpost.reshape(1, d), gn.reshape(1, d))
    return (outs[0], outs[1]) if with_next else (outs[0], None)


def _rope_tables(ang):
    cos = jnp.cos(ang)
    sin = jnp.sin(ang)
    return jnp.concatenate([cos, cos], axis=-1), jnp.concatenate([-sin, sin], axis=-1)


def _angles_1d(s):
    t = jnp.arange(s, dtype=F32)
    inv = ROPE_THETA ** (-jnp.arange(0, HEAD_DIM, 2, dtype=F32) / HEAD_DIM)
    return t[:, None] * inv[None, :]


def _angles_axial(s):
    rows = s // GRID_W
    row = jnp.repeat(jnp.arange(rows, dtype=F32), GRID_W)
    col = jnp.tile(jnp.arange(GRID_W, dtype=F32), rows)
    half = HEAD_DIM // 2
    inv = ROPE_THETA ** (-jnp.arange(0, half, 2, dtype=F32) / half)
    return jnp.concatenate([row[:, None] * inv[None, :], col[:, None] * inv[None, :]], axis=-1)


def kernel(x_prompt, x_sample, norm_mix_pre, norm_mix_post, norm_ffn_pre, norm_ffn_post, w_in, q_norm_a, k_norm_a, sink_b, conv_w, conv_b, gate_r_w, gate_r_b, gate_i_w, gate_i_b, lru_lambda, w_branch, w_out, w_ffn_in, w_ffn_out):
    nb_p, seq, d = x_prompt.shape
    nb_s, seq_s, _ = x_sample.shape
    assert seq == seq_s
    nseq = nb_p + nb_s
    depth = w_in.shape[0]
    x = jnp.concatenate([x_prompt, x_sample], axis=0).reshape(nseq * seq, d)

    cos_ax, sin_ax = _rope_tables(_angles_axial(seq))
    cos_1d, sin_1d = _rope_tables(_angles_1d(seq))

    qa0 = 0
    ka0 = qa0 + A_HEADS * HEAD_DIM
    va0 = ka0 + A_KV * HEAD_DIM
    qb0 = va0 + A_KV * HEAD_DIM
    kb0 = qb0 + B_HEADS * HEAD_DIM
    vb0 = kb0 + B_KV * HEAD_DIM
    xc0 = vb0 + B_KV * HEAD_DIM
    g0 = xc0 + 2 * C_WIDTH
    cols = lambda w, a, b: w[:, a:b]

    h = _prenorm(x, norm_mix_pre[0])
    for l in range(depth):
        wl = w_in[l]
        w_qka = jnp.concatenate([cols(wl, qa0, ka0), cols(wl, ka0, va0)], axis=1).astype(BF16)
        w_qkb = jnp.concatenate([cols(wl, qb0, kb0), cols(wl, kb0, vb0)], axis=1).astype(BF16)
        w_v = jnp.concatenate([cols(wl, va0, qb0), cols(wl, vb0, xc0)], axis=1).astype(BF16)
        w_xy = cols(wl, xc0, g0).astype(BF16)
        w_g = wl[:, g0:].astype(BF16)
        gains_a = jnp.concatenate([jnp.tile(q_norm_a[l], A_HEADS), jnp.tile(k_norm_a[l], A_KV)]).reshape(1, -1)

        qk_a = _proj_rope(h, w_qka, gains_a, cos_ax, sin_ax, seq, normed=True)
        qk_b = _proj_rope(h, w_qkb, gains_a, cos_1d, sin_1d, seq, normed=False)
        v_ab = _proj(h, w_v, None, BF16, "proj_v")
        xy = _proj(h, w_xy, None, F32, "proj_xy")
        gates = _proj(h, w_g, "sigmoid", F32, "proj_gates")

        oa = _attn_global(qk_a, v_ab, nseq, seq)
        ob = _attn_window(qk_b, v_ab, sink_b[l], nseq, seq)
        oc = _rglru(xy, conv_w[l], conv_b[l], gate_r_w[l].astype(BF16), gate_r_b[l],
                    gate_i_w[l].astype(BF16), gate_i_b[l], lru_lambda[l], nseq, seq)

        merged = _merge(oa, ob, oc, gates, w_branch[l].astype(BF16))
        x, h2 = _out_proj(merged, w_out[l].astype(BF16), x, norm_mix_post[l], norm_ffn_pre[l])
        hid = _proj(h2, w_ffn_in[l].astype(BF16), "relu2", BF16, "ffn_in")
        w2 = w_ffn_out[l].astype(BF16)
        if l + 1 < depth:
            x, h = _ffn_out(hid, w2, x, norm_ffn_post[l], norm_mix_pre[l + 1])

    rows_p = nb_p * seq
    y_p, _ = _ffn_out(hid, w2, x, norm_ffn_post[depth - 1], None, 0, rows_p)
    y_s, _ = _ffn_out(hid, w2, x, norm_ffn_post[depth - 1], None, rows_p, nb_s * seq)
    return (y_p.reshape(nb_p, seq, d), y_s.reshape(nb_s, seq, d))
```

```python
import functools
import math

import jax
import jax.numpy as jnp
from jax import lax
from jax.experimental import pallas as pl
from jax.experimental.pallas import tpu as pltpu

F32 = jnp.float32
BF16 = jnp.bfloat16

HEAD_DIM = 128
A_HEADS = 8
A_KV = 2
B_HEADS = 8
B_KV = 2
C_WIDTH = 1024
C_BLOCKS = 8
CONV_W = 4
LRU_C = 8.0
MIX_W = 1024
N_BRANCH = 3
WINDOW = 128
GRID_W = 64
ROPE_THETA = 10000.0
EPS = 1e-6
NEG_BIG = -1e30
ATTN_A_QSCALE = HEAD_DIM ** -0.5 * math.log2(math.e)

VMEM_LIMIT_BYTES = 56 * 1024 * 1024


def _params(*sem):
    return pltpu.CompilerParams(dimension_semantics=sem, vmem_limit_bytes=VMEM_LIMIT_BYTES)


def _tile(n, pref):
    t = min(n, pref)
    assert n % t == 0, (n, t)
    return t


def _rms(x, g):
    return x * lax.rsqrt(jnp.mean(x * x, axis=-1, keepdims=True) + EPS) * g


def _rope(x, cos, sin_signed):
    return x * cos + pltpu.roll(x, HEAD_DIM // 2, axis=1) * sin_signed


def _prenorm_kernel(x_ref, g_ref, h_ref):
    h_ref[...] = _rms(x_ref[...], g_ref[...]).astype(BF16)


def _prenorm(x, g):
    t, d = x.shape
    tm = _tile(t, 512)
    return pl.pallas_call(
        _prenorm_kernel,
        grid=(t // tm,),
        in_specs=[pl.BlockSpec((tm, d), lambda i: (i, 0)),
                  pl.BlockSpec((1, d), lambda i: (0, 0))],
        out_specs=pl.BlockSpec((tm, d), lambda i: (i, 0)),
        out_shape=jax.ShapeDtypeStruct((t, d), BF16),
        compiler_params=_params("parallel"),
        name="prenorm",
    )(x, g.reshape(1, d))


def _proj_rope_kernel(h_ref, w_ref, g_ref, cos_ref, sin_ref, o_ref, *, normed):
    hv = h_ref[...]
    cos = cos_ref[...]
    sin = sin_ref[...]
    for c0 in range(0, o_ref.shape[1], 2 * HEAD_DIM):
        acc = jnp.dot(hv, w_ref[:, c0:c0 + 2 * HEAD_DIM], preferred_element_type=F32)
        for hd in range(2):
            sl = slice(c0 + hd * HEAD_DIM, c0 + (hd + 1) * HEAD_DIM)
            xh = acc[:, hd * HEAD_DIM:(hd + 1) * HEAD_DIM]
            if normed:
                xh = _rms(xh, g_ref[:, sl])
            o_ref[:, sl] = _rope(xh, cos, sin).astype(BF16)


def _proj_rope(h, w, gains, cos, sin, seq, normed):
    t, d = h.shape
    n = w.shape[1]
    tm = _tile(seq, 512)
    nsb = seq // tm
    return pl.pallas_call(
        functools.partial(_proj_rope_kernel, normed=normed),
        grid=(t // tm,),
        in_specs=[pl.BlockSpec((tm, d), lambda i: (i, 0)),
                  pl.BlockSpec((d, n), lambda i: (0, 0)),
                  pl.BlockSpec((1, n), lambda i: (0, 0)),
                  pl.BlockSpec((tm, HEAD_DIM), lambda i: (i % nsb, 0)),
                  pl.BlockSpec((tm, HEAD_DIM), lambda i: (i % nsb, 0))],
        out_specs=pl.BlockSpec((tm, n), lambda i: (i, 0)),
        out_shape=jax.ShapeDtypeStruct((t, n), BF16),
        compiler_params=_params("parallel"),
        name="proj_rope_normed" if normed else "proj_rope",
    )(h, w, gains, cos, sin)


def _proj_kernel(h_ref, w_ref, o_ref, *, act):
    acc = jnp.dot(h_ref[...], w_ref[...], preferred_element_type=F32)
    if act == "sigmoid":
        acc = 0.5 * jnp.tanh(0.5 * acc) + 0.5
    elif act == "relu2":
        acc = jnp.square(jnp.maximum(acc, 0.0))
    o_ref[...] = acc.astype(o_ref.dtype)


def _proj(h, w, act, out_dtype, name):
    t, d = h.shape
    n = w.shape[1]
    tm = _tile(t, 512)
    tn = _tile(n, 2048)
    return pl.pallas_call(
        functools.partial(_proj_kernel, act=act),
        grid=(n // tn, t // tm),
        in_specs=[pl.BlockSpec((tm, d), lambda j, i: (i, 0)),
                  pl.BlockSpec((d, tn), lambda j, i: (0, j))],
        out_specs=pl.BlockSpec((tm, tn), lambda j, i: (i, j)),
        out_shape=jax.ShapeDtypeStruct((t, n), out_dtype),
        compiler_params=_params("parallel", "parallel"),
        name=name,
    )(h, w)


VT_ONES = 16
VT_ROWS = HEAD_DIM + VT_ONES


def _proj_v_kernel(h_ref, w_ref, v_ref, vt_ref):
    acc = jnp.dot(h_ref[...], w_ref[...], preferred_element_type=F32)
    v_ref[...] = acc.astype(BF16)
    ones = jnp.ones((VT_ONES, acc.shape[0]), BF16)
    for hh in range(A_KV):
        vt_ref[hh * VT_ROWS:hh * VT_ROWS + HEAD_DIM, :] = acc[:, hh * HEAD_DIM:(hh + 1) * HEAD_DIM].T.astype(BF16)
        vt_ref[hh * VT_ROWS + HEAD_DIM:(hh + 1) * VT_ROWS, :] = ones


def _proj_v(h, w, tk):
    t, d = h.shape
    n = w.shape[1]
    na = A_KV * VT_ROWS
    return pl.pallas_call(
        _proj_v_kernel,
        grid=(t // tk,),
        in_specs=[pl.BlockSpec((tk, d), lambda i: (i, 0)),
                  pl.BlockSpec((d, n), lambda i: (0, 0))],
        out_specs=[pl.BlockSpec((tk, n), lambda i: (i, 0)),
                   pl.BlockSpec((None, na, tk), lambda i: (i, 0, 0))],
        out_shape=[jax.ShapeDtypeStruct((t, n), BF16),
                   jax.ShapeDtypeStruct((t // tk, na, tk), BF16)],
        compiler_params=_params("parallel"),
        name="proj_v",
    )(h, w)


def _attn_global_kernel(q_ref, k_ref, vt_ref, o_ref, m_ref, acc_ref, s0_ref, s1_ref, *, tq, tk, n_rep):
    seq = k_ref.shape[0]
    nk = seq // tk
    q = jnp.concatenate([q_ref[:, g * HEAD_DIM:(g + 1) * HEAD_DIM] for g in range(n_rep)], axis=0)
    m_ref[...] = jnp.full(m_ref.shape, NEG_BIG, F32)
    acc_ref[...] = jnp.zeros(acc_ref.shape, F32)

    def scores(c, s_ref):
        off = pl.multiple_of(c * tk, tk)
        s_ref[...] = lax.dot_general(k_ref[pl.ds(off, tk), :], q, (((1,), (1,)), ((), ())),
                                     preferred_element_type=F32)

    def update(c, s_ref):
        s = s_ref[...]
        m_prev = m_ref[...]
        m_new = jnp.maximum(m_prev, jnp.max(s, axis=0, keepdims=True))
        alpha = jnp.exp2(m_prev - m_new)
        p = jnp.exp2(s - jnp.concatenate([m_new] * (tk // 8), axis=0))
        acc_ref[...] = (jnp.concatenate([alpha] * (VT_ROWS // 8), axis=0) * acc_ref[...]
                        + jnp.dot(vt_ref[c], p.astype(BF16), preferred_element_type=F32))
        m_ref[...] = m_new

    scores(0, s0_ref)

    def pair(i, carry):
        c = 2 * i
        scores(c + 1, s1_ref)
        update(c, s0_ref)
        scores(jnp.minimum(c + 2, nk - 1), s0_ref)
        update(c + 1, s1_ref)
        return carry

    lax.fori_loop(0, nk // 2, pair, 0, unroll=min(4, nk // 2))
    acc = acc_ref[...]
    denom = jnp.concatenate([acc[HEAD_DIM:HEAD_DIM + 8]] * (HEAD_DIM // 8), axis=0)
    o = (acc[:HEAD_DIM] / denom).T
    for g in range(n_rep):
        o_ref[:, g * HEAD_DIM:(g + 1) * HEAD_DIM] = o[g * tq:(g + 1) * tq].astype(BF16)


def _attn_key_tile(seq):
    tk = _tile(seq // 2, 512)
    assert (seq // tk) % 2 == 0
    return tk


def _attn_global(qk, vt, nseq, seq):
    t = qk.shape[0]
    n_rep = A_HEADS // A_KV
    tq = _tile(seq, 256)
    tk = _attn_key_tile(seq)
    nqb = seq // tq
    gw = n_rep * HEAD_DIM
    return pl.pallas_call(
        functools.partial(_attn_global_kernel, tq=tq, tk=tk, n_rep=n_rep),
        grid=(nseq, A_KV, nqb),
        in_specs=[pl.BlockSpec((tq, gw), lambda b, h, i: (b * nqb + i, h)),
                  pl.BlockSpec((seq, HEAD_DIM), lambda b, h, i: (b, A_HEADS + h)),
                  pl.BlockSpec((seq // tk, VT_ROWS, tk), lambda b, h, i: (b, h, 0))],
        out_specs=pl.BlockSpec((tq, gw), lambda b, h, i: (b * nqb + i, h)),
        out_shape=jax.ShapeDtypeStruct((t, A_HEADS * HEAD_DIM), BF16),
        scratch_shapes=[pltpu.VMEM((8, n_rep * tq), F32),
                        pltpu.VMEM((VT_ROWS, n_rep * tq), F32),
                        pltpu.VMEM((tk, n_rep * tq), F32),
                        pltpu.VMEM((tk, n_rep * tq), F32)],
        compiler_params=_params("parallel", "parallel", "parallel"),
        name="attn_global",
    )(qk, qk, vt)


def _attn_window_kernel(sink_ref, q_ref, k_ref, v_ref, o_ref, *, tq, nsub, n_rep):
    seq = k_ref.shape[0]
    wk = min(seq, tq + 2 * WINDOW)
    c2 = HEAD_DIM ** -0.5 * math.log2(math.e)
    inv_scale = HEAD_DIM ** 0.5
    h = pl.program_id(1)
    i = pl.program_id(2)
    sink = jnp.concatenate([jnp.full((tq, HEAD_DIM), sink_ref[h * n_rep + g] * inv_scale, F32)
                            for g in range(n_rep)], axis=0)
    row = lax.broadcasted_iota(jnp.int32, (tq, wk), 0)
    col = lax.broadcasted_iota(jnp.int32, (tq, wk), 1)
    for j in range(nsub):
        q0 = (i * nsub + j) * tq
        start = pl.multiple_of(jnp.clip(q0 - WINDOW, 0, seq - wk), HEAD_DIM)
        k = k_ref[pl.ds(start, wk), :]
        v = v_ref[pl.ds(start, wk), :]
        v_ones = jnp.concatenate([v, jnp.ones_like(v)], axis=1)
        valid = jnp.abs(row - col + (q0 - start)) <= WINDOW
        q = jnp.concatenate([q_ref[j * tq:(j + 1) * tq, g * HEAD_DIM:(g + 1) * HEAD_DIM]
                             for g in range(n_rep)], axis=0)
        s = lax.dot_general(q, k, (((1,), (1,)), ((), ())), preferred_element_type=F32)
        s = jnp.where(jnp.concatenate([valid] * n_rep, axis=0), s, NEG_BIG)
        m = jnp.maximum(jnp.max(s, axis=-1, keepdims=True), sink)
        p = jnp.exp2((s - jnp.concatenate([m] * (wk // HEAD_DIM), axis=1)) * c2)
        acc = jnp.dot(p.astype(BF16), v_ones, preferred_element_type=F32)
        o = acc[:, :HEAD_DIM] / (acc[:, HEAD_DIM:] + jnp.exp2((sink - m) * c2))
        for g in range(n_rep):
            o_ref[j * tq:(j + 1) * tq, g * HEAD_DIM:(g + 1) * HEAD_DIM] = o[g * tq:(g + 1) * tq].astype(BF16)


def _attn_window(qk, v, sink, nseq, seq):
    t = qk.shape[0]
    n_rep = B_HEADS // B_KV
    tq = _tile(seq, 256)
    nsub = 2 if seq % (2 * tq) == 0 else 1
    tb = nsub * tq
    nqb = seq // tb
    gw = n_rep * HEAD_DIM
    return pl.pallas_call(
        functools.partial(_attn_window_kernel, tq=tq, nsub=nsub, n_rep=n_rep),
        grid=(nseq, B_KV, nqb),
        in_specs=[pl.BlockSpec(memory_space=pltpu.SMEM),
                  pl.BlockSpec((tb, gw), lambda b, h, i: (b * nqb + i, h)),
                  pl.BlockSpec((seq, HEAD_DIM), lambda b, h, i: (b, B_HEADS + h)),
                  pl.BlockSpec((seq, HEAD_DIM), lambda b, h, i: (b, A_KV + h))],
        out_specs=pl.BlockSpec((tb, gw), lambda b, h, i: (b * nqb + i, h)),
        out_shape=jax.ShapeDtypeStruct((t, B_HEADS * HEAD_DIM), BF16),
        compiler_params=_params("parallel", "parallel", "parallel"),
        name="attn_window",
    )(sink, qk, qk, v)


N_SEG = 8
LIN_PAD = 8


def _segment_plan(seq):
    m = -(-seq // (4 * N_SEG))
    m += 1 - m % 2
    while True:
        for dv in range(1, m + 1):
            if m % dv == 0 and (16 <= 4 * dv <= 64 or (dv == m and 4 * dv <= 96)):
                assert (N_SEG - 1) * 4 * m < seq <= N_SEG * 4 * m
                return 4 * m, 4 * dv
        m += 2


def _rglru_kernel(x_ref, y_ref, cw_ref, cb_ref, wr_ref, br_ref, wi_ref, bi_ref, lam_ref, o_ref,
                  lin_ref, hf_ref, pf_ref, hb_ref, pb_ref, *, pitch, chunk, lchunk):
    seq, bw = x_ref.shape
    nchunk = pitch // chunk
    rows = chunk * N_SEG
    left = CONV_W // 2
    last_valid = seq - (N_SEG - 1) * pitch
    lin_rows = lin_ref.shape[0]

    lin_ref[0:LIN_PAD, :] = jnp.zeros((LIN_PAD, bw), F32)
    lin_ref[LIN_PAD + seq:lin_rows, :] = jnp.zeros((lin_rows - LIN_PAD - seq, bw), F32)

    def copy_in(c, carry):
        t0 = pl.multiple_of(c * lchunk, lchunk)
        lin_ref[pl.ds(LIN_PAD + t0, lchunk), :] = x_ref[pl.ds(t0, lchunk), :]
        return carry

    lax.fori_loop(0, seq // lchunk, copy_in, 0)

    lam = lam_ref[...]
    log_sig = jnp.minimum(lam, 0.0) - jnp.log1p(jnp.exp(-jnp.abs(lam)))

    def conv(g0):
        u = cb_ref[...]
        for j in range(CONV_W):
            xt = jnp.concatenate(
                [lin_ref[pl.ds(LIN_PAD + g0 + g + j - left, N_SEG, stride=pitch), :] for g in range(chunk)],
                axis=0)
            u = u + xt * cw_ref[j:j + 1, :]
        return u

    def sigmoid(z):
        return 0.5 * jnp.tanh(0.5 * z) + 0.5

    def recurrence_terms(u, d):
        ub = u.astype(BF16)
        r = sigmoid(jnp.dot(ub, wr_ref[d], preferred_element_type=F32) + br_ref[d:d + 1, :])
        gi = sigmoid(jnp.dot(ub, wi_ref[d], preferred_element_type=F32) + bi_ref[d:d + 1, :])
        log_a = LRU_C * r * log_sig[d:d + 1, :]
        a = jnp.exp(log_a)
        th = jnp.tanh(log_a)
        z = -2.0 * th / (1.0 - th)
        drive = z * lax.rsqrt(jnp.maximum(z, 1e-30)) * (gi * u)
        return a, drive

    def scan(a, b, h, p, reverse):
        hs = [None] * chunk
        ps = [None] * chunk
        for g in (range(chunk - 1, -1, -1) if reverse else range(chunk)):
            ag = a[g * N_SEG:(g + 1) * N_SEG]
            h = ag * h + b[g * N_SEG:(g + 1) * N_SEG]
            p = ag * p
            hs[g] = h
            ps[g] = p
        return jnp.concatenate(hs, axis=0), jnp.concatenate(ps, axis=0), h, p

    row = lax.broadcasted_iota(jnp.int32, (rows, bw), 0)
    in_last_segment = (row & (N_SEG - 1)) == N_SEG - 1
    group = row >> 3

    def step(c, carry):
        hf, pf, hb, pb = carry
        g0 = c * chunk
        a, drive = recurrence_terms(conv(g0), 0)
        hh, pp, hf, pf = scan(a, drive, hf, pf, reverse=False)
        r0 = pl.multiple_of(g0 * N_SEG, N_SEG)
        hf_ref[pl.ds(r0, rows), :] = hh
        pf_ref[pl.ds(r0, rows), :] = pp

        g1 = (nchunk - 1 - c) * chunk
        a, drive = recurrence_terms(conv(g1), 1)
        drive = jnp.where(in_last_segment & (group >= last_valid - g1), 0.0, drive)
        hh, pp, hb, pb = scan(a, drive, hb, pb, reverse=True)
        r1 = pl.multiple_of(g1 * N_SEG, N_SEG)
        hb_ref[pl.ds(r1, rows), :] = hh
        pb_ref[pl.ds(r1, rows), :] = pp
        return hf, pf, hb, pb

    zero = jnp.zeros((N_SEG, bw), F32)
    one = jnp.ones((N_SEG, bw), F32)
    hf, pf, hb, pb = lax.fori_loop(0, nchunk, step, (zero, one, zero, one), unroll=2)

    cin = [zero[0:1]]
    for r in range(1, N_SEG):
        cin.append(hf[r - 1:r] + pf[r - 1:r] * cin[r - 1])
    cinb = [zero[0:1]]
    for r in range(N_SEG - 2, -1, -1):
        cinb.append(hb[r + 1:r + 2] + pb[r + 1:r + 2] * cinb[-1])
    cin_f = jnp.concatenate([jnp.concatenate(cin, axis=0)] * chunk, axis=0)
    cin_b = jnp.concatenate([jnp.concatenate(cinb[::-1], axis=0)] * chunk, axis=0)

    def combine(c, carry):
        g0 = c * chunk
        r0 = pl.multiple_of(g0 * N_SEG, N_SEG)
        hsum = ((hf_ref[pl.ds(r0, rows), :] + pf_ref[pl.ds(r0, rows), :] * cin_f)
                + (hb_ref[pl.ds(r0, rows), :] + pb_ref[pl.ds(r0, rows), :] * cin_b))
        for g in range(chunk):
            lin_ref[pl.ds(LIN_PAD + g0 + g, N_SEG, stride=pitch), :] = hsum[g * N_SEG:(g + 1) * N_SEG]
        return carry

    lax.fori_loop(0, nchunk, combine, 0)

    def gate_out(c, carry):
        t0 = pl.multiple_of(c * lchunk, lchunk)
        gate = jax.nn.gelu(y_ref[pl.ds(t0, lchunk), :], approximate=True)
        o_ref[pl.ds(t0, lchunk), :] = (lin_ref[pl.ds(LIN_PAD + t0, lchunk), :] * gate).astype(BF16)
        return carry

    lax.fori_loop(0, seq // lchunk, gate_out, 0)


def _rglru(xy, conv_w, conv_b, wr, br, wi, bi, lam, nseq, seq):
    t = xy.shape[0]
    bw = C_WIDTH // C_BLOCKS
    pitch, chunk = _segment_plan(seq)
    lchunk = _tile(seq, 512)
    vec = lambda rows: pl.BlockSpec((rows, bw), lambda b, n: (0, n))
    mat = pl.BlockSpec((2, None, bw, bw), lambda b, n: (0, n, 0, 0))
    seg_major = pltpu.VMEM((pitch * N_SEG, bw), F32)
    return pl.pallas_call(
        functools.partial(_rglru_kernel, pitch=pitch, chunk=chunk, lchunk=lchunk),
        grid=(nseq, C_BLOCKS),
        in_specs=[pl.BlockSpec((seq, bw), lambda b, n: (b, n)),
                  pl.BlockSpec((seq, bw), lambda b, n: (b, C_BLOCKS + n)),
                  vec(CONV_W), vec(1), mat, vec(2), mat, vec(2), vec(2)],
        out_specs=pl.BlockSpec((seq, bw), lambda b, n: (b, n)),
        out_shape=jax.ShapeDtypeStruct((t, C_WIDTH), BF16),
        scratch_shapes=[pltpu.VMEM((LIN_PAD + pitch * N_SEG + 8, bw), F32),
                        seg_major, seg_major, seg_major, seg_major],
        compiler_params=_params("parallel", "parallel"),
        name="rglru",
    )(xy, xy, conv_w, conv_b.reshape(1, C_WIDTH), wr, br, wi, bi, lam)


def _merge_kernel(oa_ref, ob_ref, oc_ref, g0_ref, g1_ref, g2_ref, wb_ref, o_ref):
    acc = g0_ref[...] * jnp.dot(oa_ref[...], wb_ref[0], preferred_element_type=F32)
    acc = acc + g1_ref[...] * jnp.dot(ob_ref[...], wb_ref[1], preferred_element_type=F32)
    acc = acc + g2_ref[...] * jnp.dot(oc_ref[...], wb_ref[2], preferred_element_type=F32)
    o_ref[...] = acc.astype(BF16)


def _merge(oa, ob, oc, gates, wb):
    t, kw = oa.shape
    d = wb.shape[2]
    tm = _tile(t, 512)
    tn = _tile(d, 1024)
    ncb = d // tn
    o_spec = pl.BlockSpec((tm, kw), lambda j, i: (i, 0))
    g_spec = lambda br: pl.BlockSpec((tm, tn), lambda j, i: (i, br * ncb + j))
    return pl.pallas_call(
        _merge_kernel,
        grid=(ncb, t // tm),
        in_specs=[o_spec, o_spec, o_spec, g_spec(0), g_spec(1), g_spec(2),
                  pl.BlockSpec((N_BRANCH, kw, tn), lambda j, i: (0, 0, j))],
        out_specs=pl.BlockSpec((tm, tn), lambda j, i: (i, j)),
        out_shape=jax.ShapeDtypeStruct((t, d), BF16),
        compiler_params=_params("parallel", "parallel"),
        name="merge",
    )(oa, ob, oc, gates, gates, gates, wb)


def _residual_epilogue(y, x_ref, gpost_ref, gnext_ref, xo_ref, h_ref):
    xn = x_ref[...] + _rms(y, gpost_ref[...])
    xo_ref[...] = xn
    if h_ref is not None:
        h_ref[...] = _rms(xn, gnext_ref[...]).astype(BF16)


def _out_kernel(m_ref, w_ref, x_ref, gpost_ref, gnext_ref, xo_ref, h_ref):
    y = jnp.dot(m_ref[...], w_ref[...], preferred_element_type=F32)
    _residual_epilogue(y, x_ref, gpost_ref, gnext_ref, xo_ref, h_ref)


def _out_proj(merged, w, x, gpost, gnext):
    t, d = x.shape
    tm = _tile(t, 512)
    row = pl.BlockSpec((tm, d), lambda i: (i, 0))
    gain = pl.BlockSpec((1, d), lambda i: (0, 0))
    return pl.pallas_call(
        _out_kernel,
        grid=(t // tm,),
        in_specs=[row, pl.BlockSpec((d, d), lambda i: (0, 0)), row, gain, gain],
        out_specs=[row, row],
        out_shape=[jax.ShapeDtypeStruct((t, d), F32), jax.ShapeDtypeStruct((t, d), BF16)],
        compiler_params=_params("parallel"),
        name="out_proj",
    )(merged, w, x, gpost.reshape(1, d), gnext.reshape(1, d))


def _ffn_out_kernel(hid_ref, w_ref, x_ref, gpost_ref, gnext_ref, xo_ref, *rest):
    h_ref = rest[0] if rest else None
    k = pl.program_id(1)

    @pl.when(k == 0)
    def _():
        xo_ref[...] = jnp.dot(hid_ref[...], w_ref[...], preferred_element_type=F32)

    @pl.when(k > 0)
    def _():
        xo_ref[...] += jnp.dot(hid_ref[...], w_ref[...], preferred_element_type=F32)

    @pl.when(k == pl.num_programs(1) - 1)
    def _():
        _residual_epilogue(xo_ref[...], x_ref, gpost_ref, gnext_ref, xo_ref, h_ref)


def _ffn_out(hid, w, x, gpost, gnext, row_start=0, nrows=None):
    d = x.shape[1]
    t = x.shape[0] if nrows is None else nrows
    ff = hid.shape[1]
    tm = _tile(math.gcd(row_start, t), 1024)
    tk = _tile(ff, 512)
    i0 = row_start // tm
    row_in = pl.BlockSpec((tm, d), lambda i, k: (i0 + i, 0))
    row = pl.BlockSpec((tm, d), lambda i, k: (i, 0))
    gain = pl.BlockSpec((1, d), lambda i, k: (0, 0))
    with_next = gnext is not None
    out_specs = [row, row] if with_next else [row]
    out_shape = [jax.ShapeDtypeStruct((t, d), F32)] + ([jax.ShapeDtypeStruct((t, d), BF16)] if with_next else [])
    gn = gnext if with_next else gpost
    outs = pl.pallas_call(
        _ffn_out_kernel,
        grid=(t // tm, ff // tk),
        in_specs=[pl.BlockSpec((tm, tk), lambda i, k: (i0 + i, k)),
                  pl.BlockSpec((tk, d), lambda i, k: (k, 0)),
                  row_in, gain, gain],
        out_specs=out_specs,
        out_shape=out_shape,
        compiler_params=_params("parallel", "arbitrary"),
        name="ffn_out",
    )(hid, w, x, gpost.reshape(1, d), gn.reshape(1, d))
    return (outs[0], outs[1]) if with_next else (outs[0], None)


def _rope_tables(ang):
    cos = jnp.cos(ang)
    sin = jnp.sin(ang)
    return jnp.concatenate([cos, cos], axis=-1), jnp.concatenate([-sin, sin], axis=-1)


def _angles_1d(s):
    t = jnp.arange(s, dtype=F32)
    inv = ROPE_THETA ** (-jnp.arange(0, HEAD_DIM, 2, dtype=F32) / HEAD_DIM)
    return t[:, None] * inv[None, :]


def _angles_axial(s):
    rows = s // GRID_W
    row = jnp.repeat(jnp.arange(rows, dtype=F32), GRID_W)
    col = jnp.tile(jnp.arange(GRID_W, dtype=F32), rows)
    half = HEAD_DIM // 2
    inv = ROPE_THETA ** (-jnp.arange(0, half, 2, dtype=F32) / half)
    return jnp.concatenate([row[:, None] * inv[None, :], col[:, None] * inv[None, :]], axis=-1)


def kernel(x_prompt, x_sample, norm_mix_pre, norm_mix_post, norm_ffn_pre, norm_ffn_post, w_in, q_norm_a, k_norm_a, sink_b, conv_w, conv_b, gate_r_w, gate_r_b, gate_i_w, gate_i_b, lru_lambda, w_branch, w_out, w_ffn_in, w_ffn_out):
    nb_p, seq, d = x_prompt.shape
    nb_s, seq_s, _ = x_sample.shape
    assert seq == seq_s
    nseq = nb_p + nb_s
    depth = w_in.shape[0]
    x = jnp.concatenate([x_prompt, x_sample], axis=0).reshape(nseq * seq, d)

    cos_ax, sin_ax = _rope_tables(_angles_axial(seq))
    cos_1d, sin_1d = _rope_tables(_angles_1d(seq))

    qa0 = 0
    ka0 = qa0 + A_HEADS * HEAD_DIM
    va0 = ka0 + A_KV * HEAD_DIM
    qb0 = va0 + A_KV * HEAD_DIM
    kb0 = qb0 + B_HEADS * HEAD_DIM
    vb0 = kb0 + B_KV * HEAD_DIM
    xc0 = vb0 + B_KV * HEAD_DIM
    g0 = xc0 + 2 * C_WIDTH
    cols = lambda w, a, b: w[:, a:b]

    h = _prenorm(x, norm_mix_pre[0])
    for l in range(depth):
        wl = w_in[l]
        w_qka = jnp.concatenate([cols(wl, qa0, ka0), cols(wl, ka0, va0)], axis=1).astype(BF16)
        w_qkb = jnp.concatenate([cols(wl, qb0, kb0), cols(wl, kb0, vb0)], axis=1).astype(BF16)
        w_v = jnp.concatenate([cols(wl, va0, qb0), cols(wl, vb0, xc0)], axis=1).astype(BF16)
        w_xy = cols(wl, xc0, g0).astype(BF16)
        w_g = wl[:, g0:].astype(BF16)
        gains_a = jnp.concatenate([jnp.tile(q_norm_a[l] * ATTN_A_QSCALE, A_HEADS),
                                   jnp.tile(k_norm_a[l], A_KV)]).reshape(1, -1)

        qk_a = _proj_rope(h, w_qka, gains_a, cos_ax, sin_ax, seq, normed=True)
        qk_b = _proj_rope(h, w_qkb, gains_a, cos_1d, sin_1d, seq, normed=False)
        v_ab, vt_a = _proj_v(h, w_v, _attn_key_tile(seq))
        xy = _proj(h, w_xy, None, F32, "proj_xy")
        gates = _proj(h, w_g, "sigmoid", F32, "proj_gates")

        oa = _attn_global(qk_a, vt_a, nseq, seq)
        ob = _attn_window(qk_b, v_ab, sink_b[l], nseq, seq)
        oc = _rglru(xy, conv_w[l], conv_b[l], gate_r_w[l].astype(BF16), gate_r_b[l],
                    gate_i_w[l].astype(BF16), gate_i_b[l], lru_lambda[l], nseq, seq)

        merged = _merge(oa, ob, oc, gates, w_branch[l].astype(BF16))
        x, h2 = _out_proj(merged, w_out[l].astype(BF16), x, norm_mix_post[l], norm_ffn_pre[l])
        hid = _proj(h2, w_ffn_in[l].astype(BF16), "relu2", BF16, "ffn_in")
        w2 = w_ffn_out[l].astype(BF16)
        if l + 1 < depth:
            x, h = _ffn_out(hid, w2, x, norm_ffn_post[l], norm_mix_pre[l + 1])

    rows_p = nb_p * seq
    y_p, _ = _ffn_out(hid, w2, x, norm_ffn_post[depth - 1], None, 0, rows_p)
    y_s, _ = _ffn_out(hid, w2, x, norm_ffn_post[depth - 1], None, rows_p, nb_s * seq)
    return (y_p.reshape(nb_p, seq, d), y_s.reshape(nb_s, seq, d))
```

```python
import functools
import math

import jax
import jax.numpy as jnp
from jax import lax
from jax.experimental import pallas as pl
from jax.experimental.pallas import tpu as pltpu

F32 = jnp.float32
BF16 = jnp.bfloat16

HEAD_DIM = 128
A_HEADS = 8
A_KV = 2
B_HEADS = 8
B_KV = 2
C_WIDTH = 1024
C_BLOCKS = 8
CONV_W = 4
LRU_C = 8.0
MIX_W = 1024
N_BRANCH = 3
WINDOW = 128
GRID_W = 64
ROPE_THETA = 10000.0
EPS = 1e-6
NEG_BIG = -1e30
ATTN_A_QSCALE = HEAD_DIM ** -0.5 * math.log2(math.e)

VMEM_LIMIT_BYTES = 56 * 1024 * 1024


def _params(*sem):
    return pltpu.CompilerParams(dimension_semantics=sem, vmem_limit_bytes=VMEM_LIMIT_BYTES)


def _tile(n, pref):
    t = min(n, pref)
    assert n % t == 0, (n, t)
    return t


def _rms(x, g):
    return x * lax.rsqrt(jnp.mean(x * x, axis=-1, keepdims=True) + EPS) * g


def _rope(x, cos, sin_signed):
    return x * cos + pltpu.roll(x, HEAD_DIM // 2, axis=1) * sin_signed


def _prenorm_kernel(x_ref, g_ref, h_ref):
    h_ref[...] = _rms(x_ref[...], g_ref[...]).astype(BF16)


def _prenorm(x, g):
    t, d = x.shape
    tm = _tile(t, 512)
    return pl.pallas_call(
        _prenorm_kernel,
        grid=(t // tm,),
        in_specs=[pl.BlockSpec((tm, d), lambda i: (i, 0)),
                  pl.BlockSpec((1, d), lambda i: (0, 0))],
        out_specs=pl.BlockSpec((tm, d), lambda i: (i, 0)),
        out_shape=jax.ShapeDtypeStruct((t, d), BF16),
        compiler_params=_params("parallel"),
        name="prenorm",
    )(x, g.reshape(1, d))


def _proj_rope_kernel(h_ref, w_ref, g_ref, cos_ref, sin_ref, o_ref, *, normed, nsplit):
    rows_per = h_ref.shape[0] // nsplit
    for r in range(nsplit):
        rows = slice(r * rows_per, (r + 1) * rows_per)
        acc = jnp.dot(h_ref[rows, :], w_ref[...], preferred_element_type=F32)
        cos = cos_ref[rows, :]
        sin = sin_ref[rows, :]
        for hd in range(acc.shape[1] // HEAD_DIM):
            sl = slice(hd * HEAD_DIM, (hd + 1) * HEAD_DIM)
            xh = acc[:, sl]
            if normed:
                xh = _rms(xh, g_ref[:, sl])
            o_ref[rows, sl] = _rope(xh, cos, sin).astype(BF16)


def _proj_rope(h, w, gains, cos, sin, seq, normed):
    t, d = h.shape
    n = w.shape[1]
    tm = _tile(seq, 512)
    nsb = seq // tm
    return pl.pallas_call(
        functools.partial(_proj_rope_kernel, normed=normed, nsplit=4 if tm % 512 == 0 else 1),
        grid=(t // tm,),
        in_specs=[pl.BlockSpec((tm, d), lambda i: (i, 0)),
                  pl.BlockSpec((d, n), lambda i: (0, 0)),
                  pl.BlockSpec((1, n), lambda i: (0, 0)),
                  pl.BlockSpec((tm, HEAD_DIM), lambda i: (i % nsb, 0)),
                  pl.BlockSpec((tm, HEAD_DIM), lambda i: (i % nsb, 0))],
        out_specs=pl.BlockSpec((tm, n), lambda i: (i, 0)),
        out_shape=jax.ShapeDtypeStruct((t, n), BF16),
        compiler_params=_params("parallel"),
        name="proj_rope_normed" if normed else "proj_rope",
    )(h, w, gains, cos, sin)


def _proj_kernel(h_ref, w_ref, o_ref, *, act):
    acc = jnp.dot(h_ref[...], w_ref[...], preferred_element_type=F32)
    if act == "sigmoid":
        acc = 0.5 * jnp.tanh(0.5 * acc) + 0.5
    elif act == "relu2":
        acc = jnp.square(jnp.maximum(acc, 0.0))
    o_ref[...] = acc.astype(o_ref.dtype)


def _proj(h, w, act, out_dtype, name):
    t, d = h.shape
    n = w.shape[1]
    tm = _tile(t, 512)
    tn = _tile(n, 2048)
    return pl.pallas_call(
        functools.partial(_proj_kernel, act=act),
        grid=(n // tn, t // tm),
        in_specs=[pl.BlockSpec((tm, d), lambda j, i: (i, 0)),
                  pl.BlockSpec((d, tn), lambda j, i: (0, j))],
        out_specs=pl.BlockSpec((tm, tn), lambda j, i: (i, j)),
        out_shape=jax.ShapeDtypeStruct((t, n), out_dtype),
        compiler_params=_params("parallel", "parallel"),
        name=name,
    )(h, w)


VT_ONES = 16
VT_ROWS = HEAD_DIM + VT_ONES


def _proj_v_kernel(h_ref, w_ref, v_ref, vt_ref):
    acc = jnp.dot(h_ref[...], w_ref[...], preferred_element_type=F32)
    v_ref[...] = acc.astype(BF16)
    ones = jnp.ones((VT_ONES, acc.shape[0]), BF16)
    for hh in range(A_KV):
        vt_ref[hh * VT_ROWS:hh * VT_ROWS + HEAD_DIM, :] = acc[:, hh * HEAD_DIM:(hh + 1) * HEAD_DIM].T.astype(BF16)
        vt_ref[hh * VT_ROWS + HEAD_DIM:(hh + 1) * VT_ROWS, :] = ones


def _proj_v(h, w, tk):
    t, d = h.shape
    n = w.shape[1]
    na = A_KV * VT_ROWS
    return pl.pallas_call(
        _proj_v_kernel,
        grid=(t // tk,),
        in_specs=[pl.BlockSpec((tk, d), lambda i: (i, 0)),
                  pl.BlockSpec((d, n), lambda i: (0, 0))],
        out_specs=[pl.BlockSpec((tk, n), lambda i: (i, 0)),
                   pl.BlockSpec((None, na, tk), lambda i: (i, 0, 0))],
        out_shape=[jax.ShapeDtypeStruct((t, n), BF16),
                   jax.ShapeDtypeStruct((t // tk, na, tk), BF16)],
        compiler_params=_params("parallel"),
        name="proj_v",
    )(h, w)


def _attn_global_kernel(q_ref, k_ref, vt_ref, o_ref, m_ref, acc_ref, s0_ref, s1_ref, *, tq, tk, n_rep):
    seq = k_ref.shape[0]
    nk = seq // tk
    q = jnp.concatenate([q_ref[:, g * HEAD_DIM:(g + 1) * HEAD_DIM] for g in range(n_rep)], axis=0)
    m_ref[...] = jnp.full(m_ref.shape, NEG_BIG, F32)
    acc_ref[...] = jnp.zeros(acc_ref.shape, F32)

    def scores(c, s_ref):
        off = pl.multiple_of(c * tk, tk)
        s_ref[...] = lax.dot_general(k_ref[pl.ds(off, tk), :], q, (((1,), (1,)), ((), ())),
                                     preferred_element_type=F32)

    def update(c, s_ref):
        s = s_ref[...]
        m_prev = m_ref[...]
        m_new = jnp.maximum(m_prev, jnp.max(s, axis=0, keepdims=True))
        alpha = jnp.exp2(m_prev - m_new)
        p = jnp.exp2(s - jnp.concatenate([m_new] * (tk // 8), axis=0))
        acc_ref[...] = (jnp.concatenate([alpha] * (VT_ROWS // 8), axis=0) * acc_ref[...]
                        + jnp.dot(vt_ref[c], p.astype(BF16), preferred_element_type=F32))
        m_ref[...] = m_new

    scores(0, s0_ref)

    def pair(i, carry):
        c = 2 * i
        scores(c + 1, s1_ref)
        update(c, s0_ref)
        scores(jnp.minimum(c + 2, nk - 1), s0_ref)
        update(c + 1, s1_ref)
        return carry

    lax.fori_loop(0, nk // 2, pair, 0, unroll=min(4, nk // 2))
    acc = acc_ref[...]
    denom = jnp.concatenate([acc[HEAD_DIM:HEAD_DIM + 8]] * (HEAD_DIM // 8), axis=0)
    o = (acc[:HEAD_DIM] / denom).T
    for g in range(n_rep):
        o_ref[:, g * HEAD_DIM:(g + 1) * HEAD_DIM] = o[g * tq:(g + 1) * tq].astype(BF16)


def _attn_key_tile(seq):
    tk = _tile(seq // 2, 512)
    assert (seq // tk) % 2 == 0
    return tk


def _attn_global(qk, vt, nseq, seq):
    t = qk.shape[0]
    n_rep = A_HEADS // A_KV
    tq = _tile(seq, 256)
    tk = _attn_key_tile(seq)
    nqb = seq // tq
    gw = n_rep * HEAD_DIM
    return pl.pallas_call(
        functools.partial(_attn_global_kernel, tq=tq, tk=tk, n_rep=n_rep),
        grid=(nseq, A_KV, nqb),
        in_specs=[pl.BlockSpec((tq, gw), lambda b, h, i: (b * nqb + i, h)),
                  pl.BlockSpec((seq, HEAD_DIM), lambda b, h, i: (b, A_HEADS + h)),
                  pl.BlockSpec((seq // tk, VT_ROWS, tk), lambda b, h, i: (b, h, 0))],
        out_specs=pl.BlockSpec((tq, gw), lambda b, h, i: (b * nqb + i, h)),
        out_shape=jax.ShapeDtypeStruct((t, A_HEADS * HEAD_DIM), BF16),
        scratch_shapes=[pltpu.VMEM((8, n_rep * tq), F32),
                        pltpu.VMEM((VT_ROWS, n_rep * tq), F32),
                        pltpu.VMEM((tk, n_rep * tq), F32),
                        pltpu.VMEM((tk, n_rep * tq), F32)],
        compiler_params=_params("parallel", "parallel", "parallel"),
        name="attn_global",
    )(qk, qk, vt)


def _attn_window_kernel(sink_ref, q_ref, k_ref, v_ref, o_ref, *, tq, nsub, n_rep):
    seq = k_ref.shape[0]
    wk = min(seq, tq + 2 * WINDOW)
    c2 = HEAD_DIM ** -0.5 * math.log2(math.e)
    inv_scale = HEAD_DIM ** 0.5
    h = pl.program_id(1)
    i = pl.program_id(2)
    sink = jnp.concatenate([jnp.full((tq, HEAD_DIM), sink_ref[h * n_rep + g] * inv_scale, F32)
                            for g in range(n_rep)], axis=0)
    row = lax.broadcasted_iota(jnp.int32, (tq, wk), 0)
    col = lax.broadcasted_iota(jnp.int32, (tq, wk), 1)
    for j in range(nsub):
        q0 = (i * nsub + j) * tq
        start = pl.multiple_of(jnp.clip(q0 - WINDOW, 0, seq - wk), HEAD_DIM)
        k = k_ref[pl.ds(start, wk), :]
        v = v_ref[pl.ds(start, wk), :]
        v_ones = jnp.concatenate([v, jnp.ones_like(v)], axis=1)
        valid = jnp.abs(row - col + (q0 - start)) <= WINDOW
        q = jnp.concatenate([q_ref[j * tq:(j + 1) * tq, g * HEAD_DIM:(g + 1) * HEAD_DIM]
                             for g in range(n_rep)], axis=0)
        s = lax.dot_general(q, k, (((1,), (1,)), ((), ())), preferred_element_type=F32)
        s = jnp.where(jnp.concatenate([valid] * n_rep, axis=0), s, NEG_BIG)
        m = jnp.maximum(jnp.max(s, axis=-1, keepdims=True), sink)
        p = jnp.exp2((s - jnp.concatenate([m] * (wk // HEAD_DIM), axis=1)) * c2)
        acc = jnp.dot(p.astype(BF16), v_ones, preferred_element_type=F32)
        o = acc[:, :HEAD_DIM] / (acc[:, HEAD_DIM:] + jnp.exp2((sink - m) * c2))
        for g in range(n_rep):
            o_ref[j * tq:(j + 1) * tq, g * HEAD_DIM:(g + 1) * HEAD_DIM] = o[g * tq:(g + 1) * tq].astype(BF16)


def _attn_window(qk, v, sink, nseq, seq):
    t = qk.shape[0]
    n_rep = B_HEADS // B_KV
    tq = _tile(seq, 256)
    nsub = 2 if seq % (2 * tq) == 0 else 1
    tb = nsub * tq
    nqb = seq // tb
    gw = n_rep * HEAD_DIM
    return pl.pallas_call(
        functools.partial(_attn_window_kernel, tq=tq, nsub=nsub, n_rep=n_rep),
        grid=(nseq, B_KV, nqb),
        in_specs=[pl.BlockSpec(memory_space=pltpu.SMEM),
                  pl.BlockSpec((tb, gw), lambda b, h, i: (b * nqb + i, h)),
                  pl.BlockSpec((seq, HEAD_DIM), lambda b, h, i: (b, B_HEADS + h)),
                  pl.BlockSpec((seq, HEAD_DIM), lambda b, h, i: (b, A_KV + h))],
        out_specs=pl.BlockSpec((tb, gw), lambda b, h, i: (b * nqb + i, h)),
        out_shape=jax.ShapeDtypeStruct((t, B_HEADS * HEAD_DIM), BF16),
        compiler_params=_params("parallel", "parallel", "parallel"),
        name="attn_window",
    )(sink, qk, qk, v)


N_SEG = 8
LIN_PAD = 8


def _segment_plan(seq):
    m = -(-seq // (4 * N_SEG))
    m += 1 - m % 2
    while True:
        for dv in range(1, m + 1):
            if m % dv == 0 and (16 <= 4 * dv <= 64 or (dv == m and 4 * dv <= 96)):
                assert (N_SEG - 1) * 4 * m < seq <= N_SEG * 4 * m
                return 4 * m, 4 * dv
        m += 2


def _rglru_kernel(x_ref, y_ref, cw_ref, cb_ref, wr_ref, br_ref, wi_ref, bi_ref, lam_ref, o_ref,
                  lin_ref, hf_ref, pf_ref, hb_ref, pb_ref, *, pitch, chunk, lchunk):
    seq, bw = x_ref.shape
    nchunk = pitch // chunk
    rows = chunk * N_SEG
    left = CONV_W // 2
    last_valid = seq - (N_SEG - 1) * pitch
    lin_rows = lin_ref.shape[0]

    lin_ref[0:LIN_PAD, :] = jnp.zeros((LIN_PAD, bw), F32)
    lin_ref[LIN_PAD + seq:lin_rows, :] = jnp.zeros((lin_rows - LIN_PAD - seq, bw), F32)

    def copy_in(c, carry):
        t0 = pl.multiple_of(c * lchunk, lchunk)
        lin_ref[pl.ds(LIN_PAD + t0, lchunk), :] = x_ref[pl.ds(t0, lchunk), :]
        return carry

    lax.fori_loop(0, seq // lchunk, copy_in, 0)

    lam = lam_ref[...]
    log_sig = jnp.minimum(lam, 0.0) - jnp.log1p(jnp.exp(-jnp.abs(lam)))

    def conv(g0):
        u = cb_ref[...]
        for j in range(CONV_W):
            xt = jnp.concatenate(
                [lin_ref[pl.ds(LIN_PAD + g0 + g + j - left, N_SEG, stride=pitch), :] for g in range(chunk)],
                axis=0)
            u = u + xt * cw_ref[j:j + 1, :]
        return u

    def sigmoid(z):
        return 0.5 * jnp.tanh(0.5 * z) + 0.5

    def recurrence_terms(u, d):
        ub = u.astype(BF16)
        r = sigmoid(jnp.dot(ub, wr_ref[d], preferred_element_type=F32) + br_ref[d:d + 1, :])
        gi = sigmoid(jnp.dot(ub, wi_ref[d], preferred_element_type=F32) + bi_ref[d:d + 1, :])
        log_a = LRU_C * r * log_sig[d:d + 1, :]
        a = jnp.exp(log_a)
        th = jnp.tanh(log_a)
        z = -2.0 * th / (1.0 - th)
        drive = z * lax.rsqrt(jnp.maximum(z, 1e-30)) * (gi * u)
        return a, drive

    def scan(a, b, h, p, reverse):
        hs = [None] * chunk
        ps = [None] * chunk
        for g in (range(chunk - 1, -1, -1) if reverse else range(chunk)):
            ag = a[g * N_SEG:(g + 1) * N_SEG]
            h = ag * h + b[g * N_SEG:(g + 1) * N_SEG]
            p = ag * p
            hs[g] = h
            ps[g] = p
        return jnp.concatenate(hs, axis=0), jnp.concatenate(ps, axis=0), h, p

    row = lax.broadcasted_iota(jnp.int32, (rows, bw), 0)
    in_last_segment = (row & (N_SEG - 1)) == N_SEG - 1
    group = row >> 3

    def step(c, carry):
        hf, pf, hb, pb = carry
        g0 = c * chunk
        a, drive = recurrence_terms(conv(g0), 0)
        hh, pp, hf, pf = scan(a, drive, hf, pf, reverse=False)
        r0 = pl.multiple_of(g0 * N_SEG, N_SEG)
        hf_ref[pl.ds(r0, rows), :] = hh
        pf_ref[pl.ds(r0, rows), :] = pp

        g1 = (nchunk - 1 - c) * chunk
        a, drive = recurrence_terms(conv(g1), 1)
        drive = jnp.where(in_last_segment & (group >= last_valid - g1), 0.0, drive)
        hh, pp, hb, pb = scan(a, drive, hb, pb, reverse=True)
        r1 = pl.multiple_of(g1 * N_SEG, N_SEG)
        hb_ref[pl.ds(r1, rows), :] = hh
        pb_ref[pl.ds(r1, rows), :] = pp
        return hf, pf, hb, pb

    zero = jnp.zeros((N_SEG, bw), F32)
    one = jnp.ones((N_SEG, bw), F32)
    hf, pf, hb, pb = lax.fori_loop(0, nchunk, step, (zero, one, zero, one), unroll=2)

    cin = [zero[0:1]]
    for r in range(1, N_SEG):
        cin.append(hf[r - 1:r] + pf[r - 1:r] * cin[r - 1])
    cinb = [zero[0:1]]
    for r in range(N_SEG - 2, -1, -1):
        cinb.append(hb[r + 1:r + 2] + pb[r + 1:r + 2] * cinb[-1])
    cin_f = jnp.concatenate([jnp.concatenate(cin, axis=0)] * chunk, axis=0)
    cin_b = jnp.concatenate([jnp.concatenate(cinb[::-1], axis=0)] * chunk, axis=0)

    def combine(c, carry):
        g0 = c * chunk
        r0 = pl.multiple_of(g0 * N_SEG, N_SEG)
        hsum = ((hf_ref[pl.ds(r0, rows), :] + pf_ref[pl.ds(r0, rows), :] * cin_f)
                + (hb_ref[pl.ds(r0, rows), :] + pb_ref[pl.ds(r0, rows), :] * cin_b))
        for g in range(chunk):
            lin_ref[pl.ds(LIN_PAD + g0 + g, N_SEG, stride=pitch), :] = hsum[g * N_SEG:(g + 1) * N_SEG]
        return carry

    lax.fori_loop(0, nchunk, combine, 0)

    def gate_out(c, carry):
        t0 = pl.multiple_of(c * lchunk, lchunk)
        gate = jax.nn.gelu(y_ref[pl.ds(t0, lchunk), :], approximate=True)
        o_ref[pl.ds(t0, lchunk), :] = (lin_ref[pl.ds(LIN_PAD + t0, lchunk), :] * gate).astype(BF16)
        return carry

    lax.fori_loop(0, seq // lchunk, gate_out, 0)


def _rglru(xy, conv_w, conv_b, wr, br, wi, bi, lam, nseq, seq):
    t = xy.shape[0]
    bw = C_WIDTH // C_BLOCKS
    pitch, chunk = _segment_plan(seq)
    lchunk = _tile(seq, 512)
    vec = lambda rows: pl.BlockSpec((rows, bw), lambda b, n: (0, n))
    mat = pl.BlockSpec((2, None, bw, bw), lambda b, n: (0, n, 0, 0))
    seg_major = pltpu.VMEM((pitch * N_SEG, bw), F32)
    return pl.pallas_call(
        functools.partial(_rglru_kernel, pitch=pitch, chunk=chunk, lchunk=lchunk),
        grid=(nseq, C_BLOCKS),
        in_specs=[pl.BlockSpec((seq, bw), lambda b, n: (b, n)),
                  pl.BlockSpec((seq, bw), lambda b, n: (b, C_BLOCKS + n)),
                  vec(CONV_W), vec(1), mat, vec(2), mat, vec(2), vec(2)],
        out_specs=pl.BlockSpec((seq, bw), lambda b, n: (b, n)),
        out_shape=jax.ShapeDtypeStruct((t, C_WIDTH), BF16),
        scratch_shapes=[pltpu.VMEM((LIN_PAD + pitch * N_SEG + 8, bw), F32),
                        seg_major, seg_major, seg_major, seg_major],
        compiler_params=_params("parallel", "parallel"),
        name="rglru",
    )(xy, xy, conv_w, conv_b.reshape(1, C_WIDTH), wr, br, wi, bi, lam)


def _merge_kernel(oa_ref, ob_ref, oc_ref, g0_ref, g1_ref, g2_ref, wb_ref, o_ref):
    acc = g0_ref[...] * jnp.dot(oa_ref[...], wb_ref[0], preferred_element_type=F32)
    acc = acc + g1_ref[...] * jnp.dot(ob_ref[...], wb_ref[1], preferred_element_type=F32)
    acc = acc + g2_ref[...] * jnp.dot(oc_ref[...], wb_ref[2], preferred_element_type=F32)
    o_ref[...] = acc.astype(BF16)


def _merge(oa, ob, oc, gates, wb):
    t, kw = oa.shape
    d = wb.shape[2]
    tm = _tile(t, 512)
    tn = _tile(d, 1024)
    ncb = d // tn
    o_spec = pl.BlockSpec((tm, kw), lambda j, i: (i, 0))
    g_spec = lambda br: pl.BlockSpec((tm, tn), lambda j, i: (i, br * ncb + j))
    return pl.pallas_call(
        _merge_kernel,
        grid=(ncb, t // tm),
        in_specs=[o_spec, o_spec, o_spec, g_spec(0), g_spec(1), g_spec(2),
                  pl.BlockSpec((N_BRANCH, kw, tn), lambda j, i: (0, 0, j))],
        out_specs=pl.BlockSpec((tm, tn), lambda j, i: (i, j)),
        out_shape=jax.ShapeDtypeStruct((t, d), BF16),
        compiler_params=_params("parallel", "parallel"),
        name="merge",
    )(oa, ob, oc, gates, gates, gates, wb)


def _out_kernel(m_ref, w_ref, x_ref, gpost_ref, gnext_ref, xo_ref, h_ref, *, nsplit):
    rows_per = m_ref.shape[0] // nsplit
    for r in range(nsplit):
        rows = slice(r * rows_per, (r + 1) * rows_per)
        y = jnp.dot(m_ref[rows, :], w_ref[...], preferred_element_type=F32)
        xn = x_ref[rows, :] + _rms(y, gpost_ref[...])
        xo_ref[rows, :] = xn
        h_ref[rows, :] = _rms(xn, gnext_ref[...]).astype(BF16)


def _out_proj(merged, w, x, gpost, gnext):
    t, d = x.shape
    tm = _tile(t, 512)
    row = pl.BlockSpec((tm, d), lambda i: (i, 0))
    gain = pl.BlockSpec((1, d), lambda i: (0, 0))
    return pl.pallas_call(
        functools.partial(_out_kernel, nsplit=4 if tm % 512 == 0 else 1),
        grid=(t // tm,),
        in_specs=[row, pl.BlockSpec((d, d), lambda i: (0, 0)), row, gain, gain],
        out_specs=[row, row],
        out_shape=[jax.ShapeDtypeStruct((t, d), F32), jax.ShapeDtypeStruct((t, d), BF16)],
        compiler_params=_params("parallel"),
        name="out_proj",
    )(merged, w, x, gpost.reshape(1, d), gnext.reshape(1, d))


def _ffn_out_kernel(hid_ref, w_ref, x_ref, gpost_ref, gnext_ref, xo_ref, *rest, nsplit):
    h_ref = rest[0] if len(rest) == 2 else None
    xs_ref = rest[-1]
    k = pl.program_id(1)
    nk, _, slab = xs_ref.shape
    d = xo_ref.shape[1]

    xs_ref[k] = x_ref[...]

    @pl.when((k == 0) & (nk > 1))
    def _():
        xo_ref[...] = jnp.dot(hid_ref[...], w_ref[...], preferred_element_type=F32)

    @pl.when((k > 0) & (k < nk - 1))
    def _():
        xo_ref[...] += jnp.dot(hid_ref[...], w_ref[...], preferred_element_type=F32)

    @pl.when(k == nk - 1)
    def _():
        slabs = [slice(j * slab, (j + 1) * slab) for j in range(nk)]
        rows_per = xo_ref.shape[0] // nsplit
        for r in range(nsplit):
            rows = slice(r * rows_per, (r + 1) * rows_per)
            part = jnp.dot(hid_ref[rows, :], w_ref[...], preferred_element_type=F32)
            y = part if nk == 1 else xo_ref[rows, :] + part
            rs = lax.rsqrt(jnp.sum(jnp.square(y), axis=-1, keepdims=True) * (1.0 / d) + EPS)
            ssq = 0.0
            for j, sl in enumerate(slabs):
                xn = xs_ref[j, rows, :] + y[:, sl] * rs * gpost_ref[:, sl]
                xo_ref[rows, sl] = xn
                ssq = ssq + jnp.sum(jnp.square(xn), axis=-1, keepdims=True)
            if h_ref is not None:
                rs = lax.rsqrt(ssq * (1.0 / d) + EPS)
                for sl in slabs:
                    h_ref[rows, sl] = (xo_ref[rows, sl] * rs * gnext_ref[:, sl]).astype(BF16)


def _ffn_out(hid, w, x, gpost, gnext, row_start=0, nrows=None):
    d = x.shape[1]
    t = x.shape[0] if nrows is None else nrows
    ff = hid.shape[1]
    tm = _tile(math.gcd(row_start, t), 1024)
    tk = _tile(ff, 1024)
    nk = ff // tk
    slab = d // nk
    assert slab % HEAD_DIM == 0
    i0 = row_start // tm
    row_in = pl.BlockSpec((tm, slab), lambda i, k: (i0 + i, k))
    row = pl.BlockSpec((tm, d), lambda i, k: (i, 0))
    gain = pl.BlockSpec((1, d), lambda i, k: (0, 0))
    with_next = gnext is not None
    out_specs = [row, row] if with_next else [row]
    out_shape = [jax.ShapeDtypeStruct((t, d), F32)] + ([jax.ShapeDtypeStruct((t, d), BF16)] if with_next else [])
    gn = gnext if with_next else gpost
    outs = pl.pallas_call(
        functools.partial(_ffn_out_kernel, nsplit=4 if tm % 512 == 0 else 1),
        grid=(t // tm, ff // tk),
        in_specs=[pl.BlockSpec((tm, tk), lambda i, k: (i0 + i, k)),
                  pl.BlockSpec((tk, d), lambda i, k: (k, 0)),
                  row_in, gain, gain],
        out_specs=out_specs,
        out_shape=out_shape,
        scratch_shapes=[pltpu.VMEM((nk, tm, slab), F32)],
        compiler_params=_params("parallel", "arbitrary"),
        name="ffn_out",
    )(hid, w, x, gpost.reshape(1, d), gn.reshape(1, d))
    return (outs[0], outs[1]) if with_next else (outs[0], None)


def _rope_tables(ang):
    cos = jnp.cos(ang)
    sin = jnp.sin(ang)
    return jnp.concatenate([cos, cos], axis=-1), jnp.concatenate([-sin, sin], axis=-1)


def _angles_1d(s):
    t = jnp.arange(s, dtype=F32)
    inv = ROPE_THETA ** (-jnp.arange(0, HEAD_DIM, 2, dtype=F32) / HEAD_DIM)
    return t[:, None] * inv[None, :]


def _angles_axial(s):
    rows = s // GRID_W
    row = jnp.repeat(jnp.arange(rows, dtype=F32), GRID_W)
    col = jnp.tile(jnp.arange(GRID_W, dtype=F32), rows)
    half = HEAD_DIM // 2
    inv = ROPE_THETA ** (-jnp.arange(0, half, 2, dtype=F32) / half)
    return jnp.concatenate([row[:, None] * inv[None, :], col[:, None] * inv[None, :]], axis=-1)


def kernel(x_prompt, x_sample, norm_mix_pre, norm_mix_post, norm_ffn_pre, norm_ffn_post, w_in, q_norm_a, k_norm_a, sink_b, conv_w, conv_b, gate_r_w, gate_r_b, gate_i_w, gate_i_b, lru_lambda, w_branch, w_out, w_ffn_in, w_ffn_out):
    nb_p, seq, d = x_prompt.shape
    nb_s, seq_s, _ = x_sample.shape
    assert seq == seq_s
    nseq = nb_p + nb_s
    depth = w_in.shape[0]
    x = jnp.concatenate([x_prompt, x_sample], axis=0).reshape(nseq * seq, d)

    cos_ax, sin_ax = _rope_tables(_angles_axial(seq))
    cos_1d, sin_1d = _rope_tables(_angles_1d(seq))

    qa0 = 0
    ka0 = qa0 + A_HEADS * HEAD_DIM
    va0 = ka0 + A_KV * HEAD_DIM
    qb0 = va0 + A_KV * HEAD_DIM
    kb0 = qb0 + B_HEADS * HEAD_DIM
    vb0 = kb0 + B_KV * HEAD_DIM
    xc0 = vb0 + B_KV * HEAD_DIM
    g0 = xc0 + 2 * C_WIDTH
    cols = lambda w, a, b: w[:, a:b]

    h = _prenorm(x, norm_mix_pre[0])
    for l in range(depth):
        wl = w_in[l]
        w_qka = jnp.concatenate([cols(wl, qa0, ka0), cols(wl, ka0, va0)], axis=1).astype(BF16)
        w_qkb = jnp.concatenate([cols(wl, qb0, kb0), cols(wl, kb0, vb0)], axis=1).astype(BF16)
        w_v = jnp.concatenate([cols(wl, va0, qb0), cols(wl, vb0, xc0)], axis=1).astype(BF16)
        w_xy = cols(wl, xc0, g0).astype(BF16)
        w_g = wl[:, g0:].astype(BF16)
        gains_a = jnp.concatenate([jnp.tile(q_norm_a[l] * ATTN_A_QSCALE, A_HEADS),
                                   jnp.tile(k_norm_a[l], A_KV)]).reshape(1, -1)

        qk_a = _proj_rope(h, w_qka, gains_a, cos_ax, sin_ax, seq, normed=True)
        qk_b = _proj_rope(h, w_qkb, gains_a, cos_1d, sin_1d, seq, normed=False)
        v_ab, vt_a = _proj_v(h, w_v, _attn_key_tile(seq))
        xy = _proj(h, w_xy, None, F32, "proj_xy")
        gates = _proj(h, w_g, "sigmoid", F32, "proj_gates")

        oa = _attn_global(qk_a, vt_a, nseq, seq)
        ob = _attn_window(qk_b, v_ab, sink_b[l], nseq, seq)
        oc = _rglru(xy, conv_w[l], conv_b[l], gate_r_w[l].astype(BF16), gate_r_b[l],
                    gate_i_w[l].astype(BF16), gate_i_b[l], lru_lambda[l], nseq, seq)

        merged = _merge(oa, ob, oc, gates, w_branch[l].astype(BF16))
        x, h2 = _out_proj(merged, w_out[l].astype(BF16), x, norm_mix_post[l], norm_ffn_pre[l])
        hid = _proj(h2, w_ffn_in[l].astype(BF16), "relu2", BF16, "ffn_in")
        w2 = w_ffn_out[l].astype(BF16)
        if l + 1 < depth:
            x, h = _ffn_out(hid, w2, x, norm_ffn_post[l], norm_mix_pre[l + 1])

    rows_p = nb_p * seq
    y_p, _ = _ffn_out(hid, w2, x, norm_ffn_post[depth - 1], None, 0, rows_p)
    y_s, _ = _ffn_out(hid, w2, x, norm_ffn_post[depth - 1], None, rows_p, nb_s * seq)
    return (y_p.reshape(nb_p, seq, d), y_s.reshape(nb_s, seq, d))
```

```python
import functools
import math

import jax
import jax.numpy as jnp
from jax import lax
from jax.experimental import pallas as pl
from jax.experimental.pallas import tpu as pltpu

F32 = jnp.float32
BF16 = jnp.bfloat16

HEAD_DIM = 128
A_HEADS = 8
A_KV = 2
B_HEADS = 8
B_KV = 2
C_WIDTH = 1024
C_BLOCKS = 8
CONV_W = 4
LRU_C = 8.0
MIX_W = 1024
N_BRANCH = 3
WINDOW = 128
GRID_W = 64
ROPE_THETA = 10000.0
EPS = 1e-6
NEG_BIG = -1e30
ATTN_A_QSCALE = HEAD_DIM ** -0.5 * math.log2(math.e)

VMEM_LIMIT_BYTES = 56 * 1024 * 1024


def _params(*sem):
    return pltpu.CompilerParams(dimension_semantics=sem, vmem_limit_bytes=VMEM_LIMIT_BYTES)


def _tile(n, pref):
    t = min(n, pref)
    assert n % t == 0, (n, t)
    return t


def _rms(x, g):
    return x * lax.rsqrt(jnp.mean(x * x, axis=-1, keepdims=True) + EPS) * g


def _rope(x, cos, sin_signed):
    return x * cos + pltpu.roll(x, HEAD_DIM // 2, axis=1) * sin_signed


def _two_part_rows(xa, xb, tm):
    d = xa.shape[1]
    na = xa.shape[0] // tm
    spec_a = pl.BlockSpec((tm, d), lambda i: (jnp.minimum(i, na - 1), 0))
    spec_b = pl.BlockSpec((tm, d), lambda i: (jnp.maximum(i - na, 0), 0))
    return na, spec_a, spec_b


def _prenorm_kernel(xa_ref, xb_ref, g_ref, h_ref, *, na):
    x = jnp.where(pl.program_id(0) < na, xa_ref[...], xb_ref[...])
    h_ref[...] = _rms(x, g_ref[...]).astype(BF16)


def _prenorm(xa, xb, g):
    d = xa.shape[1]
    t = xa.shape[0] + xb.shape[0]
    tm = _tile(math.gcd(xa.shape[0], xb.shape[0]), 512)
    na, spec_a, spec_b = _two_part_rows(xa, xb, tm)
    return pl.pallas_call(
        functools.partial(_prenorm_kernel, na=na),
        grid=(t // tm,),
        in_specs=[spec_a, spec_b, pl.BlockSpec((1, d), lambda i: (0, 0))],
        out_specs=pl.BlockSpec((tm, d), lambda i: (i, 0)),
        out_shape=jax.ShapeDtypeStruct((t, d), BF16),
        compiler_params=_params("parallel"),
        name="prenorm",
    )(xa, xb, g.reshape(1, d))


def _proj_rope_kernel(h_ref, w_ref, g_ref, cos_ref, sin_ref, o_ref, *, normed, nsplit):
    rows_per = h_ref.shape[0] // nsplit
    for r in range(nsplit):
        rows = slice(r * rows_per, (r + 1) * rows_per)
        acc = jnp.dot(h_ref[rows, :], w_ref[...], preferred_element_type=F32)
        cos = cos_ref[rows, :]
        sin = sin_ref[rows, :]
        for hd in range(acc.shape[1] // HEAD_DIM):
            sl = slice(hd * HEAD_DIM, (hd + 1) * HEAD_DIM)
            xh = acc[:, sl]
            if normed:
                xh = _rms(xh, g_ref[:, sl])
            o_ref[rows, sl] = _rope(xh, cos, sin).astype(BF16)


def _proj_rope(h, w, gains, cos, sin, seq, normed):
    t, d = h.shape
    n = w.shape[1]
    tm = _tile(seq, 512)
    nsb = seq // tm
    return pl.pallas_call(
        functools.partial(_proj_rope_kernel, normed=normed, nsplit=4 if tm % 512 == 0 else 1),
        grid=(t // tm,),
        in_specs=[pl.BlockSpec((tm, d), lambda i: (i, 0)),
                  pl.BlockSpec((d, n), lambda i: (0, 0)),
                  pl.BlockSpec((1, n), lambda i: (0, 0)),
                  pl.BlockSpec((tm, HEAD_DIM), lambda i: (i % nsb, 0)),
                  pl.BlockSpec((tm, HEAD_DIM), lambda i: (i % nsb, 0))],
        out_specs=pl.BlockSpec((tm, n), lambda i: (i, 0)),
        out_shape=jax.ShapeDtypeStruct((t, n), BF16),
        compiler_params=_params("parallel"),
        name="proj_rope_normed" if normed else "proj_rope",
    )(h, w, gains, cos, sin)


def _proj_kernel(h_ref, w_ref, o_ref, *, act):
    acc = jnp.dot(h_ref[...], w_ref[...], preferred_element_type=F32)
    if act == "sigmoid":
        acc = 0.5 * jnp.tanh(0.5 * acc) + 0.5
    elif act == "relu2":
        acc = jnp.square(jnp.maximum(acc, 0.0))
    o_ref[...] = acc.astype(o_ref.dtype)


def _proj(h, w, act, out_dtype, name):
    t, d = h.shape
    n = w.shape[1]
    tm = _tile(t, 512)
    tn = _tile(n, 2048)
    return pl.pallas_call(
        functools.partial(_proj_kernel, act=act),
        grid=(n // tn, t // tm),
        in_specs=[pl.BlockSpec((tm, d), lambda j, i: (i, 0)),
                  pl.BlockSpec((d, tn), lambda j, i: (0, j))],
        out_specs=pl.BlockSpec((tm, tn), lambda j, i: (i, j)),
        out_shape=jax.ShapeDtypeStruct((t, n), out_dtype),
        compiler_params=_params("parallel", "parallel"),
        name=name,
    )(h, w)


VT_ONES = 16
VT_ROWS = HEAD_DIM + VT_ONES


def _proj_v_kernel(h_ref, w_ref, v_ref, vt_ref):
    acc = jnp.dot(h_ref[...], w_ref[...], preferred_element_type=F32)
    v_ref[...] = acc.astype(BF16)
    ones = jnp.ones((VT_ONES, acc.shape[0]), BF16)
    for hh in range(A_KV):
        vt_ref[hh * VT_ROWS:hh * VT_ROWS + HEAD_DIM, :] = acc[:, hh * HEAD_DIM:(hh + 1) * HEAD_DIM].T.astype(BF16)
        vt_ref[hh * VT_ROWS + HEAD_DIM:(hh + 1) * VT_ROWS, :] = ones


def _proj_v(h, w, tk):
    t, d = h.shape
    n = w.shape[1]
    na = A_KV * VT_ROWS
    return pl.pallas_call(
        _proj_v_kernel,
        grid=(t // tk,),
        in_specs=[pl.BlockSpec((tk, d), lambda i: (i, 0)),
                  pl.BlockSpec((d, n), lambda i: (0, 0))],
        out_specs=[pl.BlockSpec((tk, n), lambda i: (i, 0)),
                   pl.BlockSpec((None, na, tk), lambda i: (i, 0, 0))],
        out_shape=[jax.ShapeDtypeStruct((t, n), BF16),
                   jax.ShapeDtypeStruct((t // tk, na, tk), BF16)],
        compiler_params=_params("parallel"),
        name="proj_v",
    )(h, w)


def _attn_global_kernel(q_ref, k_ref, vt_ref, o_ref, qt_ref, m_ref, acc_ref, s0_ref, s1_ref, *, tq, tk, n_rep):
    seq = k_ref.shape[0]
    nk = seq // tk
    qt_ref[...] = jnp.concatenate(
        [q_ref[:, g * HEAD_DIM:(g + 1) * HEAD_DIM].astype(F32).T for g in range(n_rep)], axis=1).astype(BF16)
    m_ref[...] = jnp.full(m_ref.shape, NEG_BIG, F32)
    acc_ref[...] = jnp.zeros(acc_ref.shape, F32)

    def scores(c, s_ref):
        off = pl.multiple_of(c * tk, tk)
        s_ref[...] = jnp.dot(k_ref[pl.ds(off, tk), :], qt_ref[...], preferred_element_type=F32)

    def update(c, s_ref):
        s = s_ref[...]
        m_prev = m_ref[...]
        m_new = jnp.maximum(m_prev, jnp.max(s, axis=0, keepdims=True))
        alpha = jnp.exp2(m_prev - m_new)
        p = jnp.exp2(s - jnp.concatenate([m_new] * (tk // 8), axis=0))
        acc_ref[...] = (jnp.concatenate([alpha] * (VT_ROWS // 8), axis=0) * acc_ref[...]
                        + jnp.dot(vt_ref[c], p.astype(BF16), preferred_element_type=F32))
        m_ref[...] = m_new

    scores(0, s0_ref)

    def pair(i, carry):
        c = 2 * i
        scores(c + 1, s1_ref)
        update(c, s0_ref)
        scores(jnp.minimum(c + 2, nk - 1), s0_ref)
        update(c + 1, s1_ref)
        return carry

    lax.fori_loop(0, nk // 2, pair, 0, unroll=min(4, nk // 2))
    acc = acc_ref[...]
    denom = jnp.concatenate([acc[HEAD_DIM:HEAD_DIM + 8]] * (HEAD_DIM // 8), axis=0)
    o = (acc[:HEAD_DIM] / denom).T
    for g in range(n_rep):
        o_ref[:, g * HEAD_DIM:(g + 1) * HEAD_DIM] = o[g * tq:(g + 1) * tq].astype(BF16)


def _attn_key_tile(seq):
    tk = _tile(seq // 2, 512)
    assert (seq // tk) % 2 == 0
    return tk


def _attn_global(qk, vt, nseq, seq):
    t = qk.shape[0]
    n_rep = A_HEADS // A_KV
    tq = _tile(seq, 256)
    tk = _attn_key_tile(seq)
    nqb = seq // tq
    gw = n_rep * HEAD_DIM
    return pl.pallas_call(
        functools.partial(_attn_global_kernel, tq=tq, tk=tk, n_rep=n_rep),
        grid=(nseq, A_KV, nqb),
        in_specs=[pl.BlockSpec((tq, gw), lambda b, h, i: (b * nqb + i, h)),
                  pl.BlockSpec((seq, HEAD_DIM), lambda b, h, i: (b, A_HEADS + h)),
                  pl.BlockSpec((seq // tk, VT_ROWS, tk), lambda b, h, i: (b, h, 0))],
        out_specs=pl.BlockSpec((tq, gw), lambda b, h, i: (b * nqb + i, h)),
        out_shape=jax.ShapeDtypeStruct((t, A_HEADS * HEAD_DIM), BF16),
        scratch_shapes=[pltpu.VMEM((HEAD_DIM, n_rep * tq), BF16),
                        pltpu.VMEM((8, n_rep * tq), F32),
                        pltpu.VMEM((VT_ROWS, n_rep * tq), F32),
                        pltpu.VMEM((tk, n_rep * tq), F32),
                        pltpu.VMEM((tk, n_rep * tq), F32)],
        compiler_params=_params("parallel", "parallel", "parallel"),
        name="attn_global",
    )(qk, qk, vt)


def _attn_window_kernel(sink_ref, q_ref, k_ref, v_ref, o_ref, *, tq, nsub, n_rep):
    seq = k_ref.shape[0]
    wk = min(seq, tq + 2 * WINDOW)
    c2 = HEAD_DIM ** -0.5 * math.log2(math.e)
    inv_scale = HEAD_DIM ** 0.5
    h = pl.program_id(1)
    i = pl.program_id(2)
    sink = jnp.concatenate([jnp.full((tq, HEAD_DIM), sink_ref[h * n_rep + g] * inv_scale, F32)
                            for g in range(n_rep)], axis=0)
    row = lax.broadcasted_iota(jnp.int32, (tq, wk), 0)
    col = lax.broadcasted_iota(jnp.int32, (tq, wk), 1)
    for j in range(nsub):
        q0 = (i * nsub + j) * tq
        start = pl.multiple_of(jnp.clip(q0 - WINDOW, 0, seq - wk), HEAD_DIM)
        k = k_ref[pl.ds(start, wk), :]
        v = v_ref[pl.ds(start, wk), :]
        v_ones = jnp.concatenate([v, jnp.ones_like(v)], axis=1)
        valid = jnp.abs(row - col + (q0 - start)) <= WINDOW
        q = jnp.concatenate([q_ref[j * tq:(j + 1) * tq, g * HEAD_DIM:(g + 1) * HEAD_DIM]
                             for g in range(n_rep)], axis=0)
        s = lax.dot_general(q, k, (((1,), (1,)), ((), ())), preferred_element_type=F32)
        s = jnp.where(jnp.concatenate([valid] * n_rep, axis=0), s, NEG_BIG)
        m = jnp.maximum(jnp.max(s, axis=-1, keepdims=True), sink)
        p = jnp.exp2((s - jnp.concatenate([m] * (wk // HEAD_DIM), axis=1)) * c2)
        acc = jnp.dot(p.astype(BF16), v_ones, preferred_element_type=F32)
        o = acc[:, :HEAD_DIM] / (acc[:, HEAD_DIM:] + jnp.exp2((sink - m) * c2))
        for g in range(n_rep):
            o_ref[j * tq:(j + 1) * tq, g * HEAD_DIM:(g + 1) * HEAD_DIM] = o[g * tq:(g + 1) * tq].astype(BF16)


def _attn_window(qk, v, sink, nseq, seq):
    t = qk.shape[0]
    n_rep = B_HEADS // B_KV
    tq = _tile(seq, 256)
    nsub = 2 if seq % (2 * tq) == 0 else 1
    tb = nsub * tq
    nqb = seq // tb
    gw = n_rep * HEAD_DIM
    return pl.pallas_call(
        functools.partial(_attn_window_kernel, tq=tq, nsub=nsub, n_rep=n_rep),
        grid=(nseq, B_KV, nqb),
        in_specs=[pl.BlockSpec(memory_space=pltpu.SMEM),
                  pl.BlockSpec((tb, gw), lambda b, h, i: (b * nqb + i, h)),
                  pl.BlockSpec((seq, HEAD_DIM), lambda b, h, i: (b, B_HEADS + h)),
                  pl.BlockSpec((seq, HEAD_DIM), lambda b, h, i: (b, A_KV + h))],
        out_specs=pl.BlockSpec((tb, gw), lambda b, h, i: (b * nqb + i, h)),
        out_shape=jax.ShapeDtypeStruct((t, B_HEADS * HEAD_DIM), BF16),
        compiler_params=_params("parallel", "parallel", "parallel"),
        name="attn_window",
    )(sink, qk, qk, v)


N_SEG = 8
LIN_PAD = 8


def _segment_plan(seq):
    m = -(-seq // (4 * N_SEG))
    m += 1 - m % 2
    while True:
        for dv in range(1, m + 1):
            if m % dv == 0 and (16 <= 4 * dv <= 64 or (dv == m and 4 * dv <= 96)):
                assert (N_SEG - 1) * 4 * m < seq <= N_SEG * 4 * m
                return 4 * m, 4 * dv
        m += 2


def _rglru_kernel(x_ref, y_ref, cw_ref, cb_ref, wr_ref, br_ref, wi_ref, bi_ref, lam_ref, o_ref,
                  lin_ref, hf_ref, pf_ref, hb_ref, pb_ref, *, pitch, chunk, lchunk):
    seq, bw = x_ref.shape
    nchunk = pitch // chunk
    rows = chunk * N_SEG
    left = CONV_W // 2
    last_valid = seq - (N_SEG - 1) * pitch
    lin_rows = lin_ref.shape[0]

    lin_ref[0:LIN_PAD, :] = jnp.zeros((LIN_PAD, bw), F32)
    lin_ref[LIN_PAD + seq:lin_rows, :] = jnp.zeros((lin_rows - LIN_PAD - seq, bw), F32)

    def copy_in(c, carry):
        t0 = pl.multiple_of(c * lchunk, lchunk)
        lin_ref[pl.ds(LIN_PAD + t0, lchunk), :] = x_ref[pl.ds(t0, lchunk), :]
        return carry

    lax.fori_loop(0, seq // lchunk, copy_in, 0)

    lam = lam_ref[...]
    log_sig = jnp.minimum(lam, 0.0) - jnp.log1p(jnp.exp(-jnp.abs(lam)))

    def conv(g0):
        u = cb_ref[...]
        for j in range(CONV_W):
            xt = jnp.concatenate(
                [lin_ref[pl.ds(LIN_PAD + g0 + g + j - left, N_SEG, stride=pitch), :] for g in range(chunk)],
                axis=0)
            u = u + xt * cw_ref[j:j + 1, :]
        return u

    def sigmoid(z):
        return 0.5 * jnp.tanh(0.5 * z) + 0.5

    def recurrence_terms(u, d):
        ub = u.astype(BF16)
        r = sigmoid(jnp.dot(ub, wr_ref[d], preferred_element_type=F32) + br_ref[d:d + 1, :])
        gi = sigmoid(jnp.dot(ub, wi_ref[d], preferred_element_type=F32) + bi_ref[d:d + 1, :])
        log_a = LRU_C * r * log_sig[d:d + 1, :]
        a = jnp.exp(log_a)
        th = jnp.tanh(log_a)
        z = -2.0 * th / (1.0 - th)
        drive = z * lax.rsqrt(jnp.maximum(z, 1e-30)) * (gi * u)
        return a, drive

    def scan(a, b, h, p, reverse):
        hs = [None] * chunk
        ps = [None] * chunk
        for g in (range(chunk - 1, -1, -1) if reverse else range(chunk)):
            ag = a[g * N_SEG:(g + 1) * N_SEG]
            h = ag * h + b[g * N_SEG:(g + 1) * N_SEG]
            p = ag * p
            hs[g] = h
            ps[g] = p
        return jnp.concatenate(hs, axis=0), jnp.concatenate(ps, axis=0), h, p

    row = lax.broadcasted_iota(jnp.int32, (rows, bw), 0)
    in_last_segment = (row & (N_SEG - 1)) == N_SEG - 1
    group = row >> 3

    def step(c, carry):
        hf, pf, hb, pb = carry
        g0 = c * chunk
        a, drive = recurrence_terms(conv(g0), 0)
        hh, pp, hf, pf = scan(a, drive, hf, pf, reverse=False)
        r0 = pl.multiple_of(g0 * N_SEG, N_SEG)
        hf_ref[pl.ds(r0, rows), :] = hh
        pf_ref[pl.ds(r0, rows), :] = pp

        g1 = (nchunk - 1 - c) * chunk
        a, drive = recurrence_terms(conv(g1), 1)
        drive = jnp.where(in_last_segment & (group >= last_valid - g1), 0.0, drive)
        hh, pp, hb, pb = scan(a, drive, hb, pb, reverse=True)
        r1 = pl.multiple_of(g1 * N_SEG, N_SEG)
        hb_ref[pl.ds(r1, rows), :] = hh
        pb_ref[pl.ds(r1, rows), :] = pp
        return hf, pf, hb, pb

    zero = jnp.zeros((N_SEG, bw), F32)
    one = jnp.ones((N_SEG, bw), F32)
    hf, pf, hb, pb = lax.fori_loop(0, nchunk, step, (zero, one, zero, one), unroll=2)

    cin = [zero[0:1]]
    for r in range(1, N_SEG):
        cin.append(hf[r - 1:r] + pf[r - 1:r] * cin[r - 1])
    cinb = [zero[0:1]]
    for r in range(N_SEG - 2, -1, -1):
        cinb.append(hb[r + 1:r + 2] + pb[r + 1:r + 2] * cinb[-1])
    cin_f = jnp.concatenate([jnp.concatenate(cin, axis=0)] * chunk, axis=0)
    cin_b = jnp.concatenate([jnp.concatenate(cinb[::-1], axis=0)] * chunk, axis=0)

    def combine(c, carry):
        g0 = c * chunk
        r0 = pl.multiple_of(g0 * N_SEG, N_SEG)
        hsum = ((hf_ref[pl.ds(r0, rows), :] + pf_ref[pl.ds(r0, rows), :] * cin_f)
                + (hb_ref[pl.ds(r0, rows), :] + pb_ref[pl.ds(r0, rows), :] * cin_b))
        for g in range(chunk):
            lin_ref[pl.ds(LIN_PAD + g0 + g, N_SEG, stride=pitch), :] = hsum[g * N_SEG:(g + 1) * N_SEG]
        return carry

    lax.fori_loop(0, nchunk, combine, 0)

    def gate_out(c, carry):
        t0 = pl.multiple_of(c * lchunk, lchunk)
        gate = jax.nn.gelu(y_ref[pl.ds(t0, lchunk), :], approximate=True)
        o_ref[pl.ds(t0, lchunk), :] = (lin_ref[pl.ds(LIN_PAD + t0, lchunk), :] * gate).astype(BF16)
        return carry

    lax.fori_loop(0, seq // lchunk, gate_out, 0)


def _rglru(xy, conv_w, conv_b, wr, br, wi, bi, lam, nseq, seq):
    t = xy.shape[0]
    bw = C_WIDTH // C_BLOCKS
    pitch, chunk = _segment_plan(seq)
    lchunk = _tile(seq, 512)
    vec = lambda rows: pl.BlockSpec((rows, bw), lambda b, n: (0, n))
    mat = pl.BlockSpec((2, None, bw, bw), lambda b, n: (0, n, 0, 0))
    seg_major = pltpu.VMEM((pitch * N_SEG, bw), F32)
    return pl.pallas_call(
        functools.partial(_rglru_kernel, pitch=pitch, chunk=chunk, lchunk=lchunk),
        grid=(nseq, C_BLOCKS),
        in_specs=[pl.BlockSpec((seq, bw), lambda b, n: (b, n)),
                  pl.BlockSpec((seq, bw), lambda b, n: (b, C_BLOCKS + n)),
                  vec(CONV_W), vec(1), mat, vec(2), mat, vec(2), vec(2)],
        out_specs=pl.BlockSpec((seq, bw), lambda b, n: (b, n)),
        out_shape=jax.ShapeDtypeStruct((t, C_WIDTH), BF16),
        scratch_shapes=[pltpu.VMEM((LIN_PAD + pitch * N_SEG + 8, bw), F32),
                        seg_major, seg_major, seg_major, seg_major],
        compiler_params=_params("parallel", "parallel"),
        name="rglru",
    )(xy, xy, conv_w, conv_b.reshape(1, C_WIDTH), wr, br, wi, bi, lam)


def _merge_kernel(oa_ref, ob_ref, oc_ref, g0_ref, g1_ref, g2_ref, wb_ref, o_ref):
    acc = g0_ref[...] * jnp.dot(oa_ref[...], wb_ref[0], preferred_element_type=F32)
    acc = acc + g1_ref[...] * jnp.dot(ob_ref[...], wb_ref[1], preferred_element_type=F32)
    acc = acc + g2_ref[...] * jnp.dot(oc_ref[...], wb_ref[2], preferred_element_type=F32)
    o_ref[...] = acc.astype(BF16)


def _merge(oa, ob, oc, gates, wb):
    t, kw = oa.shape
    d = wb.shape[2]
    tm = _tile(t, 512)
    tn = _tile(d, 1024)
    ncb = d // tn
    o_spec = pl.BlockSpec((tm, kw), lambda j, i: (i, 0))
    g_spec = lambda br: pl.BlockSpec((tm, tn), lambda j, i: (i, br * ncb + j))
    return pl.pallas_call(
        _merge_kernel,
        grid=(ncb, t // tm),
        in_specs=[o_spec, o_spec, o_spec, g_spec(0), g_spec(1), g_spec(2),
                  pl.BlockSpec((N_BRANCH, kw, tn), lambda j, i: (0, 0, j))],
        out_specs=pl.BlockSpec((tm, tn), lambda j, i: (i, j)),
        out_shape=jax.ShapeDtypeStruct((t, d), BF16),
        compiler_params=_params("parallel", "parallel"),
        name="merge",
    )(oa, ob, oc, gates, gates, gates, wb)


def _out_kernel(m_ref, w_ref, gpost_ref, gnext_ref, *rest, nsplit, na):
    x_refs, (xo_ref, h_ref) = rest[:-2], rest[-2:]
    rows_per = m_ref.shape[0] // nsplit
    for r in range(nsplit):
        rows = slice(r * rows_per, (r + 1) * rows_per)
        y = jnp.dot(m_ref[rows, :], w_ref[...], preferred_element_type=F32)
        if len(x_refs) == 2:
            x = jnp.where(pl.program_id(0) < na, x_refs[0][rows, :], x_refs[1][rows, :])
        else:
            x = x_refs[0][rows, :]
        xn = x + _rms(y, gpost_ref[...])
        xo_ref[rows, :] = xn
        h_ref[rows, :] = _rms(xn, gnext_ref[...]).astype(BF16)


def _out_proj(merged, w, x, gpost, gnext):
    t, d = merged.shape
    row = lambda tm: pl.BlockSpec((tm, d), lambda i: (i, 0))
    if isinstance(x, tuple):
        tm = _tile(math.gcd(x[0].shape[0], x[1].shape[0]), 512)
        na, spec_a, spec_b = _two_part_rows(x[0], x[1], tm)
        x_args, x_specs = list(x), [spec_a, spec_b]
    else:
        tm = _tile(t, 512)
        na, x_args, x_specs = 0, [x], [row(tm)]
    gain = pl.BlockSpec((1, d), lambda i: (0, 0))
    return pl.pallas_call(
        functools.partial(_out_kernel, nsplit=4 if tm % 512 == 0 else 1, na=na),
        grid=(t // tm,),
        in_specs=[row(tm), pl.BlockSpec((d, d), lambda i: (0, 0)), gain, gain] + x_specs,
        out_specs=[row(tm), row(tm)],
        out_shape=[jax.ShapeDtypeStruct((t, d), F32), jax.ShapeDtypeStruct((t, d), BF16)],
        compiler_params=_params("parallel"),
        name="out_proj",
    )(merged, w, gpost.reshape(1, d), gnext.reshape(1, d), *x_args)


def _ffn_out_kernel(hid_ref, w_ref, x_ref, gpost_ref, gnext_ref, xo_ref, *rest, nsplit):
    h_ref = rest[0] if len(rest) == 2 else None
    xs_ref = rest[-1]
    k = pl.program_id(1)
    nk, _, slab = xs_ref.shape
    d = xo_ref.shape[1]

    xs_ref[k] = x_ref[...]

    @pl.when((k == 0) & (nk > 1))
    def _():
        xo_ref[...] = jnp.dot(hid_ref[...], w_ref[...], preferred_element_type=F32)

    @pl.when((k > 0) & (k < nk - 1))
    def _():
        xo_ref[...] += jnp.dot(hid_ref[...], w_ref[...], preferred_element_type=F32)

    @pl.when(k == nk - 1)
    def _():
        slabs = [slice(j * slab, (j + 1) * slab) for j in range(nk)]
        rows_per = xo_ref.shape[0] // nsplit
        for r in range(nsplit):
            rows = slice(r * rows_per, (r + 1) * rows_per)
            part = jnp.dot(hid_ref[rows, :], w_ref[...], preferred_element_type=F32)
            y = part if nk == 1 else xo_ref[rows, :] + part
            rs = lax.rsqrt(jnp.sum(jnp.square(y), axis=-1, keepdims=True) * (1.0 / d) + EPS)
            ssq = 0.0
            for j, sl in enumerate(slabs):
                xn = xs_ref[j, rows, :] + y[:, sl] * rs * gpost_ref[:, sl]
                xo_ref[rows, sl] = xn
                ssq = ssq + jnp.sum(jnp.square(xn), axis=-1, keepdims=True)
            if h_ref is not None:
                rs = lax.rsqrt(ssq * (1.0 / d) + EPS)
                for sl in slabs:
                    h_ref[rows, sl] = (xo_ref[rows, sl] * rs * gnext_ref[:, sl]).astype(BF16)


def _ffn_out(hid, w, x, gpost, gnext, row_start=0, nrows=None):
    d = x.shape[1]
    t = x.shape[0] if nrows is None else nrows
    ff = hid.shape[1]
    tm = _tile(math.gcd(row_start, t), 1024)
    tk = _tile(ff, 1024)
    nk = ff // tk
    slab = d // nk
    assert slab % HEAD_DIM == 0
    i0 = row_start // tm
    row_in = pl.BlockSpec((tm, slab), lambda i, k: (i0 + i, k))
    row = pl.BlockSpec((tm, d), lambda i, k: (i, 0))
    gain = pl.BlockSpec((1, d), lambda i, k: (0, 0))
    with_next = gnext is not None
    out_specs = [row, row] if with_next else [row]
    out_shape = [jax.ShapeDtypeStruct((t, d), F32)] + ([jax.ShapeDtypeStruct((t, d), BF16)] if with_next else [])
    gn = gnext if with_next else gpost
    outs = pl.pallas_call(
        functools.partial(_ffn_out_kernel, nsplit=4 if tm % 512 == 0 else 1),
        grid=(t // tm, ff // tk),
        in_specs=[pl.BlockSpec((tm, tk), lambda i, k: (i0 + i, k)),
                  pl.BlockSpec((tk, d), lambda i, k: (k, 0)),
                  row_in, gain, gain],
        out_specs=out_specs,
        out_shape=out_shape,
        scratch_shapes=[pltpu.VMEM((nk, tm, slab), F32)],
        compiler_params=_params("parallel", "arbitrary"),
        name="ffn_out",
    )(hid, w, x, gpost.reshape(1, d), gn.reshape(1, d))
    return (outs[0], outs[1]) if with_next else (outs[0], None)


def _rope_tables(ang):
    cos = jnp.cos(ang)
    sin = jnp.sin(ang)
    return jnp.concatenate([cos, cos], axis=-1), jnp.concatenate([-sin, sin], axis=-1)


def _angles_1d(s):
    t = jnp.arange(s, dtype=F32)
    inv = ROPE_THETA ** (-jnp.arange(0, HEAD_DIM, 2, dtype=F32) / HEAD_DIM)
    return t[:, None] * inv[None, :]


def _angles_axial(s):
    rows = s // GRID_W
    row = jnp.repeat(jnp.arange(rows, dtype=F32), GRID_W)
    col = jnp.tile(jnp.arange(GRID_W, dtype=F32), rows)
    half = HEAD_DIM // 2
    inv = ROPE_THETA ** (-jnp.arange(0, half, 2, dtype=F32) / half)
    return jnp.concatenate([row[:, None] * inv[None, :], col[:, None] * inv[None, :]], axis=-1)


def kernel(x_prompt, x_sample, norm_mix_pre, norm_mix_post, norm_ffn_pre, norm_ffn_post, w_in, q_norm_a, k_norm_a, sink_b, conv_w, conv_b, gate_r_w, gate_r_b, gate_i_w, gate_i_b, lru_lambda, w_branch, w_out, w_ffn_in, w_ffn_out):
    nb_p, seq, d = x_prompt.shape
    nb_s, seq_s, _ = x_sample.shape
    assert seq == seq_s
    nseq = nb_p + nb_s
    depth = w_in.shape[0]
    x = (x_prompt.reshape(nb_p * seq, d), x_sample.reshape(nb_s * seq, d))

    cos_ax, sin_ax = _rope_tables(_angles_axial(seq))
    cos_1d, sin_1d = _rope_tables(_angles_1d(seq))

    qa0 = 0
    ka0 = qa0 + A_HEADS * HEAD_DIM
    va0 = ka0 + A_KV * HEAD_DIM
    qb0 = va0 + A_KV * HEAD_DIM
    kb0 = qb0 + B_HEADS * HEAD_DIM
    vb0 = kb0 + B_KV * HEAD_DIM
    xc0 = vb0 + B_KV * HEAD_DIM
    g0 = xc0 + 2 * C_WIDTH
    cols = lambda w, a, b: w[:, a:b]

    h = _prenorm(x[0], x[1], norm_mix_pre[0])
    for l in range(depth):
        wl = w_in[l]
        w_qka = jnp.concatenate([cols(wl, qa0, ka0), cols(wl, ka0, va0)], axis=1).astype(BF16)
        w_qkb = jnp.concatenate([cols(wl, qb0, kb0), cols(wl, kb0, vb0)], axis=1).astype(BF16)
        w_v = jnp.concatenate([cols(wl, va0, qb0), cols(wl, vb0, xc0)], axis=1).astype(BF16)
        w_xy = cols(wl, xc0, g0).astype(BF16)
        w_g = wl[:, g0:].astype(BF16)
        gains_a = jnp.concatenate([jnp.tile(q_norm_a[l] * ATTN_A_QSCALE, A_HEADS),
                                   jnp.tile(k_norm_a[l], A_KV)]).reshape(1, -1)

        qk_a = _proj_rope(h, w_qka, gains_a, cos_ax, sin_ax, seq, normed=True)
        qk_b = _proj_rope(h, w_qkb, gains_a, cos_1d, sin_1d, seq, normed=False)
        v_ab, vt_a = _proj_v(h, w_v, _attn_key_tile(seq))
        xy = _proj(h, w_xy, None, F32, "proj_xy")
        gates = _proj(h, w_g, "sigmoid", F32, "proj_gates")

        oa = _attn_global(qk_a, vt_a, nseq, seq)
        ob = _attn_window(qk_b, v_ab, sink_b[l], nseq, seq)
        oc = _rglru(xy, conv_w[l], conv_b[l], gate_r_w[l].astype(BF16), gate_r_b[l],
                    gate_i_w[l].astype(BF16), gate_i_b[l], lru_lambda[l], nseq, seq)

        merged = _merge(oa, ob, oc, gates, w_branch[l].astype(BF16))
        x, h2 = _out_proj(merged, w_out[l].astype(BF16), x, norm_mix_post[l], norm_ffn_pre[l])
        hid = _proj(h2, w_ffn_in[l].astype(BF16), "relu2", BF16, "ffn_in")
        w2 = w_ffn_out[l].astype(BF16)
        if l + 1 < depth:
            x, h = _ffn_out(hid, w2, x, norm_ffn_post[l], norm_mix_pre[l + 1])

    rows_p = nb_p * seq
    y_p, _ = _ffn_out(hid, w2, x, norm_ffn_post[depth - 1], None, 0, rows_p)
    y_s, _ = _ffn_out(hid, w2, x, norm_ffn_post[depth - 1], None, rows_p, nb_s * seq)
    return (y_p.reshape(nb_p, seq, d), y_s.reshape(nb_s, seq, d))
```

```python
import functools
import math

import jax
import jax.numpy as jnp
from jax import lax
from jax.experimental import pallas as pl
from jax.experimental.pallas import tpu as pltpu

F32 = jnp.float32
BF16 = jnp.bfloat16

HEAD_DIM = 128
A_HEADS = 8
A_KV = 2
B_HEADS = 8
B_KV = 2
C_WIDTH = 1024
C_BLOCKS = 8
CONV_W = 4
LRU_C = 8.0
MIX_W = 1024
N_BRANCH = 3
WINDOW = 128
GRID_W = 64
ROPE_THETA = 10000.0
EPS = 1e-6
NEG_BIG = -1e30
ATTN_A_QSCALE = HEAD_DIM ** -0.5 * math.log2(math.e)
ATTN_Q_TILES_PER_STEP = 2

VMEM_LIMIT_BYTES = 56 * 1024 * 1024


def _params(*sem):
    return pltpu.CompilerParams(dimension_semantics=sem, vmem_limit_bytes=VMEM_LIMIT_BYTES)


def _tile(n, pref):
    t = min(n, pref)
    assert n % t == 0, (n, t)
    return t


def _rms(x, g):
    return x * lax.rsqrt(jnp.mean(x * x, axis=-1, keepdims=True) + EPS) * g


def _rope(x, cos, sin_signed):
    return x * cos + pltpu.roll(x, HEAD_DIM // 2, axis=1) * sin_signed


def _two_part_rows(xa, xb, tm):
    d = xa.shape[1]
    na = xa.shape[0] // tm
    spec_a = pl.BlockSpec((tm, d), lambda i: (jnp.minimum(i, na - 1), 0))
    spec_b = pl.BlockSpec((tm, d), lambda i: (jnp.maximum(i - na, 0), 0))
    return na, spec_a, spec_b


def _prenorm_kernel(xa_ref, xb_ref, g_ref, h_ref, *, na):
    x = jnp.where(pl.program_id(0) < na, xa_ref[...], xb_ref[...])
    h_ref[...] = _rms(x, g_ref[...]).astype(BF16)


def _prenorm(xa, xb, g):
    d = xa.shape[1]
    t = xa.shape[0] + xb.shape[0]
    tm = _tile(math.gcd(xa.shape[0], xb.shape[0]), 512)
    na, spec_a, spec_b = _two_part_rows(xa, xb, tm)
    return pl.pallas_call(
        functools.partial(_prenorm_kernel, na=na),
        grid=(t // tm,),
        in_specs=[spec_a, spec_b, pl.BlockSpec((1, d), lambda i: (0, 0))],
        out_specs=pl.BlockSpec((tm, d), lambda i: (i, 0)),
        out_shape=jax.ShapeDtypeStruct((t, d), BF16),
        compiler_params=_params("parallel"),
        name="prenorm",
    )(xa, xb, g.reshape(1, d))


def _proj_rope_kernel(h_ref, w_ref, g_ref, cos_ref, sin_ref, o_ref, *, normed, nsplit):
    rows_per = h_ref.shape[0] // nsplit
    for r in range(nsplit):
        rows = slice(r * rows_per, (r + 1) * rows_per)
        acc = jnp.dot(h_ref[rows, :], w_ref[...], preferred_element_type=F32)
        cos = cos_ref[rows, :]
        sin = sin_ref[rows, :]
        for hd in range(acc.shape[1] // HEAD_DIM):
            sl = slice(hd * HEAD_DIM, (hd + 1) * HEAD_DIM)
            xh = acc[:, sl]
            if normed:
                xh = _rms(xh, g_ref[:, sl])
            o_ref[rows, sl] = _rope(xh, cos, sin).astype(BF16)


def _proj_rope(h, w, gains, cos, sin, seq, normed):
    t, d = h.shape
    n = w.shape[1]
    tm = _tile(seq, 512)
    nsb = seq // tm
    return pl.pallas_call(
        functools.partial(_proj_rope_kernel, normed=normed, nsplit=4 if tm % 512 == 0 else 1),
        grid=(t // tm,),
        in_specs=[pl.BlockSpec((tm, d), lambda i: (i, 0)),
                  pl.BlockSpec((d, n), lambda i: (0, 0)),
                  pl.BlockSpec((1, n), lambda i: (0, 0)),
                  pl.BlockSpec((tm, HEAD_DIM), lambda i: (i % nsb, 0)),
                  pl.BlockSpec((tm, HEAD_DIM), lambda i: (i % nsb, 0))],
        out_specs=pl.BlockSpec((tm, n), lambda i: (i, 0)),
        out_shape=jax.ShapeDtypeStruct((t, n), BF16),
        compiler_params=_params("parallel"),
        name="proj_rope_normed" if normed else "proj_rope",
    )(h, w, gains, cos, sin)


def _proj_kernel(h_ref, w_ref, o_ref, *, act):
    acc = jnp.dot(h_ref[...], w_ref[...], preferred_element_type=F32)
    if act == "sigmoid":
        acc = 0.5 * jnp.tanh(0.5 * acc) + 0.5
    elif act == "relu2":
        acc = jnp.square(jnp.maximum(acc, 0.0))
    o_ref[...] = acc.astype(o_ref.dtype)


def _proj(h, w, act, out_dtype, name):
    t, d = h.shape
    n = w.shape[1]
    tm = _tile(t, 512)
    tn = _tile(n, 2048)
    return pl.pallas_call(
        functools.partial(_proj_kernel, act=act),
        grid=(n // tn, t // tm),
        in_specs=[pl.BlockSpec((tm, d), lambda j, i: (i, 0)),
                  pl.BlockSpec((d, tn), lambda j, i: (0, j))],
        out_specs=pl.BlockSpec((tm, tn), lambda j, i: (i, j)),
        out_shape=jax.ShapeDtypeStruct((t, n), out_dtype),
        compiler_params=_params("parallel", "parallel"),
        name=name,
    )(h, w)


VT_ONES = 16
VT_ROWS = HEAD_DIM + VT_ONES


def _proj_v_kernel(h_ref, w_ref, v_ref, vt_ref):
    acc = jnp.dot(h_ref[...], w_ref[...], preferred_element_type=F32)
    v_ref[...] = acc.astype(BF16)
    ones = jnp.ones((VT_ONES, acc.shape[0]), BF16)
    for hh in range(A_KV):
        vt_ref[hh * VT_ROWS:hh * VT_ROWS + HEAD_DIM, :] = acc[:, hh * HEAD_DIM:(hh + 1) * HEAD_DIM].T.astype(BF16)
        vt_ref[hh * VT_ROWS + HEAD_DIM:(hh + 1) * VT_ROWS, :] = ones


def _proj_v(h, w, tk):
    t, d = h.shape
    n = w.shape[1]
    na = A_KV * VT_ROWS
    return pl.pallas_call(
        _proj_v_kernel,
        grid=(t // tk,),
        in_specs=[pl.BlockSpec((tk, d), lambda i: (i, 0)),
                  pl.BlockSpec((d, n), lambda i: (0, 0))],
        out_specs=[pl.BlockSpec((tk, n), lambda i: (i, 0)),
                   pl.BlockSpec((None, na, tk), lambda i: (i, 0, 0))],
        out_shape=[jax.ShapeDtypeStruct((t, n), BF16),
                   jax.ShapeDtypeStruct((t // tk, na, tk), BF16)],
        compiler_params=_params("parallel"),
        name="proj_v",
    )(h, w)


def _attn_global_kernel(q_ref, k_ref, vt_ref, o_ref, m_ref, acc_ref, s0_ref, s1_ref, *, tq, tk, n_rep):
    seq = k_ref.shape[0]
    nk = seq // tk
    ntile = q_ref.shape[0] // tq

    def stacked_q(t):
        return jnp.concatenate([q_ref[t * tq:(t + 1) * tq, g * HEAD_DIM:(g + 1) * HEAD_DIM]
                                for g in range(n_rep)], axis=0)

    def scores(c, q, s_ref):
        off = pl.multiple_of(c * tk, tk)
        s_ref[...] = lax.dot_general(k_ref[pl.ds(off, tk), :], q, (((1,), (1,)), ((), ())),
                                     preferred_element_type=F32)

    def update(c, s_ref):
        s = s_ref[...]
        m_prev = m_ref[...]
        m_new = jnp.maximum(m_prev, jnp.max(s, axis=0, keepdims=True))
        alpha = jnp.exp2(m_prev - m_new)
        p = jnp.exp2(s - jnp.concatenate([m_new] * (tk // 8), axis=0))
        acc_ref[...] = (jnp.concatenate([alpha] * (VT_ROWS // 8), axis=0) * acc_ref[...]
                        + jnp.dot(vt_ref[c], p.astype(BF16), preferred_element_type=F32))
        m_ref[...] = m_new

    q = stacked_q(0)
    scores(0, q, s0_ref)
    for t in range(ntile):
        q_next = stacked_q(t + 1) if t + 1 < ntile else q
        wrap_to = 0 if t + 1 < ntile else nk - 1
        m_ref[...] = jnp.full(m_ref.shape, NEG_BIG, F32)
        acc_ref[...] = jnp.zeros(acc_ref.shape, F32)

        def pair(i, carry, q=q, q_next=q_next, wrap_to=wrap_to):
            c = 2 * i
            scores(c + 1, q, s1_ref)
            update(c, s0_ref)
            wrap = c + 2 >= nk
            scores(jnp.where(wrap, wrap_to, c + 2), jnp.where(wrap, q_next, q), s0_ref)
            update(c + 1, s1_ref)
            return carry

        lax.fori_loop(0, nk // 2, pair, 0, unroll=min(4, nk // 2))
        acc = acc_ref[...]
        denom = jnp.concatenate([acc[HEAD_DIM:HEAD_DIM + 8]] * (HEAD_DIM // 8), axis=0)
        o = (acc[:HEAD_DIM] / denom).T
        for g in range(n_rep):
            o_ref[t * tq:(t + 1) * tq, g * HEAD_DIM:(g + 1) * HEAD_DIM] = o[g * tq:(g + 1) * tq].astype(BF16)
        q = q_next


def _attn_key_tile(seq):
    tk = _tile(seq // 2, 512)
    assert (seq // tk) % 2 == 0
    return tk


def _attn_global(qk, vt, nseq, seq):
    t = qk.shape[0]
    n_rep = A_HEADS // A_KV
    tq = _tile(seq, 256)
    tk = _attn_key_tile(seq)
    tb = _tile(seq, ATTN_Q_TILES_PER_STEP * tq)
    nqb = seq // tb
    gw = n_rep * HEAD_DIM
    return pl.pallas_call(
        functools.partial(_attn_global_kernel, tq=tq, tk=tk, n_rep=n_rep),
        grid=(nseq, A_KV, nqb),
        in_specs=[pl.BlockSpec((tb, gw), lambda b, h, i: (b * nqb + i, h)),
                  pl.BlockSpec((seq, HEAD_DIM), lambda b, h, i: (b, A_HEADS + h)),
                  pl.BlockSpec((seq // tk, VT_ROWS, tk), lambda b, h, i: (b, h, 0))],
        out_specs=pl.BlockSpec((tb, gw), lambda b, h, i: (b * nqb + i, h)),
        out_shape=jax.ShapeDtypeStruct((t, A_HEADS * HEAD_DIM), BF16),
        scratch_shapes=[pltpu.VMEM((8, n_rep * tq), F32),
                        pltpu.VMEM((VT_ROWS, n_rep * tq), F32),
                        pltpu.VMEM((tk, n_rep * tq), F32),
                        pltpu.VMEM((tk, n_rep * tq), F32)],
        compiler_params=_params("parallel", "parallel", "parallel"),
        name="attn_global",
    )(qk, qk, vt)


def _attn_window_kernel(sink_ref, q_ref, k_ref, v_ref, o_ref, *, tq, nsub, n_rep):
    seq = k_ref.shape[0]
    wk = min(seq, tq + 2 * WINDOW)
    c2 = HEAD_DIM ** -0.5 * math.log2(math.e)
    inv_scale = HEAD_DIM ** 0.5
    h = pl.program_id(1)
    i = pl.program_id(2)
    sink = jnp.concatenate([jnp.full((tq, HEAD_DIM), sink_ref[h * n_rep + g] * inv_scale, F32)
                            for g in range(n_rep)], axis=0)
    row = lax.broadcasted_iota(jnp.int32, (tq, wk), 0)
    col = lax.broadcasted_iota(jnp.int32, (tq, wk), 1)
    for j in range(nsub):
        q0 = (i * nsub + j) * tq
        start = pl.multiple_of(jnp.clip(q0 - WINDOW, 0, seq - wk), HEAD_DIM)
        k = k_ref[pl.ds(start, wk), :]
        v = v_ref[pl.ds(start, wk), :]
        v_ones = jnp.concatenate([v, jnp.ones_like(v)], axis=1)
        valid = jnp.abs(row - col + (q0 - start)) <= WINDOW
        q = jnp.concatenate([q_ref[j * tq:(j + 1) * tq, g * HEAD_DIM:(g + 1) * HEAD_DIM]
                             for g in range(n_rep)], axis=0)
        s = lax.dot_general(q, k, (((1,), (1,)), ((), ())), preferred_element_type=F32)
        s = jnp.where(jnp.concatenate([valid] * n_rep, axis=0), s, NEG_BIG)
        m = jnp.maximum(jnp.max(s, axis=-1, keepdims=True), sink)
        p = jnp.exp2((s - jnp.concatenate([m] * (wk // HEAD_DIM), axis=1)) * c2)
        acc = jnp.dot(p.astype(BF16), v_ones, preferred_element_type=F32)
        o = acc[:, :HEAD_DIM] / (acc[:, HEAD_DIM:] + jnp.exp2((sink - m) * c2))
        for g in range(n_rep):
            o_ref[j * tq:(j + 1) * tq, g * HEAD_DIM:(g + 1) * HEAD_DIM] = o[g * tq:(g + 1) * tq].astype(BF16)


def _attn_window(qk, v, sink, nseq, seq):
    t = qk.shape[0]
    n_rep = B_HEADS // B_KV
    tq = _tile(seq, 256)
    nsub = 2 if seq % (2 * tq) == 0 else 1
    tb = nsub * tq
    nqb = seq // tb
    gw = n_rep * HEAD_DIM
    return pl.pallas_call(
        functools.partial(_attn_window_kernel, tq=tq, nsub=nsub, n_rep=n_rep),
        grid=(nseq, B_KV, nqb),
        in_specs=[pl.BlockSpec(memory_space=pltpu.SMEM),
                  pl.BlockSpec((tb, gw), lambda b, h, i: (b * nqb + i, h)),
                  pl.BlockSpec((seq, HEAD_DIM), lambda b, h, i: (b, B_HEADS + h)),
                  pl.BlockSpec((seq, HEAD_DIM), lambda b, h, i: (b, A_KV + h))],
        out_specs=pl.BlockSpec((tb, gw), lambda b, h, i: (b * nqb + i, h)),
        out_shape=jax.ShapeDtypeStruct((t, B_HEADS * HEAD_DIM), BF16),
        compiler_params=_params("parallel", "parallel", "parallel"),
        name="attn_window",
    )(sink, qk, qk, v)


N_SEG = 8
LIN_PAD = 8


def _segment_plan(seq):
    m = -(-seq // (4 * N_SEG))
    m += 1 - m % 2
    while True:
        for dv in range(1, m + 1):
            if m % dv == 0 and (16 <= 4 * dv <= 64 or (dv == m and 4 * dv <= 96)):
                assert (N_SEG - 1) * 4 * m < seq <= N_SEG * 4 * m
                return 4 * m, 4 * dv
        m += 2


def _rglru_kernel(x_ref, y_ref, cw_ref, cb_ref, wr_ref, br_ref, wi_ref, bi_ref, lam_ref, o_ref,
                  lin_ref, hf_ref, pf_ref, hb_ref, pb_ref, *, pitch, chunk, lchunk):
    seq, bw = x_ref.shape
    nchunk = pitch // chunk
    rows = chunk * N_SEG
    left = CONV_W // 2
    last_valid = seq - (N_SEG - 1) * pitch
    lin_rows = lin_ref.shape[0]

    lin_ref[0:LIN_PAD, :] = jnp.zeros((LIN_PAD, bw), F32)
    lin_ref[LIN_PAD + seq:lin_rows, :] = jnp.zeros((lin_rows - LIN_PAD - seq, bw), F32)

    def copy_in(c, carry):
        t0 = pl.multiple_of(c * lchunk, lchunk)
        lin_ref[pl.ds(LIN_PAD + t0, lchunk), :] = x_ref[pl.ds(t0, lchunk), :]
        return carry

    lax.fori_loop(0, seq // lchunk, copy_in, 0)

    lam = lam_ref[...]
    log_sig = jnp.minimum(lam, 0.0) - jnp.log1p(jnp.exp(-jnp.abs(lam)))

    def conv(g0):
        u = cb_ref[...]
        for j in range(CONV_W):
            xt = jnp.concatenate(
                [lin_ref[pl.ds(LIN_PAD + g0 + g + j - left, N_SEG, stride=pitch), :] for g in range(chunk)],
                axis=0)
            u = u + xt * cw_ref[j:j + 1, :]
        return u

    def sigmoid(z):
        return 0.5 * jnp.tanh(0.5 * z) + 0.5

    def recurrence_terms(u, d):
        ub = u.astype(BF16)
        r = sigmoid(jnp.dot(ub, wr_ref[d], preferred_element_type=F32) + br_ref[d:d + 1, :])
        gi = sigmoid(jnp.dot(ub, wi_ref[d], preferred_element_type=F32) + bi_ref[d:d + 1, :])
        log_a = LRU_C * r * log_sig[d:d + 1, :]
        a = jnp.exp(log_a)
        th = jnp.tanh(log_a)
        z = -2.0 * th / (1.0 - th)
        drive = z * lax.rsqrt(jnp.maximum(z, 1e-30)) * (gi * u)
        return a, drive

    def scan(a, b, h, p, reverse):
        hs = [None] * chunk
        ps = [None] * chunk
        for g in (range(chunk - 1, -1, -1) if reverse else range(chunk)):
            ag = a[g * N_SEG:(g + 1) * N_SEG]
            h = ag * h + b[g * N_SEG:(g + 1) * N_SEG]
            p = ag * p
            hs[g] = h
            ps[g] = p
        return jnp.concatenate(hs, axis=0), jnp.concatenate(ps, axis=0), h, p

    row = lax.broadcasted_iota(jnp.int32, (rows, bw), 0)
    in_last_segment = (row & (N_SEG - 1)) == N_SEG - 1
    group = row >> 3

    def step(c, carry):
        hf, pf, hb, pb = carry
        g0 = c * chunk
        a, drive = recurrence_terms(conv(g0), 0)
        hh, pp, hf, pf = scan(a, drive, hf, pf, reverse=False)
        r0 = pl.multiple_of(g0 * N_SEG, N_SEG)
        hf_ref[pl.ds(r0, rows), :] = hh
        pf_ref[pl.ds(r0, rows), :] = pp

        g1 = (nchunk - 1 - c) * chunk
        a, drive = recurrence_terms(conv(g1), 1)
        drive = jnp.where(in_last_segment & (group >= last_valid - g1), 0.0, drive)
        hh, pp, hb, pb = scan(a, drive, hb, pb, reverse=True)
        r1 = pl.multiple_of(g1 * N_SEG, N_SEG)
        hb_ref[pl.ds(r1, rows), :] = hh
        pb_ref[pl.ds(r1, rows), :] = pp
        return hf, pf, hb, pb

    zero = jnp.zeros((N_SEG, bw), F32)
    one = jnp.ones((N_SEG, bw), F32)
    hf, pf, hb, pb = lax.fori_loop(0, nchunk, step, (zero, one, zero, one), unroll=2)

    cin = [zero[0:1]]
    for r in range(1, N_SEG):
        cin.append(hf[r - 1:r] + pf[r - 1:r] * cin[r - 1])
    cinb = [zero[0:1]]
    for r in range(N_SEG - 2, -1, -1):
        cinb.append(hb[r + 1:r + 2] + pb[r + 1:r + 2] * cinb[-1])
    cin_f = jnp.concatenate([jnp.concatenate(cin, axis=0)] * chunk, axis=0)
    cin_b = jnp.concatenate([jnp.concatenate(cinb[::-1], axis=0)] * chunk, axis=0)

    def combine(c, carry):
        g0 = c * chunk
        r0 = pl.multiple_of(g0 * N_SEG, N_SEG)
        hsum = ((hf_ref[pl.ds(r0, rows), :] + pf_ref[pl.ds(r0, rows), :] * cin_f)
                + (hb_ref[pl.ds(r0, rows), :] + pb_ref[pl.ds(r0, rows), :] * cin_b))
        for g in range(chunk):
            lin_ref[pl.ds(LIN_PAD + g0 + g, N_SEG, stride=pitch), :] = hsum[g * N_SEG:(g + 1) * N_SEG]
        return carry

    lax.fori_loop(0, nchunk, combine, 0)

    def gate_out(c, carry):
        t0 = pl.multiple_of(c * lchunk, lchunk)
        gate = jax.nn.gelu(y_ref[pl.ds(t0, lchunk), :], approximate=True)
        o_ref[pl.ds(t0, lchunk), :] = (lin_ref[pl.ds(LIN_PAD + t0, lchunk), :] * gate).astype(BF16)
        return carry

    lax.fori_loop(0, seq // lchunk, gate_out, 0)


def _rglru(xy, conv_w, conv_b, wr, br, wi, bi, lam, nseq, seq):
    t = xy.shape[0]
    bw = C_WIDTH // C_BLOCKS
    pitch, chunk = _segment_plan(seq)
    lchunk = _tile(seq, 512)
    vec = lambda rows: pl.BlockSpec((rows, bw), lambda b, n: (0, n))
    mat = pl.BlockSpec((2, None, bw, bw), lambda b, n: (0, n, 0, 0))
    seg_major = pltpu.VMEM((pitch * N_SEG, bw), F32)
    return pl.pallas_call(
        functools.partial(_rglru_kernel, pitch=pitch, chunk=chunk, lchunk=lchunk),
        grid=(nseq, C_BLOCKS),
        in_specs=[pl.BlockSpec((seq, bw), lambda b, n: (b, n)),
                  pl.BlockSpec((seq, bw), lambda b, n: (b, C_BLOCKS + n)),
                  vec(CONV_W), vec(1), mat, vec(2), mat, vec(2), vec(2)],
        out_specs=pl.BlockSpec((seq, bw), lambda b, n: (b, n)),
        out_shape=jax.ShapeDtypeStruct((t, C_WIDTH), BF16),
        scratch_shapes=[pltpu.VMEM((LIN_PAD + pitch * N_SEG + 8, bw), F32),
                        seg_major, seg_major, seg_major, seg_major],
        compiler_params=_params("parallel", "parallel"),
        name="rglru",
    )(xy, xy, conv_w, conv_b.reshape(1, C_WIDTH), wr, br, wi, bi, lam)


def _merge_kernel(oa_ref, ob_ref, oc_ref, g0_ref, g1_ref, g2_ref, wb_ref, o_ref):
    acc = g0_ref[...] * jnp.dot(oa_ref[...], wb_ref[0], preferred_element_type=F32)
    acc = acc + g1_ref[...] * jnp.dot(ob_ref[...], wb_ref[1], preferred_element_type=F32)
    acc = acc + g2_ref[...] * jnp.dot(oc_ref[...], wb_ref[2], preferred_element_type=F32)
    o_ref[...] = acc.astype(BF16)


def _merge(oa, ob, oc, gates, wb):
    t, kw = oa.shape
    d = wb.shape[2]
    tm = _tile(t, 512)
    tn = _tile(d, 1024)
    ncb = d // tn
    o_spec = pl.BlockSpec((tm, kw), lambda j, i: (i, 0))
    g_spec = lambda br: pl.BlockSpec((tm, tn), lambda j, i: (i, br * ncb + j))
    return pl.pallas_call(
        _merge_kernel,
        grid=(ncb, t // tm),
        in_specs=[o_spec, o_spec, o_spec, g_spec(0), g_spec(1), g_spec(2),
                  pl.BlockSpec((N_BRANCH, kw, tn), lambda j, i: (0, 0, j))],
        out_specs=pl.BlockSpec((tm, tn), lambda j, i: (i, j)),
        out_shape=jax.ShapeDtypeStruct((t, d), BF16),
        compiler_params=_params("parallel", "parallel"),
        name="merge",
    )(oa, ob, oc, gates, gates, gates, wb)


def _out_kernel(m_ref, w_ref, gpost_ref, gnext_ref, *rest, nsplit, na):
    x_refs, (xo_ref, h_ref) = rest[:-2], rest[-2:]
    rows_per = m_ref.shape[0] // nsplit
    for r in range(nsplit):
        rows = slice(r * rows_per, (r + 1) * rows_per)
        y = jnp.dot(m_ref[rows, :], w_ref[...], preferred_element_type=F32)
        if len(x_refs) == 2:
            x = jnp.where(pl.program_id(0) < na, x_refs[0][rows, :], x_refs[1][rows, :])
        else:
            x = x_refs[0][rows, :]
        xn = x + _rms(y, gpost_ref[...])
        xo_ref[rows, :] = xn
        h_ref[rows, :] = _rms(xn, gnext_ref[...]).astype(BF16)


def _out_proj(merged, w, x, gpost, gnext):
    t, d = merged.shape
    row = lambda tm: pl.BlockSpec((tm, d), lambda i: (i, 0))
    if isinstance(x, tuple):
        tm = _tile(math.gcd(x[0].shape[0], x[1].shape[0]), 512)
        na, spec_a, spec_b = _two_part_rows(x[0], x[1], tm)
        x_args, x_specs = list(x), [spec_a, spec_b]
    else:
        tm = _tile(t, 512)
        na, x_args, x_specs = 0, [x], [row(tm)]
    gain = pl.BlockSpec((1, d), lambda i: (0, 0))
    return pl.pallas_call(
        functools.partial(_out_kernel, nsplit=4 if tm % 512 == 0 else 1, na=na),
        grid=(t // tm,),
        in_specs=[row(tm), pl.BlockSpec((d, d), lambda i: (0, 0)), gain, gain] + x_specs,
        out_specs=[row(tm), row(tm)],
        out_shape=[jax.ShapeDtypeStruct((t, d), F32), jax.ShapeDtypeStruct((t, d), BF16)],
        compiler_params=_params("parallel"),
        name="out_proj",
    )(merged, w, gpost.reshape(1, d), gnext.reshape(1, d), *x_args)


def _ffn_out_kernel(hid_ref, w_ref, x_ref, gpost_ref, gnext_ref, xo_ref, *rest, nsplit):
    h_ref = rest[0] if len(rest) == 2 else None
    xs_ref = rest[-1]
    k = pl.program_id(1)
    nk, _, slab = xs_ref.shape
    d = xo_ref.shape[1]

    xs_ref[k] = x_ref[...]

    @pl.when((k == 0) & (nk > 1))
    def _():
        xo_ref[...] = jnp.dot(hid_ref[...], w_ref[...], preferred_element_type=F32)

    @pl.when((k > 0) & (k < nk - 1))
    def _():
        xo_ref[...] += jnp.dot(hid_ref[...], w_ref[...], preferred_element_type=F32)

    @pl.when(k == nk - 1)
    def _():
        slabs = [slice(j * slab, (j + 1) * slab) for j in range(nk)]
        rows_per = xo_ref.shape[0] // nsplit
        for r in range(nsplit):
            rows = slice(r * rows_per, (r + 1) * rows_per)
            part = jnp.dot(hid_ref[rows, :], w_ref[...], preferred_element_type=F32)
            y = part if nk == 1 else xo_ref[rows, :] + part
            rs = lax.rsqrt(jnp.sum(jnp.square(y), axis=-1, keepdims=True) * (1.0 / d) + EPS)
            ssq = 0.0
            for j, sl in enumerate(slabs):
                xn = xs_ref[j, rows, :] + y[:, sl] * rs * gpost_ref[:, sl]
                xo_ref[rows, sl] = xn
                ssq = ssq + jnp.sum(jnp.square(xn), axis=-1, keepdims=True)
            if h_ref is not None:
                rs = lax.rsqrt(ssq * (1.0 / d) + EPS)
                for sl in slabs:
                    h_ref[rows, sl] = (xo_ref[rows, sl] * rs * gnext_ref[:, sl]).astype(BF16)


def _ffn_out(hid, w, x, gpost, gnext, row_start=0, nrows=None):
    d = x.shape[1]
    t = x.shape[0] if nrows is None else nrows
    ff = hid.shape[1]
    tm = _tile(math.gcd(row_start, t), 1024)
    tk = _tile(ff, 1024)
    nk = ff // tk
    slab = d // nk
    assert slab % HEAD_DIM == 0
    i0 = row_start // tm
    row_in = pl.BlockSpec((tm, slab), lambda i, k: (i0 + i, k))
    row = pl.BlockSpec((tm, d), lambda i, k: (i, 0))
    gain = pl.BlockSpec((1, d), lambda i, k: (0, 0))
    with_next = gnext is not None
    out_specs = [row, row] if with_next else [row]
    out_shape = [jax.ShapeDtypeStruct((t, d), F32)] + ([jax.ShapeDtypeStruct((t, d), BF16)] if with_next else [])
    gn = gnext if with_next else gpost
    outs = pl.pallas_call(
        functools.partial(_ffn_out_kernel, nsplit=4 if tm % 512 == 0 else 1),
        grid=(t // tm, ff // tk),
        in_specs=[pl.BlockSpec((tm, tk), lambda i, k: (i0 + i, k)),
                  pl.BlockSpec((tk, d), lambda i, k: (k, 0)),
                  row_in, gain, gain],
        out_specs=out_specs,
        out_shape=out_shape,
        scratch_shapes=[pltpu.VMEM((nk, tm, slab), F32)],
        compiler_params=_params("parallel", "arbitrary"),
        name="ffn_out",
    )(hid, w, x, gpost.reshape(1, d), gn.reshape(1, d))
    return (outs[0], outs[1]) if with_next else (outs[0], None)


def _rope_tables(ang):
    cos = jnp.cos(ang)
    sin = jnp.sin(ang)
    return jnp.concatenate([cos, cos], axis=-1), jnp.concatenate([-sin, sin], axis=-1)


def _angles_1d(s):
    t = jnp.arange(s, dtype=F32)
    inv = ROPE_THETA ** (-jnp.arange(0, HEAD_DIM, 2, dtype=F32) / HEAD_DIM)
    return t[:, None] * inv[None, :]


def _angles_axial(s):
    rows = s // GRID_W
    row = jnp.repeat(jnp.arange(rows, dtype=F32), GRID_W)
    col = jnp.tile(jnp.arange(GRID_W, dtype=F32), rows)
    half = HEAD_DIM // 2
    inv = ROPE_THETA ** (-jnp.arange(0, half, 2, dtype=F32) / half)
    return jnp.concatenate([row[:, None] * inv[None, :], col[:, None] * inv[None, :]], axis=-1)


def kernel(x_prompt, x_sample, norm_mix_pre, norm_mix_post, norm_ffn_pre, norm_ffn_post, w_in, q_norm_a, k_norm_a, sink_b, conv_w, conv_b, gate_r_w, gate_r_b, gate_i_w, gate_i_b, lru_lambda, w_branch, w_out, w_ffn_in, w_ffn_out):
    nb_p, seq, d = x_prompt.shape
    nb_s, seq_s, _ = x_sample.shape
    assert seq == seq_s
    nseq = nb_p + nb_s
    depth = w_in.shape[0]
    x = (x_prompt.reshape(nb_p * seq, d), x_sample.reshape(nb_s * seq, d))

    cos_ax, sin_ax = _rope_tables(_angles_axial(seq))
    cos_1d, sin_1d = _rope_tables(_angles_1d(seq))

    qa0 = 0
    ka0 = qa0 + A_HEADS * HEAD_DIM
    va0 = ka0 + A_KV * HEAD_DIM
    qb0 = va0 + A_KV * HEAD_DIM
    kb0 = qb0 + B_HEADS * HEAD_DIM
    vb0 = kb0 + B_KV * HEAD_DIM
    xc0 = vb0 + B_KV * HEAD_DIM
    g0 = xc0 + 2 * C_WIDTH
    cols = lambda w, a, b: w[:, a:b]

    h = _prenorm(x[0], x[1], norm_mix_pre[0])
    for l in range(depth):
        wl = w_in[l]
        w_qka = jnp.concatenate([cols(wl, qa0, ka0), cols(wl, ka0, va0)], axis=1).astype(BF16)
        w_qkb = jnp.concatenate([cols(wl, qb0, kb0), cols(wl, kb0, vb0)], axis=1).astype(BF16)
        w_v = jnp.concatenate([cols(wl, va0, qb0), cols(wl, vb0, xc0)], axis=1).astype(BF16)
        w_xy = cols(wl, xc0, g0).astype(BF16)
        w_g = wl[:, g0:].astype(BF16)
        gains_a = jnp.concatenate([jnp.tile(q_norm_a[l] * ATTN_A_QSCALE, A_HEADS),
                                   jnp.tile(k_norm_a[l], A_KV)]).reshape(1, -1)

        qk_a = _proj_rope(h, w_qka, gains_a, cos_ax, sin_ax, seq, normed=True)
        qk_b = _proj_rope(h, w_qkb, gains_a, cos_1d, sin_1d, seq, normed=False)
        v_ab, vt_a = _proj_v(h, w_v, _attn_key_tile(seq))
        xy = _proj(h, w_xy, None, F32, "proj_xy")
        gates = _proj(h, w_g, "sigmoid", F32, "proj_gates")

        oa = _attn_global(qk_a, vt_a, nseq, seq)
        ob = _attn_window(qk_b, v_ab, sink_b[l], nseq, seq)
        oc = _rglru(xy, conv_w[l], conv_b[l], gate_r_w[l].astype(BF16), gate_r_b[l],
                    gate_i_w[l].astype(BF16), gate_i_b[l], lru_lambda[l], nseq, seq)

        merged = _merge(oa, ob, oc, gates, w_branch[l].astype(BF16))
        x, h2 = _out_proj(merged, w_out[l].astype(BF16), x, norm_mix_post[l], norm_ffn_pre[l])
        hid = _proj(h2, w_ffn_in[l].astype(BF16), "relu2", BF16, "ffn_in")
        w2 = w_ffn_out[l].astype(BF16)
        if l + 1 < depth:
            x, h = _ffn_out(hid, w2, x, norm_ffn_post[l], norm_mix_pre[l + 1])

    rows_p = nb_p * seq
    y_p, _ = _ffn_out(hid, w2, x, norm_ffn_post[depth - 1], None, 0, rows_p)
    y_s, _ = _ffn_out(hid, w2, x, norm_ffn_post[depth - 1], None, rows_p, nb_s * seq)
    return (y_p.reshape(nb_p, seq, d), y_s.reshape(nb_s, seq, d))
```

```python
import functools
import math

import jax
import jax.numpy as jnp
from jax import lax
from jax.experimental import pallas as pl
from jax.experimental.pallas import tpu as pltpu

F32 = jnp.float32
BF16 = jnp.bfloat16

HEAD_DIM = 128
A_HEADS = 8
A_KV = 2
B_HEADS = 8
B_KV = 2
C_WIDTH = 1024
C_BLOCKS = 8
CONV_W = 4
LRU_C = 8.0
MIX_W = 1024
N_BRANCH = 3
WINDOW = 128
GRID_W = 64
ROPE_THETA = 10000.0
EPS = 1e-6
NEG_BIG = -1e30
ATTN_QSCALE = HEAD_DIM ** -0.5 * math.log2(math.e)
ATTN_Q_TILES_PER_STEP = 4

VMEM_LIMIT_BYTES = 56 * 1024 * 1024


def _params(*sem):
    return pltpu.CompilerParams(dimension_semantics=sem, vmem_limit_bytes=VMEM_LIMIT_BYTES)


def _tile(n, pref):
    t = min(n, pref)
    assert n % t == 0, (n, t)
    return t


def _rms(x, g):
    return x * lax.rsqrt(jnp.mean(x * x, axis=-1, keepdims=True) + EPS) * g


def _rope(x, cos, sin_signed):
    return x * cos + pltpu.roll(x, HEAD_DIM // 2, axis=1) * sin_signed


def _two_part_rows(xa, xb, tm):
    d = xa.shape[1]
    na = xa.shape[0] // tm
    spec_a = pl.BlockSpec((tm, d), lambda i: (jnp.minimum(i, na - 1), 0))
    spec_b = pl.BlockSpec((tm, d), lambda i: (jnp.maximum(i - na, 0), 0))
    return na, spec_a, spec_b


def _prenorm_kernel(xa_ref, xb_ref, g_ref, h_ref, *, na):
    x = jnp.where(pl.program_id(0) < na, xa_ref[...], xb_ref[...])
    h_ref[...] = _rms(x, g_ref[...]).astype(BF16)


def _prenorm(xa, xb, g):
    d = xa.shape[1]
    t = xa.shape[0] + xb.shape[0]
    tm = _tile(math.gcd(xa.shape[0], xb.shape[0]), 512)
    na, spec_a, spec_b = _two_part_rows(xa, xb, tm)
    return pl.pallas_call(
        functools.partial(_prenorm_kernel, na=na),
        grid=(t // tm,),
        in_specs=[spec_a, spec_b, pl.BlockSpec((1, d), lambda i: (0, 0))],
        out_specs=pl.BlockSpec((tm, d), lambda i: (i, 0)),
        out_shape=jax.ShapeDtypeStruct((t, d), BF16),
        compiler_params=_params("parallel"),
        name="prenorm",
    )(xa, xb, g.reshape(1, d))


def _proj_rope_kernel(h_ref, w_ref, g_ref, cos_ref, sin_ref, o_ref, *, normed, nsplit):
    rows_per = h_ref.shape[0] // nsplit
    for r in range(nsplit):
        rows = slice(r * rows_per, (r + 1) * rows_per)
        acc = jnp.dot(h_ref[rows, :], w_ref[...], preferred_element_type=F32)
        cos = cos_ref[rows, :]
        sin = sin_ref[rows, :]
        for hd in range(acc.shape[1] // HEAD_DIM):
            sl = slice(hd * HEAD_DIM, (hd + 1) * HEAD_DIM)
            xh = acc[:, sl]
            xh = _rms(xh, g_ref[:, sl]) if normed else xh * g_ref[:, sl]
            o_ref[rows, sl] = _rope(xh, cos, sin).astype(BF16)


def _proj_rope(h, w, gains, cos, sin, seq, normed):
    t, d = h.shape
    n = w.shape[1]
    tm = _tile(seq, 512)
    nsb = seq // tm
    return pl.pallas_call(
        functools.partial(_proj_rope_kernel, normed=normed, nsplit=4 if tm % 512 == 0 else 1),
        grid=(t // tm,),
        in_specs=[pl.BlockSpec((tm, d), lambda i: (i, 0)),
                  pl.BlockSpec((d, n), lambda i: (0, 0)),
                  pl.BlockSpec((1, n), lambda i: (0, 0)),
                  pl.BlockSpec((tm, HEAD_DIM), lambda i: (i % nsb, 0)),
                  pl.BlockSpec((tm, HEAD_DIM), lambda i: (i % nsb, 0))],
        out_specs=pl.BlockSpec((tm, n), lambda i: (i, 0)),
        out_shape=jax.ShapeDtypeStruct((t, n), BF16),
        compiler_params=_params("parallel"),
        name="proj_rope_normed" if normed else "proj_rope",
    )(h, w, gains, cos, sin)


def _proj_kernel(h_ref, w_ref, o_ref, *, act):
    acc = jnp.dot(h_ref[...], w_ref[...], preferred_element_type=F32)
    if act == "sigmoid":
        acc = 0.5 * jnp.tanh(0.5 * acc) + 0.5
    elif act == "relu2":
        acc = jnp.square(jnp.maximum(acc, 0.0))
    o_ref[...] = acc.astype(o_ref.dtype)


def _proj(h, w, act, out_dtype, name):
    t, d = h.shape
    n = w.shape[1]
    tm = _tile(t, 512)
    tn = _tile(n, 2048)
    return pl.pallas_call(
        functools.partial(_proj_kernel, act=act),
        grid=(n // tn, t // tm),
        in_specs=[pl.BlockSpec((tm, d), lambda j, i: (i, 0)),
                  pl.BlockSpec((d, tn), lambda j, i: (0, j))],
        out_specs=pl.BlockSpec((tm, tn), lambda j, i: (i, j)),
        out_shape=jax.ShapeDtypeStruct((t, n), out_dtype),
        compiler_params=_params("parallel", "parallel"),
        name=name,
    )(h, w)


VT_ONES = 16
VT_ROWS = HEAD_DIM + VT_ONES


WIN_KEY_TILE = HEAD_DIM


def _proj_v_kernel(h_ref, w_ref, vta_ref, vtb_ref):
    acc = jnp.dot(h_ref[...], w_ref[...], preferred_element_type=F32)
    rows = acc.shape[0]
    for hh in range(A_KV):
        vta_ref[hh * VT_ROWS:hh * VT_ROWS + HEAD_DIM, :] = acc[:, hh * HEAD_DIM:(hh + 1) * HEAD_DIM].T.astype(BF16)
        vta_ref[hh * VT_ROWS + HEAD_DIM:(hh + 1) * VT_ROWS, :] = jnp.ones((VT_ONES, rows), BF16)
    for hh in range(B_KV):
        vt = acc[:, (A_KV + hh) * HEAD_DIM:(A_KV + hh + 1) * HEAD_DIM].T.astype(BF16)
        for j in range(rows // WIN_KEY_TILE):
            vtb_ref[j, hh * VT_ROWS:hh * VT_ROWS + HEAD_DIM, :] = vt[:, j * WIN_KEY_TILE:(j + 1) * WIN_KEY_TILE]
            vtb_ref[j, hh * VT_ROWS + HEAD_DIM:(hh + 1) * VT_ROWS, :] = jnp.ones((VT_ONES, WIN_KEY_TILE), BF16)


def _proj_v(h, w, tk):
    t, d = h.shape
    n = w.shape[1]
    per_step = tk // WIN_KEY_TILE
    return pl.pallas_call(
        _proj_v_kernel,
        grid=(t // tk,),
        in_specs=[pl.BlockSpec((tk, d), lambda i: (i, 0)),
                  pl.BlockSpec((d, n), lambda i: (0, 0))],
        out_specs=[pl.BlockSpec((None, A_KV * VT_ROWS, tk), lambda i: (i, 0, 0)),
                   pl.BlockSpec((per_step, B_KV * VT_ROWS, WIN_KEY_TILE), lambda i: (i, 0, 0))],
        out_shape=[jax.ShapeDtypeStruct((t // tk, A_KV * VT_ROWS, tk), BF16),
                   jax.ShapeDtypeStruct((t // WIN_KEY_TILE, B_KV * VT_ROWS, WIN_KEY_TILE), BF16)],
        compiler_params=_params("parallel"),
        name="proj_v",
    )(h, w)


def _attn_global_kernel(q_ref, k_ref, vt_ref, o_ref, m_ref, acc_ref, s0_ref, s1_ref, *, tq, tk, n_rep):
    seq = k_ref.shape[0]
    nk = seq // tk
    ntile = q_ref.shape[0] // tq

    def stacked_q(t):
        return jnp.concatenate([q_ref[t * tq:(t + 1) * tq, g * HEAD_DIM:(g + 1) * HEAD_DIM]
                                for g in range(n_rep)], axis=0)

    def scores(c, q, s_ref):
        off = pl.multiple_of(c * tk, tk)
        s_ref[...] = lax.dot_general(k_ref[pl.ds(off, tk), :], q, (((1,), (1,)), ((), ())),
                                     preferred_element_type=F32)

    def update(c, s_ref):
        s = s_ref[...]
        m_prev = m_ref[...]
        m_new = jnp.maximum(m_prev, jnp.max(s, axis=0, keepdims=True))
        alpha = jnp.exp2(m_prev - m_new)
        p = jnp.exp2(s - jnp.concatenate([m_new] * (tk // 8), axis=0))
        acc_ref[...] = (jnp.concatenate([alpha] * (VT_ROWS // 8), axis=0) * acc_ref[...]
                        + jnp.dot(vt_ref[c], p.astype(BF16), preferred_element_type=F32))
        m_ref[...] = m_new

    q = stacked_q(0)
    scores(0, q, s0_ref)
    for t in range(ntile):
        q_next = stacked_q(t + 1) if t + 1 < ntile else q
        wrap_to = 0 if t + 1 < ntile else nk - 1
        m_ref[...] = jnp.full(m_ref.shape, NEG_BIG, F32)
        acc_ref[...] = jnp.zeros(acc_ref.shape, F32)

        def pair(i, carry, q=q, q_next=q_next, wrap_to=wrap_to):
            c = 2 * i
            scores(c + 1, q, s1_ref)
            update(c, s0_ref)
            wrap = c + 2 >= nk
            scores(jnp.where(wrap, wrap_to, c + 2), jnp.where(wrap, q_next, q), s0_ref)
            update(c + 1, s1_ref)
            return carry

        lax.fori_loop(0, nk // 2, pair, 0, unroll=min(4, nk // 2))
        acc = acc_ref[...]
        denom = jnp.concatenate([acc[HEAD_DIM:HEAD_DIM + 8]] * (HEAD_DIM // 8), axis=0)
        o = (acc[:HEAD_DIM] / denom).T
        for g in range(n_rep):
            o_ref[t * tq:(t + 1) * tq, g * HEAD_DIM:(g + 1) * HEAD_DIM] = o[g * tq:(g + 1) * tq].astype(BF16)
        q = q_next


def _attn_key_tile(seq):
    tk = _tile(seq // 2, 512)
    assert (seq // tk) % 2 == 0
    return tk


def _attn_global(qk, vt, nseq, seq):
    t = qk.shape[0]
    n_rep = A_HEADS // A_KV
    tq = _tile(seq, 256)
    tk = _attn_key_tile(seq)
    tb = _tile(seq, ATTN_Q_TILES_PER_STEP * tq)
    nqb = seq // tb
    gw = n_rep * HEAD_DIM
    return pl.pallas_call(
        functools.partial(_attn_global_kernel, tq=tq, tk=tk, n_rep=n_rep),
        grid=(nseq, A_KV, nqb),
        in_specs=[pl.BlockSpec((tb, gw), lambda b, h, i: (b * nqb + i, h)),
                  pl.BlockSpec((seq, HEAD_DIM), lambda b, h, i: (b, A_HEADS + h)),
                  pl.BlockSpec((seq // tk, VT_ROWS, tk), lambda b, h, i: (b, h, 0))],
        out_specs=pl.BlockSpec((tb, gw), lambda b, h, i: (b * nqb + i, h)),
        out_shape=jax.ShapeDtypeStruct((t, A_HEADS * HEAD_DIM), BF16),
        scratch_shapes=[pltpu.VMEM((8, n_rep * tq), F32),
                        pltpu.VMEM((VT_ROWS, n_rep * tq), F32),
                        pltpu.VMEM((tk, n_rep * tq), F32),
                        pltpu.VMEM((tk, n_rep * tq), F32)],
        compiler_params=_params("parallel", "parallel", "parallel"),
        name="attn_global",
    )(qk, qk, vt)


def _attn_window_kernel(sink_ref, q_ref, k_ref, vt_ref, o_ref, *, tq, nsub, n_rep):
    seq = k_ref.shape[0]
    wk = min(seq, tq + 2 * WINDOW)
    h = pl.program_id(1)
    i = pl.program_id(2)
    sink = jnp.concatenate([jnp.full((8, tq), sink_ref[h * n_rep + g] * math.log2(math.e), F32)
                            for g in range(n_rep)], axis=1)
    key = lax.broadcasted_iota(jnp.int32, (wk, tq), 0)
    qry = lax.broadcasted_iota(jnp.int32, (wk, tq), 1)
    for j in range(nsub):
        q0 = (i * nsub + j) * tq
        start = pl.multiple_of(jnp.clip(q0 - WINDOW, 0, seq - wk), WIN_KEY_TILE)
        k = k_ref[pl.ds(start, wk), :]
        kt0 = start // WIN_KEY_TILE
        vt = jnp.concatenate([vt_ref[kt0 + u] for u in range(wk // WIN_KEY_TILE)], axis=1)
        valid = jnp.abs(qry - key + (q0 - start)) <= WINDOW
        q = jnp.concatenate([q_ref[j * tq:(j + 1) * tq, g * HEAD_DIM:(g + 1) * HEAD_DIM]
                             for g in range(n_rep)], axis=0)
        s = lax.dot_general(k, q, (((1,), (1,)), ((), ())), preferred_element_type=F32)
        s = jnp.where(jnp.concatenate([valid] * n_rep, axis=1), s, NEG_BIG)
        m = jnp.maximum(jnp.max(s, axis=0, keepdims=True), sink)
        p = jnp.exp2(s - jnp.concatenate([m] * (wk // 8), axis=0))
        acc = jnp.dot(vt, p.astype(BF16), preferred_element_type=F32)
        denom = acc[HEAD_DIM:HEAD_DIM + 8] + jnp.exp2(sink - m)
        o = (acc[:HEAD_DIM] / jnp.concatenate([denom] * (HEAD_DIM // 8), axis=0)).T
        for g in range(n_rep):
            o_ref[j * tq:(j + 1) * tq, g * HEAD_DIM:(g + 1) * HEAD_DIM] = o[g * tq:(g + 1) * tq].astype(BF16)


def _attn_window(qk, vt, sink, nseq, seq):
    t = qk.shape[0]
    n_rep = B_HEADS // B_KV
    tq = _tile(seq, 256)
    nsub = 2 if seq % (2 * tq) == 0 else 1
    tb = nsub * tq
    nqb = seq // tb
    gw = n_rep * HEAD_DIM
    return pl.pallas_call(
        functools.partial(_attn_window_kernel, tq=tq, nsub=nsub, n_rep=n_rep),
        grid=(nseq, B_KV, nqb),
        in_specs=[pl.BlockSpec(memory_space=pltpu.SMEM),
                  pl.BlockSpec((tb, gw), lambda b, h, i: (b * nqb + i, h)),
                  pl.BlockSpec((seq, HEAD_DIM), lambda b, h, i: (b, B_HEADS + h)),
                  pl.BlockSpec((seq // WIN_KEY_TILE, VT_ROWS, WIN_KEY_TILE), lambda b, h, i: (b, h, 0))],
        out_specs=pl.BlockSpec((tb, gw), lambda b, h, i: (b * nqb + i, h)),
        out_shape=jax.ShapeDtypeStruct((t, B_HEADS * HEAD_DIM), BF16),
        compiler_params=_params("parallel", "parallel", "parallel"),
        name="attn_window",
    )(sink, qk, qk, vt)


N_SEG = 8
LIN_PAD = 8


def _segment_plan(seq):
    m = -(-seq // (4 * N_SEG))
    m += 1 - m % 2
    while True:
        for dv in range(1, m + 1):
            if m % dv == 0 and (16 <= 4 * dv <= 64 or (dv == m and 4 * dv <= 96)):
                assert (N_SEG - 1) * 4 * m < seq <= N_SEG * 4 * m
                return 4 * m, 4 * dv
        m += 2


def _rglru_kernel(x_ref, y_ref, cw_ref, cb_ref, wr_ref, br_ref, wi_ref, bi_ref, lam_ref, o_ref,
                  lin_ref, hf_ref, pf_ref, hb_ref, pb_ref, *, pitch, chunk, lchunk):
    seq, bw = x_ref.shape
    nchunk = pitch // chunk
    rows = chunk * N_SEG
    left = CONV_W // 2
    last_valid = seq - (N_SEG - 1) * pitch
    lin_rows = lin_ref.shape[0]

    lin_ref[0:LIN_PAD, :] = jnp.zeros((LIN_PAD, bw), F32)
    lin_ref[LIN_PAD + seq:lin_rows, :] = jnp.zeros((lin_rows - LIN_PAD - seq, bw), F32)

    def copy_in(c, carry):
        t0 = pl.multiple_of(c * lchunk, lchunk)
        lin_ref[pl.ds(LIN_PAD + t0, lchunk), :] = x_ref[pl.ds(t0, lchunk), :]
        return carry

    lax.fori_loop(0, seq // lchunk, copy_in, 0)

    lam = lam_ref[...]
    log_sig = jnp.minimum(lam, 0.0) - jnp.log1p(jnp.exp(-jnp.abs(lam)))

    def conv(g0):
        u = cb_ref[...]
        for j in range(CONV_W):
            xt = jnp.concatenate(
                [lin_ref[pl.ds(LIN_PAD + g0 + g + j - left, N_SEG, stride=pitch), :] for g in range(chunk)],
                axis=0)
            u = u + xt * cw_ref[j:j + 1, :]
        return u

    half_c_log_sig = (0.5 * LRU_C) * log_sig

    def recurrence_terms(u, d):
        ub = u.astype(BF16)
        tr = jnp.tanh(jnp.dot(ub, wr_ref[d], preferred_element_type=F32) + br_ref[d:d + 1, :])
        ti = jnp.tanh(jnp.dot(ub, wi_ref[d], preferred_element_type=F32) + bi_ref[d:d + 1, :])
        k = half_c_log_sig[d:d + 1, :]
        log_a = tr * k + k
        a = jnp.exp(log_a)
        th = jnp.tanh(log_a)
        z = -2.0 * th / (1.0 - th)
        uh = 0.5 * u
        drive = z * lax.rsqrt(jnp.maximum(z, 1e-30)) * (ti * uh + uh)
        return a, drive

    def scan(a, b, h, p, reverse):
        hs = [None] * chunk
        ps = [None] * chunk
        for g in (range(chunk - 1, -1, -1) if reverse else range(chunk)):
            ag = a[g * N_SEG:(g + 1) * N_SEG]
            h = ag * h + b[g * N_SEG:(g + 1) * N_SEG]
            p = ag * p
            hs[g] = h
            ps[g] = p
        return jnp.concatenate(hs, axis=0), jnp.concatenate(ps, axis=0), h, p

    row = lax.broadcasted_iota(jnp.int32, (rows, bw), 0)
    in_last_segment = (row & (N_SEG - 1)) == N_SEG - 1
    group = row >> 3

    def step(c, carry):
        hf, pf, hb, pb = carry
        g0 = c * chunk
        a, drive = recurrence_terms(conv(g0), 0)
        hh, pp, hf, pf = scan(a, drive, hf, pf, reverse=False)
        r0 = pl.multiple_of(g0 * N_SEG, N_SEG)
        hf_ref[pl.ds(r0, rows), :] = hh
        pf_ref[pl.ds(r0, rows), :] = pp

        g1 = (nchunk - 1 - c) * chunk
        a, drive = recurrence_terms(conv(g1), 1)
        drive = jnp.where(in_last_segment & (group >= last_valid - g1), 0.0, drive)
        hh, pp, hb, pb = scan(a, drive, hb, pb, reverse=True)
        r1 = pl.multiple_of(g1 * N_SEG, N_SEG)
        hb_ref[pl.ds(r1, rows), :] = hh
        pb_ref[pl.ds(r1, rows), :] = pp
        return hf, pf, hb, pb

    zero = jnp.zeros((N_SEG, bw), F32)
    one = jnp.ones((N_SEG, bw), F32)
    hf, pf, hb, pb = lax.fori_loop(0, nchunk, step, (zero, one, zero, one), unroll=2)

    cin = [zero[0:1]]
    for r in range(1, N_SEG):
        cin.append(hf[r - 1:r] + pf[r - 1:r] * cin[r - 1])
    cinb = [zero[0:1]]
    for r in range(N_SEG - 2, -1, -1):
        cinb.append(hb[r + 1:r + 2] + pb[r + 1:r + 2] * cinb[-1])
    cin_f = jnp.concatenate([jnp.concatenate(cin, axis=0)] * chunk, axis=0)
    cin_b = jnp.concatenate([jnp.concatenate(cinb[::-1], axis=0)] * chunk, axis=0)

    def combine(c, carry):
        g0 = c * chunk
        r0 = pl.multiple_of(g0 * N_SEG, N_SEG)
        hsum = ((hf_ref[pl.ds(r0, rows), :] + pf_ref[pl.ds(r0, rows), :] * cin_f)
                + (hb_ref[pl.ds(r0, rows), :] + pb_ref[pl.ds(r0, rows), :] * cin_b))
        for g in range(chunk):
            lin_ref[pl.ds(LIN_PAD + g0 + g, N_SEG, stride=pitch), :] = hsum[g * N_SEG:(g + 1) * N_SEG]
        return carry

    lax.fori_loop(0, nchunk, combine, 0)

    def gate_out(c, carry):
        t0 = pl.multiple_of(c * lchunk, lchunk)
        gate = jax.nn.gelu(y_ref[pl.ds(t0, lchunk), :], approximate=True)
        o_ref[pl.ds(t0, lchunk), :] = (lin_ref[pl.ds(LIN_PAD + t0, lchunk), :] * gate).astype(BF16)
        return carry

    lax.fori_loop(0, seq // lchunk, gate_out, 0)


def _rglru(xy, conv_w, conv_b, wr, br, wi, bi, lam, nseq, seq):
    t = xy.shape[0]
    bw = C_WIDTH // C_BLOCKS
    pitch, chunk = _segment_plan(seq)
    lchunk = _tile(seq, 512)
    vec = lambda rows: pl.BlockSpec((rows, bw), lambda b, n: (0, n))
    mat = pl.BlockSpec((2, None, bw, bw), lambda b, n: (0, n, 0, 0))
    seg_major = pltpu.VMEM((pitch * N_SEG, bw), F32)
    return pl.pallas_call(
        functools.partial(_rglru_kernel, pitch=pitch, chunk=chunk, lchunk=lchunk),
        grid=(nseq, C_BLOCKS),
        in_specs=[pl.BlockSpec((seq, bw), lambda b, n: (b, n)),
                  pl.BlockSpec((seq, bw), lambda b, n: (b, C_BLOCKS + n)),
                  vec(CONV_W), vec(1), mat, vec(2), mat, vec(2), vec(2)],
        out_specs=pl.BlockSpec((seq, bw), lambda b, n: (b, n)),
        out_shape=jax.ShapeDtypeStruct((t, C_WIDTH), BF16),
        scratch_shapes=[pltpu.VMEM((LIN_PAD + pitch * N_SEG + 8, bw), F32),
                        seg_major, seg_major, seg_major, seg_major],
        compiler_params=_params("parallel", "parallel"),
        name="rglru",
    )(xy, xy, conv_w, conv_b.reshape(1, C_WIDTH), wr, br, wi, bi, lam)


def _merge_kernel(oa_ref, ob_ref, oc_ref, g0_ref, g1_ref, g2_ref, wb_ref, o_ref):
    acc = g0_ref[...] * jnp.dot(oa_ref[...], wb_ref[0], preferred_element_type=F32)
    acc = acc + g1_ref[...] * jnp.dot(ob_ref[...], wb_ref[1], preferred_element_type=F32)
    acc = acc + g2_ref[...] * jnp.dot(oc_ref[...], wb_ref[2], preferred_element_type=F32)
    o_ref[...] = acc.astype(BF16)


def _merge(oa, ob, oc, gates, wb):
    t, kw = oa.shape
    d = wb.shape[2]
    tm = _tile(t, 512)
    tn = _tile(d, 1024)
    ncb = d // tn
    o_spec = pl.BlockSpec((tm, kw), lambda j, i: (i, 0))
    g_spec = lambda br: pl.BlockSpec((tm, tn), lambda j, i: (i, br * ncb + j))
    return pl.pallas_call(
        _merge_kernel,
        grid=(ncb, t // tm),
        in_specs=[o_spec, o_spec, o_spec, g_spec(0), g_spec(1), g_spec(2),
                  pl.BlockSpec((N_BRANCH, kw, tn), lambda j, i: (0, 0, j))],
        out_specs=pl.BlockSpec((tm, tn), lambda j, i: (i, j)),
        out_shape=jax.ShapeDtypeStruct((t, d), BF16),
        compiler_params=_params("parallel", "parallel"),
        name="merge",
    )(oa, ob, oc, gates, gates, gates, wb)


def _out_kernel(m_ref, w_ref, gpost_ref, gnext_ref, *rest, nsplit, na):
    x_refs, (xo_ref, h_ref) = rest[:-2], rest[-2:]
    rows_per = m_ref.shape[0] // nsplit
    for r in range(nsplit):
        rows = slice(r * rows_per, (r + 1) * rows_per)
        y = jnp.dot(m_ref[rows, :], w_ref[...], preferred_element_type=F32)
        if len(x_refs) == 2:
            x = jnp.where(pl.program_id(0) < na, x_refs[0][rows, :], x_refs[1][rows, :])
        else:
            x = x_refs[0][rows, :]
        xn = x + _rms(y, gpost_ref[...])
        xo_ref[rows, :] = xn
        h_ref[rows, :] = _rms(xn, gnext_ref[...]).astype(BF16)


def _out_proj(merged, w, x, gpost, gnext):
    t, d = merged.shape
    row = lambda tm: pl.BlockSpec((tm, d), lambda i: (i, 0))
    if isinstance(x, tuple):
        tm = _tile(math.gcd(x[0].shape[0], x[1].shape[0]), 512)
        na, spec_a, spec_b = _two_part_rows(x[0], x[1], tm)
        x_args, x_specs = list(x), [spec_a, spec_b]
    else:
        tm = _tile(t, 512)
        na, x_args, x_specs = 0, [x], [row(tm)]
    gain = pl.BlockSpec((1, d), lambda i: (0, 0))
    return pl.pallas_call(
        functools.partial(_out_kernel, nsplit=4 if tm % 512 == 0 else 1, na=na),
        grid=(t // tm,),
        in_specs=[row(tm), pl.BlockSpec((d, d), lambda i: (0, 0)), gain, gain] + x_specs,
        out_specs=[row(tm), row(tm)],
        out_shape=[jax.ShapeDtypeStruct((t, d), F32), jax.ShapeDtypeStruct((t, d), BF16)],
        compiler_params=_params("parallel"),
        name="out_proj",
    )(merged, w, gpost.reshape(1, d), gnext.reshape(1, d), *x_args)


def _ffn_out_kernel(hid_ref, w_ref, x_ref, gpost_ref, gnext_ref, xo_ref, *rest, nsplit):
    h_ref = rest[0] if len(rest) == 2 else None
    xs_ref = rest[-1]
    k = pl.program_id(1)
    nk, _, slab = xs_ref.shape
    d = xo_ref.shape[1]

    xs_ref[k] = x_ref[...]

    @pl.when((k == 0) & (nk > 1))
    def _():
        xo_ref[...] = jnp.dot(hid_ref[...], w_ref[...], preferred_element_type=F32)

    @pl.when((k > 0) & (k < nk - 1))
    def _():
        xo_ref[...] += jnp.dot(hid_ref[...], w_ref[...], preferred_element_type=F32)

    @pl.when(k == nk - 1)
    def _():
        slabs = [slice(j * slab, (j + 1) * slab) for j in range(nk)]
        rows_per = xo_ref.shape[0] // nsplit
        for r in range(nsplit):
            rows = slice(r * rows_per, (r + 1) * rows_per)
            part = jnp.dot(hid_ref[rows, :], w_ref[...], preferred_element_type=F32)
            y = part if nk == 1 else xo_ref[rows, :] + part
            rs = lax.rsqrt(jnp.sum(jnp.square(y), axis=-1, keepdims=True) * (1.0 / d) + EPS)
            ssq = 0.0
            for j, sl in enumerate(slabs):
                xn = xs_ref[j, rows, :] + y[:, sl] * rs * gpost_ref[:, sl]
                xo_ref[rows, sl] = xn
                ssq = ssq + jnp.sum(jnp.square(xn), axis=-1, keepdims=True)
            if h_ref is not None:
                rs = lax.rsqrt(ssq * (1.0 / d) + EPS)
                for sl in slabs:
                    h_ref[rows, sl] = (xo_ref[rows, sl] * rs * gnext_ref[:, sl]).astype(BF16)


def _ffn_out(hid, w, x, gpost, gnext, row_start=0, nrows=None):
    d = x.shape[1]
    t = x.shape[0] if nrows is None else nrows
    ff = hid.shape[1]
    tm = _tile(math.gcd(row_start, t), 1024)
    tk = _tile(ff, 1024)
    nk = ff // tk
    slab = d // nk
    assert slab % HEAD_DIM == 0
    i0 = row_start // tm
    row_in = pl.BlockSpec((tm, slab), lambda i, k: (i0 + i, k))
    row = pl.BlockSpec((tm, d), lambda i, k: (i, 0))
    gain = pl.BlockSpec((1, d), lambda i, k: (0, 0))
    with_next = gnext is not None
    out_specs = [row, row] if with_next else [row]
    out_shape = [jax.ShapeDtypeStruct((t, d), F32)] + ([jax.ShapeDtypeStruct((t, d), BF16)] if with_next else [])
    gn = gnext if with_next else gpost
    outs = pl.pallas_call(
        functools.partial(_ffn_out_kernel, nsplit=4 if tm % 512 == 0 else 1),
        grid=(t // tm, ff // tk),
        in_specs=[pl.BlockSpec((tm, tk), lambda i, k: (i0 + i, k)),
                  pl.BlockSpec((tk, d), lambda i, k: (k, 0)),
                  row_in, gain, gain],
        out_specs=out_specs,
        out_shape=out_shape,
        scratch_shapes=[pltpu.VMEM((nk, tm, slab), F32)],
        compiler_params=_params("parallel", "arbitrary"),
        name="ffn_out",
    )(hid, w, x, gpost.reshape(1, d), gn.reshape(1, d))
    return (outs[0], outs[1]) if with_next else (outs[0], None)


def _rope_tables(ang):
    cos = jnp.cos(ang)
    sin = jnp.sin(ang)
    return jnp.concatenate([cos, cos], axis=-1), jnp.concatenate([-sin, sin], axis=-1)


def _angles_1d(s):
    t = jnp.arange(s, dtype=F32)
    inv = ROPE_THETA ** (-jnp.arange(0, HEAD_DIM, 2, dtype=F32) / HEAD_DIM)
    return t[:, None] * inv[None, :]


def _angles_axial(s):
    rows = s // GRID_W
    row = jnp.repeat(jnp.arange(rows, dtype=F32), GRID_W)
    col = jnp.tile(jnp.arange(GRID_W, dtype=F32), rows)
    half = HEAD_DIM // 2
    inv = ROPE_THETA ** (-jnp.arange(0, half, 2, dtype=F32) / half)
    return jnp.concatenate([row[:, None] * inv[None, :], col[:, None] * inv[None, :]], axis=-1)


def kernel(x_prompt, x_sample, norm_mix_pre, norm_mix_post, norm_ffn_pre, norm_ffn_post, w_in, q_norm_a, k_norm_a, sink_b, conv_w, conv_b, gate_r_w, gate_r_b, gate_i_w, gate_i_b, lru_lambda, w_branch, w_out, w_ffn_in, w_ffn_out):
    nb_p, seq, d = x_prompt.shape
    nb_s, seq_s, _ = x_sample.shape
    assert seq == seq_s
    nseq = nb_p + nb_s
    depth = w_in.shape[0]
    x = (x_prompt.reshape(nb_p * seq, d), x_sample.reshape(nb_s * seq, d))

    cos_ax, sin_ax = _rope_tables(_angles_axial(seq))
    cos_1d, sin_1d = _rope_tables(_angles_1d(seq))

    qa0 = 0
    ka0 = qa0 + A_HEADS * HEAD_DIM
    va0 = ka0 + A_KV * HEAD_DIM
    qb0 = va0 + A_KV * HEAD_DIM
    kb0 = qb0 + B_HEADS * HEAD_DIM
    vb0 = kb0 + B_KV * HEAD_DIM
    xc0 = vb0 + B_KV * HEAD_DIM
    g0 = xc0 + 2 * C_WIDTH
    cols = lambda w, a, b: w[:, a:b]

    h = _prenorm(x[0], x[1], norm_mix_pre[0])
    for l in range(depth):
        wl = w_in[l]
        w_qka = jnp.concatenate([cols(wl, qa0, ka0), cols(wl, ka0, va0)], axis=1).astype(BF16)
        w_qkb = jnp.concatenate([cols(wl, qb0, kb0), cols(wl, kb0, vb0)], axis=1).astype(BF16)
        w_v = jnp.concatenate([cols(wl, va0, qb0), cols(wl, vb0, xc0)], axis=1).astype(BF16)
        w_xy = cols(wl, xc0, g0).astype(BF16)
        w_g = wl[:, g0:].astype(BF16)
        gains_a = jnp.concatenate([jnp.tile(q_norm_a[l] * ATTN_QSCALE, A_HEADS),
                                   jnp.tile(k_norm_a[l], A_KV)]).reshape(1, -1)
        gains_b = jnp.concatenate([jnp.full((B_HEADS * HEAD_DIM,), ATTN_QSCALE, F32),
                                   jnp.ones((B_KV * HEAD_DIM,), F32)]).reshape(1, -1)

        qk_a = _proj_rope(h, w_qka, gains_a, cos_ax, sin_ax, seq, normed=True)
        qk_b = _proj_rope(h, w_qkb, gains_b, cos_1d, sin_1d, seq, normed=False)
        vt_a, vt_b = _proj_v(h, w_v, _attn_key_tile(seq))
        xy = _proj(h, w_xy, None, F32, "proj_xy")
        gates = _proj(h, w_g, "sigmoid", F32, "proj_gates")

        oa = _attn_global(qk_a, vt_a, nseq, seq)
        ob = _attn_window(qk_b, vt_b, sink_b[l], nseq, seq)
        oc = _rglru(xy, conv_w[l], conv_b[l], (0.5 * gate_r_w[l]).astype(BF16), 0.5 * gate_r_b[l],
                    (0.5 * gate_i_w[l]).astype(BF16), 0.5 * gate_i_b[l], lru_lambda[l], nseq, seq)

        merged = _merge(oa, ob, oc, gates, w_branch[l].astype(BF16))
        x, h2 = _out_proj(merged, w_out[l].astype(BF16), x, norm_mix_post[l], norm_ffn_pre[l])
        hid = _proj(h2, w_ffn_in[l].astype(BF16), "relu2", BF16, "ffn_in")
        w2 = w_ffn_out[l].astype(BF16)
        if l + 1 < depth:
            x, h = _ffn_out(hid, w2, x, norm_ffn_post[l], norm_mix_pre[l + 1])

    rows_p = nb_p * seq
    y_p, _ = _ffn_out(hid, w2, x, norm_ffn_post[depth - 1], None, 0, rows_p)
    y_s, _ = _ffn_out(hid, w2, x, norm_ffn_post[depth - 1], None, rows_p, nb_s * seq)
    return (y_p.reshape(nb_p, seq, d), y_s.reshape(nb_s, seq, d))
```

```python
import functools
import math

import jax
import jax.numpy as jnp
from jax import lax
from jax.experimental import pallas as pl
from jax.experimental.pallas import tpu as pltpu

F32 = jnp.float32
BF16 = jnp.bfloat16

HEAD_DIM = 128
A_HEADS = 8
A_KV = 2
B_HEADS = 8
B_KV = 2
C_WIDTH = 1024
C_BLOCKS = 8
CONV_W = 4
LRU_C = 8.0
MIX_W = 1024
N_BRANCH = 3
WINDOW = 128
GRID_W = 64
ROPE_THETA = 10000.0
EPS = 1e-6
NEG_BIG = -1e30
ATTN_QSCALE = HEAD_DIM ** -0.5 * math.log2(math.e)
ATTN_Q_TILES_PER_STEP = 4

VMEM_LIMIT_BYTES = 56 * 1024 * 1024


def _params(*sem):
    return pltpu.CompilerParams(dimension_semantics=sem, vmem_limit_bytes=VMEM_LIMIT_BYTES)


def _tile(n, pref):
    t = min(n, pref)
    assert n % t == 0, (n, t)
    return t


def _rms(x, g):
    return x * lax.rsqrt(jnp.mean(x * x, axis=-1, keepdims=True) + EPS) * g


def _rope(x, cos, sin_signed):
    return x * cos + pltpu.roll(x, HEAD_DIM // 2, axis=1) * sin_signed


def _two_part_rows(xa, xb, tm):
    d = xa.shape[1]
    na = xa.shape[0] // tm
    spec_a = pl.BlockSpec((tm, d), lambda i: (jnp.minimum(i, na - 1), 0))
    spec_b = pl.BlockSpec((tm, d), lambda i: (jnp.maximum(i - na, 0), 0))
    return na, spec_a, spec_b


def _prenorm_kernel(xa_ref, xb_ref, g_ref, h_ref, *, na):
    x = jnp.where(pl.program_id(0) < na, xa_ref[...], xb_ref[...])
    h_ref[...] = _rms(x, g_ref[...]).astype(BF16)


def _prenorm(xa, xb, g):
    d = xa.shape[1]
    t = xa.shape[0] + xb.shape[0]
    tm = _tile(math.gcd(xa.shape[0], xb.shape[0]), 512)
    na, spec_a, spec_b = _two_part_rows(xa, xb, tm)
    return pl.pallas_call(
        functools.partial(_prenorm_kernel, na=na),
        grid=(t // tm,),
        in_specs=[spec_a, spec_b, pl.BlockSpec((1, d), lambda i: (0, 0))],
        out_specs=pl.BlockSpec((tm, d), lambda i: (i, 0)),
        out_shape=jax.ShapeDtypeStruct((t, d), BF16),
        compiler_params=_params("parallel"),
        name="prenorm",
    )(xa, xb, g.reshape(1, d))


def _proj_rope_kernel(h_ref, w_ref, g_ref, cos_ref, sin_ref, o_ref, *, normed, nsplit):
    rows_per = h_ref.shape[0] // nsplit
    for r in range(nsplit):
        rows = slice(r * rows_per, (r + 1) * rows_per)
        acc = jnp.dot(h_ref[rows, :], w_ref[...], preferred_element_type=F32)
        cos = cos_ref[rows, :]
        sin = sin_ref[rows, :]
        for hd in range(acc.shape[1] // HEAD_DIM):
            sl = slice(hd * HEAD_DIM, (hd + 1) * HEAD_DIM)
            xh = acc[:, sl]
            xh = _rms(xh, g_ref[:, sl]) if normed else xh * g_ref[:, sl]
            o_ref[rows, sl] = _rope(xh, cos, sin).astype(BF16)


def _proj_rope(h, w, gains, cos, sin, seq, normed):
    t, d = h.shape
    n = w.shape[1]
    tm = _tile(seq, 512)
    nsb = seq // tm
    return pl.pallas_call(
        functools.partial(_proj_rope_kernel, normed=normed, nsplit=4 if tm % 512 == 0 else 1),
        grid=(t // tm,),
        in_specs=[pl.BlockSpec((tm, d), lambda i: (i, 0)),
                  pl.BlockSpec((d, n), lambda i: (0, 0)),
                  pl.BlockSpec((1, n), lambda i: (0, 0)),
                  pl.BlockSpec((tm, HEAD_DIM), lambda i: (i % nsb, 0)),
                  pl.BlockSpec((tm, HEAD_DIM), lambda i: (i % nsb, 0))],
        out_specs=pl.BlockSpec((tm, n), lambda i: (i, 0)),
        out_shape=jax.ShapeDtypeStruct((t, n), BF16),
        compiler_params=_params("parallel"),
        name="proj_rope_normed" if normed else "proj_rope",
    )(h, w, gains, cos, sin)


def _proj_kernel(h_ref, w_ref, o_ref, *, act):
    acc = jnp.dot(h_ref[...], w_ref[...], preferred_element_type=F32)
    if act == "sigmoid":
        acc = 0.5 * jnp.tanh(0.5 * acc) + 0.5
    elif act == "relu2":
        acc = jnp.square(jnp.maximum(acc, 0.0))
    o_ref[...] = acc.astype(o_ref.dtype)


def _proj(h, w, act, out_dtype, name):
    t, d = h.shape
    n = w.shape[1]
    tm = _tile(t, 1024)
    tn = _tile(n, 2048)
    return pl.pallas_call(
        functools.partial(_proj_kernel, act=act),
        grid=(n // tn, t // tm),
        in_specs=[pl.BlockSpec((tm, d), lambda j, i: (i, 0)),
                  pl.BlockSpec((d, tn), lambda j, i: (0, j))],
        out_specs=pl.BlockSpec((tm, tn), lambda j, i: (i, j)),
        out_shape=jax.ShapeDtypeStruct((t, n), out_dtype),
        compiler_params=_params("parallel", "parallel"),
        name=name,
    )(h, w)


VT_ONES = 16
VT_ROWS = HEAD_DIM + VT_ONES


WIN_KEY_TILE = HEAD_DIM


def _proj_v_kernel(h_ref, w_ref, vta_ref, vtb_ref):
    acc = jnp.dot(h_ref[...], w_ref[...], preferred_element_type=F32)
    rows = acc.shape[0]
    for hh in range(A_KV):
        vta_ref[hh * VT_ROWS:hh * VT_ROWS + HEAD_DIM, :] = acc[:, hh * HEAD_DIM:(hh + 1) * HEAD_DIM].T.astype(BF16)
        vta_ref[hh * VT_ROWS + HEAD_DIM:(hh + 1) * VT_ROWS, :] = jnp.ones((VT_ONES, rows), BF16)
    for hh in range(B_KV):
        vt = acc[:, (A_KV + hh) * HEAD_DIM:(A_KV + hh + 1) * HEAD_DIM].T.astype(BF16)
        for j in range(rows // WIN_KEY_TILE):
            vtb_ref[j, hh * VT_ROWS:hh * VT_ROWS + HEAD_DIM, :] = vt[:, j * WIN_KEY_TILE:(j + 1) * WIN_KEY_TILE]
            vtb_ref[j, hh * VT_ROWS + HEAD_DIM:(hh + 1) * VT_ROWS, :] = jnp.ones((VT_ONES, WIN_KEY_TILE), BF16)


def _proj_v(h, w, tk):
    t, d = h.shape
    n = w.shape[1]
    per_step = tk // WIN_KEY_TILE
    return pl.pallas_call(
        _proj_v_kernel,
        grid=(t // tk,),
        in_specs=[pl.BlockSpec((tk, d), lambda i: (i, 0)),
                  pl.BlockSpec((d, n), lambda i: (0, 0))],
        out_specs=[pl.BlockSpec((None, A_KV * VT_ROWS, tk), lambda i: (i, 0, 0)),
                   pl.BlockSpec((per_step, B_KV * VT_ROWS, WIN_KEY_TILE), lambda i: (i, 0, 0))],
        out_shape=[jax.ShapeDtypeStruct((t // tk, A_KV * VT_ROWS, tk), BF16),
                   jax.ShapeDtypeStruct((t // WIN_KEY_TILE, B_KV * VT_ROWS, WIN_KEY_TILE), BF16)],
        compiler_params=_params("parallel"),
        name="proj_v",
    )(h, w)


def _attn_global_kernel(q_ref, k_ref, vt_ref, o_ref, m_ref, acc_ref, s0_ref, s1_ref, *, tq, tk, n_rep):
    seq = k_ref.shape[0]
    nk = seq // tk
    ntile = q_ref.shape[0] // tq

    def stacked_q(t):
        return jnp.concatenate([q_ref[t * tq:(t + 1) * tq, g * HEAD_DIM:(g + 1) * HEAD_DIM]
                                for g in range(n_rep)], axis=0)

    def scores(c, q, s_ref):
        off = pl.multiple_of(c * tk, tk)
        s_ref[...] = lax.dot_general(k_ref[pl.ds(off, tk), :], q, (((1,), (1,)), ((), ())),
                                     preferred_element_type=F32)

    def update(c, s_ref):
        s = s_ref[...]
        m_prev = m_ref[...]
        m_new = jnp.maximum(m_prev, jnp.max(s, axis=0, keepdims=True))
        alpha = jnp.exp2(m_prev - m_new)
        p = jnp.exp2(s - jnp.concatenate([m_new] * (tk // 8), axis=0))
        acc_ref[...] = (jnp.concatenate([alpha] * (VT_ROWS // 8), axis=0) * acc_ref[...]
                        + jnp.dot(vt_ref[c], p.astype(BF16), preferred_element_type=F32))
        m_ref[...] = m_new

    q = stacked_q(0)
    scores(0, q, s0_ref)
    for t in range(ntile):
        q_next = stacked_q(t + 1) if t + 1 < ntile else q
        wrap_to = 0 if t + 1 < ntile else nk - 1
        m_ref[...] = jnp.full(m_ref.shape, NEG_BIG, F32)
        acc_ref[...] = jnp.zeros(acc_ref.shape, F32)

        def pair(i, carry, q=q, q_next=q_next, wrap_to=wrap_to):
            c = 2 * i
            scores(c + 1, q, s1_ref)
            update(c, s0_ref)
            wrap = c + 2 >= nk
            scores(jnp.where(wrap, wrap_to, c + 2), jnp.where(wrap, q_next, q), s0_ref)
            update(c + 1, s1_ref)
            return carry

        lax.fori_loop(0, nk // 2, pair, 0, unroll=min(4, nk // 2))
        acc = acc_ref[...]
        denom = jnp.concatenate([acc[HEAD_DIM:HEAD_DIM + 8]] * (HEAD_DIM // 8), axis=0)
        o = (acc[:HEAD_DIM] / denom).T
        for g in range(n_rep):
            o_ref[t * tq:(t + 1) * tq, g * HEAD_DIM:(g + 1) * HEAD_DIM] = o[g * tq:(g + 1) * tq].astype(BF16)
        q = q_next


def _attn_key_tile(seq):
    tk = _tile(seq // 2, 512)
    assert (seq // tk) % 2 == 0
    return tk


def _attn_global(qk, vt, nseq, seq):
    t = qk.shape[0]
    n_rep = A_HEADS // A_KV
    tq = _tile(seq, 256)
    tk = _attn_key_tile(seq)
    tb = _tile(seq, ATTN_Q_TILES_PER_STEP * tq)
    nqb = seq // tb
    gw = n_rep * HEAD_DIM
    return pl.pallas_call(
        functools.partial(_attn_global_kernel, tq=tq, tk=tk, n_rep=n_rep),
        grid=(nseq, A_KV, nqb),
        in_specs=[pl.BlockSpec((tb, gw), lambda b, h, i: (b * nqb + i, h)),
                  pl.BlockSpec((seq, HEAD_DIM), lambda b, h, i: (b, A_HEADS + h)),
                  pl.BlockSpec((seq // tk, VT_ROWS, tk), lambda b, h, i: (b, h, 0))],
        out_specs=pl.BlockSpec((tb, gw), lambda b, h, i: (b * nqb + i, h)),
        out_shape=jax.ShapeDtypeStruct((t, A_HEADS * HEAD_DIM), BF16),
        scratch_shapes=[pltpu.VMEM((8, n_rep * tq), F32),
                        pltpu.VMEM((VT_ROWS, n_rep * tq), F32),
                        pltpu.VMEM((tk, n_rep * tq), F32),
                        pltpu.VMEM((tk, n_rep * tq), F32)],
        compiler_params=_params("parallel", "parallel", "parallel"),
        name="attn_global",
    )(qk, qk, vt)


def _attn_window_kernel(sink_ref, q_ref, k_ref, vt_ref, o_ref, *, tq, nsub, n_rep):
    seq = k_ref.shape[0]
    wk = min(seq, tq + 2 * WINDOW)
    h = pl.program_id(1)
    i = pl.program_id(2)
    sink = jnp.concatenate([jnp.full((8, tq), sink_ref[h * n_rep + g] * math.log2(math.e), F32)
                            for g in range(n_rep)], axis=1)
    key = lax.broadcasted_iota(jnp.int32, (wk, tq), 0)
    qry = lax.broadcasted_iota(jnp.int32, (wk, tq), 1)
    for j in range(nsub):
        q0 = (i * nsub + j) * tq
        start = pl.multiple_of(jnp.clip(q0 - WINDOW, 0, seq - wk), WIN_KEY_TILE)
        k = k_ref[pl.ds(start, wk), :]
        kt0 = start // WIN_KEY_TILE
        vt = jnp.concatenate([vt_ref[kt0 + u] for u in range(wk // WIN_KEY_TILE)], axis=1)
        band = jnp.where(jnp.abs(qry - key + (q0 - start)) <= WINDOW, 0.0, NEG_BIG)
        q = jnp.concatenate([q_ref[j * tq:(j + 1) * tq, g * HEAD_DIM:(g + 1) * HEAD_DIM]
                             for g in range(n_rep)], axis=0)
        s = lax.dot_general(k, q, (((1,), (1,)), ((), ())), preferred_element_type=F32)
        s = s + jnp.concatenate([band] * n_rep, axis=1)
        m = jnp.maximum(jnp.max(s, axis=0, keepdims=True), sink)
        p = jnp.exp2(s - jnp.concatenate([m] * (wk // 8), axis=0))
        acc = jnp.dot(vt, p.astype(BF16), preferred_element_type=F32)
        denom = acc[HEAD_DIM:HEAD_DIM + 8] + jnp.exp2(sink - m)
        o = (acc[:HEAD_DIM] / jnp.concatenate([denom] * (HEAD_DIM // 8), axis=0)).T
        for g in range(n_rep):
            o_ref[j * tq:(j + 1) * tq, g * HEAD_DIM:(g + 1) * HEAD_DIM] = o[g * tq:(g + 1) * tq].astype(BF16)


def _attn_window(qk, vt, sink, nseq, seq):
    t = qk.shape[0]
    n_rep = B_HEADS // B_KV
    tq = _tile(seq, 256)
    nsub = 2 if seq % (2 * tq) == 0 else 1
    tb = nsub * tq
    nqb = seq // tb
    gw = n_rep * HEAD_DIM
    return pl.pallas_call(
        functools.partial(_attn_window_kernel, tq=tq, nsub=nsub, n_rep=n_rep),
        grid=(nseq, B_KV, nqb),
        in_specs=[pl.BlockSpec(memory_space=pltpu.SMEM),
                  pl.BlockSpec((tb, gw), lambda b, h, i: (b * nqb + i, h)),
                  pl.BlockSpec((seq, HEAD_DIM), lambda b, h, i: (b, B_HEADS + h)),
                  pl.BlockSpec((seq // WIN_KEY_TILE, VT_ROWS, WIN_KEY_TILE), lambda b, h, i: (b, h, 0))],
        out_specs=pl.BlockSpec((tb, gw), lambda b, h, i: (b * nqb + i, h)),
        out_shape=jax.ShapeDtypeStruct((t, B_HEADS * HEAD_DIM), BF16),
        compiler_params=_params("parallel", "parallel", "parallel"),
        name="attn_window",
    )(sink, qk, qk, vt)


N_SEG = 8
LIN_PAD = 8


def _segment_plan(seq):
    m = -(-seq // (4 * N_SEG))
    m += 1 - m % 2
    while True:
        for dv in range(1, m + 1):
            if m % dv == 0 and (16 <= 4 * dv <= 64 or (dv == m and 4 * dv <= 96)):
                assert (N_SEG - 1) * 4 * m < seq <= N_SEG * 4 * m
                return 4 * m, 4 * dv
        m += 2


def _rglru_kernel(x_ref, y_ref, cw_ref, cb_ref, wr_ref, br_ref, wi_ref, bi_ref, lam_ref, o_ref,
                  lin_ref, hf_ref, pf_ref, hb_ref, pb_ref, *, pitch, chunk, lchunk):
    seq, bw = x_ref.shape
    nchunk = pitch // chunk
    rows = chunk * N_SEG
    left = CONV_W // 2
    last_valid = seq - (N_SEG - 1) * pitch
    lin_rows = lin_ref.shape[0]

    lin_ref[0:LIN_PAD, :] = jnp.zeros((LIN_PAD, bw), F32)
    lin_ref[LIN_PAD + seq:lin_rows, :] = jnp.zeros((lin_rows - LIN_PAD - seq, bw), F32)

    def copy_in(c, carry):
        t0 = pl.multiple_of(c * lchunk, lchunk)
        lin_ref[pl.ds(LIN_PAD + t0, lchunk), :] = x_ref[pl.ds(t0, lchunk), :]
        return carry

    lax.fori_loop(0, seq // lchunk, copy_in, 0)

    lam = lam_ref[...]
    log_sig = jnp.minimum(lam, 0.0) - jnp.log1p(jnp.exp(-jnp.abs(lam)))

    def conv(g0):
        u = cb_ref[...]
        for j in range(CONV_W):
            xt = jnp.concatenate(
                [lin_ref[pl.ds(LIN_PAD + g0 + g + j - left, N_SEG, stride=pitch), :] for g in range(chunk)],
                axis=0)
            u = u + xt * cw_ref[j:j + 1, :]
        return u

    half_c_log_sig = (0.5 * LRU_C) * log_sig

    def recurrence_terms(u, d):
        ub = u.astype(BF16)
        tr = jnp.tanh(jnp.dot(ub, wr_ref[d], preferred_element_type=F32) + br_ref[d:d + 1, :])
        ti = jnp.tanh(jnp.dot(ub, wi_ref[d], preferred_element_type=F32) + bi_ref[d:d + 1, :])
        k = half_c_log_sig[d:d + 1, :]
        log_a = tr * k + k
        a = jnp.exp(log_a)
        th = jnp.tanh(log_a)
        z = -2.0 * th / (1.0 - th)
        uh = 0.5 * u
        drive = z * lax.rsqrt(jnp.maximum(z, 1e-30)) * (ti * uh + uh)
        return a, drive

    def scan(a, b, h, p, reverse):
        hs = [None] * chunk
        ps = [None] * chunk
        for g in (range(chunk - 1, -1, -1) if reverse else range(chunk)):
            ag = a[g * N_SEG:(g + 1) * N_SEG]
            h = ag * h + b[g * N_SEG:(g + 1) * N_SEG]
            p = ag * p
            hs[g] = h
            ps[g] = p
        return jnp.concatenate(hs, axis=0), jnp.concatenate(ps, axis=0), h, p

    row = lax.broadcasted_iota(jnp.int32, (rows, bw), 0)
    in_last_segment = (row & (N_SEG - 1)) == N_SEG - 1
    group = row >> 3

    def step(c, carry):
        hf, pf, hb, pb = carry
        g0 = c * chunk
        a, drive = recurrence_terms(conv(g0), 0)
        hh, pp, hf, pf = scan(a, drive, hf, pf, reverse=False)
        r0 = pl.multiple_of(g0 * N_SEG, N_SEG)
        hf_ref[pl.ds(r0, rows), :] = hh
        pf_ref[pl.ds(r0, rows), :] = pp

        g1 = (nchunk - 1 - c) * chunk
        a, drive = recurrence_terms(conv(g1), 1)
        drive = jnp.where(in_last_segment & (group >= last_valid - g1), 0.0, drive)
        hh, pp, hb, pb = scan(a, drive, hb, pb, reverse=True)
        r1 = pl.multiple_of(g1 * N_SEG, N_SEG)
        hb_ref[pl.ds(r1, rows), :] = hh
        pb_ref[pl.ds(r1, rows), :] = pp
        return hf, pf, hb, pb

    zero = jnp.zeros((N_SEG, bw), F32)
    one = jnp.ones((N_SEG, bw), F32)
    hf, pf, hb, pb = lax.fori_loop(0, nchunk, step, (zero, one, zero, one), unroll=2)

    cin = [zero[0:1]]
    for r in range(1, N_SEG):
        cin.append(hf[r - 1:r] + pf[r - 1:r] * cin[r - 1])
    cinb = [zero[0:1]]
    for r in range(N_SEG - 2, -1, -1):
        cinb.append(hb[r + 1:r + 2] + pb[r + 1:r + 2] * cinb[-1])
    cin_f = jnp.concatenate([jnp.concatenate(cin, axis=0)] * chunk, axis=0)
    cin_b = jnp.concatenate([jnp.concatenate(cinb[::-1], axis=0)] * chunk, axis=0)

    def combine(c, carry):
        g0 = c * chunk
        r0 = pl.multiple_of(g0 * N_SEG, N_SEG)
        hsum = ((hf_ref[pl.ds(r0, rows), :] + pf_ref[pl.ds(r0, rows), :] * cin_f)
                + (hb_ref[pl.ds(r0, rows), :] + pb_ref[pl.ds(r0, rows), :] * cin_b))
        for g in range(chunk):
            lin_ref[pl.ds(LIN_PAD + g0 + g, N_SEG, stride=pitch), :] = hsum[g * N_SEG:(g + 1) * N_SEG]
        return carry

    lax.fori_loop(0, nchunk, combine, 0)

    def gate_out(c, carry):
        t0 = pl.multiple_of(c * lchunk, lchunk)
        gate = jax.nn.gelu(y_ref[pl.ds(t0, lchunk), :], approximate=True)
        o_ref[pl.ds(t0, lchunk), :] = (lin_ref[pl.ds(LIN_PAD + t0, lchunk), :] * gate).astype(BF16)
        return carry

    lax.fori_loop(0, seq // lchunk, gate_out, 0)


def _rglru(xy, conv_w, conv_b, wr, br, wi, bi, lam, nseq, seq):
    t = xy.shape[0]
    bw = C_WIDTH // C_BLOCKS
    pitch, chunk = _segment_plan(seq)
    lchunk = _tile(seq, 512)
    vec = lambda rows: pl.BlockSpec((rows, bw), lambda b, n: (0, n))
    mat = pl.BlockSpec((2, None, bw, bw), lambda b, n: (0, n, 0, 0))
    seg_major = pltpu.VMEM((pitch * N_SEG, bw), F32)
    return pl.pallas_call(
        functools.partial(_rglru_kernel, pitch=pitch, chunk=chunk, lchunk=lchunk),
        grid=(nseq, C_BLOCKS),
        in_specs=[pl.BlockSpec((seq, bw), lambda b, n: (b, n)),
                  pl.BlockSpec((seq, bw), lambda b, n: (b, C_BLOCKS + n)),
                  vec(CONV_W), vec(1), mat, vec(2), mat, vec(2), vec(2)],
        out_specs=pl.BlockSpec((seq, bw), lambda b, n: (b, n)),
        out_shape=jax.ShapeDtypeStruct((t, C_WIDTH), BF16),
        scratch_shapes=[pltpu.VMEM((LIN_PAD + pitch * N_SEG + 8, bw), F32),
                        seg_major, seg_major, seg_major, seg_major],
        compiler_params=_params("parallel", "parallel"),
        name="rglru",
    )(xy, xy, conv_w, conv_b.reshape(1, C_WIDTH), wr, br, wi, bi, lam)


def _merge_kernel(oa_ref, ob_ref, oc_ref, g0_ref, g1_ref, g2_ref, wb_ref, o_ref):
    acc = g0_ref[...] * jnp.dot(oa_ref[...], wb_ref[0], preferred_element_type=F32)
    acc = acc + g1_ref[...] * jnp.dot(ob_ref[...], wb_ref[1], preferred_element_type=F32)
    acc = acc + g2_ref[...] * jnp.dot(oc_ref[...], wb_ref[2], preferred_element_type=F32)
    o_ref[...] = acc.astype(BF16)


def _merge(oa, ob, oc, gates, wb):
    t, kw = oa.shape
    d = wb.shape[2]
    tm = _tile(t, 512)
    tn = _tile(d, 1024)
    ncb = d // tn
    o_spec = pl.BlockSpec((tm, kw), lambda j, i: (i, 0))
    g_spec = lambda br: pl.BlockSpec((tm, tn), lambda j, i: (i, br * ncb + j))
    return pl.pallas_call(
        _merge_kernel,
        grid=(ncb, t // tm),
        in_specs=[o_spec, o_spec, o_spec, g_spec(0), g_spec(1), g_spec(2),
                  pl.BlockSpec((N_BRANCH, kw, tn), lambda j, i: (0, 0, j))],
        out_specs=pl.BlockSpec((tm, tn), lambda j, i: (i, j)),
        out_shape=jax.ShapeDtypeStruct((t, d), BF16),
        compiler_params=_params("parallel", "parallel"),
        name="merge",
    )(oa, ob, oc, gates, gates, gates, wb)


def _out_kernel(m_ref, w_ref, gpost_ref, gnext_ref, *rest, nsplit, na):
    x_refs, (xo_ref, h_ref) = rest[:-2], rest[-2:]
    rows_per = m_ref.shape[0] // nsplit
    for r in range(nsplit):
        rows = slice(r * rows_per, (r + 1) * rows_per)
        y = jnp.dot(m_ref[rows, :], w_ref[...], preferred_element_type=F32)
        if len(x_refs) == 2:
            x = jnp.where(pl.program_id(0) < na, x_refs[0][rows, :], x_refs[1][rows, :])
        else:
            x = x_refs[0][rows, :]
        xn = x + _rms(y, gpost_ref[...])
        xo_ref[rows, :] = xn
        h_ref[rows, :] = _rms(xn, gnext_ref[...]).astype(BF16)


def _out_proj(merged, w, x, gpost, gnext):
    t, d = merged.shape
    row = lambda tm: pl.BlockSpec((tm, d), lambda i: (i, 0))
    if isinstance(x, tuple):
        tm = _tile(math.gcd(x[0].shape[0], x[1].shape[0]), 512)
        na, spec_a, spec_b = _two_part_rows(x[0], x[1], tm)
        x_args, x_specs = list(x), [spec_a, spec_b]
    else:
        tm = _tile(t, 512)
        na, x_args, x_specs = 0, [x], [row(tm)]
    gain = pl.BlockSpec((1, d), lambda i: (0, 0))
    return pl.pallas_call(
        functools.partial(_out_kernel, nsplit=4 if tm % 512 == 0 else 1, na=na),
        grid=(t // tm,),
        in_specs=[row(tm), pl.BlockSpec((d, d), lambda i: (0, 0)), gain, gain] + x_specs,
        out_specs=[row(tm), row(tm)],
        out_shape=[jax.ShapeDtypeStruct((t, d), F32), jax.ShapeDtypeStruct((t, d), BF16)],
        compiler_params=_params("parallel"),
        name="out_proj",
    )(merged, w, gpost.reshape(1, d), gnext.reshape(1, d), *x_args)


def _ffn_out_kernel(hid_ref, w_ref, x_ref, gpost_ref, gnext_ref, xo_ref, *rest, nsplit):
    h_ref = rest[0] if len(rest) == 2 else None
    xs_ref = rest[-1]
    k = pl.program_id(1)
    nk, _, slab = xs_ref.shape
    d = xo_ref.shape[1]

    xs_ref[k] = x_ref[...]

    @pl.when((k == 0) & (nk > 1))
    def _():
        xo_ref[...] = jnp.dot(hid_ref[...], w_ref[...], preferred_element_type=F32)

    @pl.when((k > 0) & (k < nk - 1))
    def _():
        xo_ref[...] += jnp.dot(hid_ref[...], w_ref[...], preferred_element_type=F32)

    @pl.when(k == nk - 1)
    def _():
        slabs = [slice(j * slab, (j + 1) * slab) for j in range(nk)]
        rows_per = xo_ref.shape[0] // nsplit
        for r in range(nsplit):
            rows = slice(r * rows_per, (r + 1) * rows_per)
            part = jnp.dot(hid_ref[rows, :], w_ref[...], preferred_element_type=F32)
            y = part if nk == 1 else xo_ref[rows, :] + part
            rs = lax.rsqrt(jnp.sum(jnp.square(y), axis=-1, keepdims=True) * (1.0 / d) + EPS)
            ssq = 0.0
            for j, sl in enumerate(slabs):
                xn = xs_ref[j, rows, :] + y[:, sl] * rs * gpost_ref[:, sl]
                xo_ref[rows, sl] = xn
                ssq = ssq + jnp.sum(jnp.square(xn), axis=-1, keepdims=True)
            if h_ref is not None:
                rs = lax.rsqrt(ssq * (1.0 / d) + EPS)
                for sl in slabs:
                    h_ref[rows, sl] = (xo_ref[rows, sl] * rs * gnext_ref[:, sl]).astype(BF16)


def _ffn_out(hid, w, x, gpost, gnext, row_start=0, nrows=None):
    d = x.shape[1]
    t = x.shape[0] if nrows is None else nrows
    ff = hid.shape[1]
    tm = _tile(math.gcd(row_start, t), 1024)
    tk = _tile(ff, 1024)
    nk = ff // tk
    slab = d // nk
    assert slab % HEAD_DIM == 0
    i0 = row_start // tm
    row_in = pl.BlockSpec((tm, slab), lambda i, k: (i0 + i, k))
    row = pl.BlockSpec((tm, d), lambda i, k: (i, 0))
    gain = pl.BlockSpec((1, d), lambda i, k: (0, 0))
    with_next = gnext is not None
    out_specs = [row, row] if with_next else [row]
    out_shape = [jax.ShapeDtypeStruct((t, d), F32)] + ([jax.ShapeDtypeStruct((t, d), BF16)] if with_next else [])
    gn = gnext if with_next else gpost
    outs = pl.pallas_call(
        functools.partial(_ffn_out_kernel, nsplit=4 if tm % 512 == 0 else 1),
        grid=(t // tm, ff // tk),
        in_specs=[pl.BlockSpec((tm, tk), lambda i, k: (i0 + i, k)),
                  pl.BlockSpec((tk, d), lambda i, k: (k, 0)),
                  row_in, gain, gain],
        out_specs=out_specs,
        out_shape=out_shape,
        scratch_shapes=[pltpu.VMEM((nk, tm, slab), F32)],
        compiler_params=_params("parallel", "arbitrary"),
        name="ffn_out",
    )(hid, w, x, gpost.reshape(1, d), gn.reshape(1, d))
    return (outs[0], outs[1]) if with_next else (outs[0], None)


def _rope_tables(ang):
    cos = jnp.cos(ang)
    sin = jnp.sin(ang)
    return jnp.concatenate([cos, cos], axis=-1), jnp.concatenate([-sin, sin], axis=-1)


def _angles_1d(s):
    t = jnp.arange(s, dtype=F32)
    inv = ROPE_THETA ** (-jnp.arange(0, HEAD_DIM, 2, dtype=F32) / HEAD_DIM)
    return t[:, None] * inv[None, :]


def _angles_axial(s):
    rows = s // GRID_W
    row = jnp.repeat(jnp.arange(rows, dtype=F32), GRID_W)
    col = jnp.tile(jnp.arange(GRID_W, dtype=F32), rows)
    half = HEAD_DIM // 2
    inv = ROPE_THETA ** (-jnp.arange(0, half, 2, dtype=F32) / half)
    return jnp.concatenate([row[:, None] * inv[None, :], col[:, None] * inv[None, :]], axis=-1)


def kernel(x_prompt, x_sample, norm_mix_pre, norm_mix_post, norm_ffn_pre, norm_ffn_post, w_in, q_norm_a, k_norm_a, sink_b, conv_w, conv_b, gate_r_w, gate_r_b, gate_i_w, gate_i_b, lru_lambda, w_branch, w_out, w_ffn_in, w_ffn_out):
    nb_p, seq, d = x_prompt.shape
    nb_s, seq_s, _ = x_sample.shape
    assert seq == seq_s
    nseq = nb_p + nb_s
    depth = w_in.shape[0]
    x = (x_prompt.reshape(nb_p * seq, d), x_sample.reshape(nb_s * seq, d))

    cos_ax, sin_ax = _rope_tables(_angles_axial(seq))
    cos_1d, sin_1d = _rope_tables(_angles_1d(seq))

    qa0 = 0
    ka0 = qa0 + A_HEADS * HEAD_DIM
    va0 = ka0 + A_KV * HEAD_DIM
    qb0 = va0 + A_KV * HEAD_DIM
    kb0 = qb0 + B_HEADS * HEAD_DIM
    vb0 = kb0 + B_KV * HEAD_DIM
    xc0 = vb0 + B_KV * HEAD_DIM
    g0 = xc0 + 2 * C_WIDTH
    cols = lambda w, a, b: w[:, a:b]

    h = _prenorm(x[0], x[1], norm_mix_pre[0])
    for l in range(depth):
        wl = w_in[l]
        w_qka = jnp.concatenate([cols(wl, qa0, ka0), cols(wl, ka0, va0)], axis=1).astype(BF16)
        w_qkb = jnp.concatenate([cols(wl, qb0, kb0), cols(wl, kb0, vb0)], axis=1).astype(BF16)
        w_v = jnp.concatenate([cols(wl, va0, qb0), cols(wl, vb0, xc0)], axis=1).astype(BF16)
        w_xy = cols(wl, xc0, g0).astype(BF16)
        w_g = wl[:, g0:].astype(BF16)
        gains_a = jnp.concatenate([jnp.tile(q_norm_a[l] * ATTN_QSCALE, A_HEADS),
                                   jnp.tile(k_norm_a[l], A_KV)]).reshape(1, -1)
        gains_b = jnp.concatenate([jnp.full((B_HEADS * HEAD_DIM,), ATTN_QSCALE, F32),
                                   jnp.ones((B_KV * HEAD_DIM,), F32)]).reshape(1, -1)

        qk_a = _proj_rope(h, w_qka, gains_a, cos_ax, sin_ax, seq, normed=True)
        qk_b = _proj_rope(h, w_qkb, gains_b, cos_1d, sin_1d, seq, normed=False)
        vt_a, vt_b = _proj_v(h, w_v, _attn_key_tile(seq))
        xy = _proj(h, w_xy, None, F32, "proj_xy")
        gates = _proj(h, w_g, "sigmoid", F32, "proj_gates")

        oa = _attn_global(qk_a, vt_a, nseq, seq)
        ob = _attn_window(qk_b, vt_b, sink_b[l], nseq, seq)
        oc = _rglru(xy, conv_w[l], conv_b[l], (0.5 * gate_r_w[l]).astype(BF16), 0.5 * gate_r_b[l],
                    (0.5 * gate_i_w[l]).astype(BF16), 0.5 * gate_i_b[l], lru_lambda[l], nseq, seq)

        merged = _merge(oa, ob, oc, gates, w_branch[l].astype(BF16))
        x, h2 = _out_proj(merged, w_out[l].astype(BF16), x, norm_mix_post[l], norm_ffn_pre[l])
        hid = _proj(h2, w_ffn_in[l].astype(BF16), "relu2", BF16, "ffn_in")
        w2 = w_ffn_out[l].astype(BF16)
        if l + 1 < depth:
            x, h = _ffn_out(hid, w2, x, norm_ffn_post[l], norm_mix_pre[l + 1])

    rows_p = nb_p * seq
    y_p, _ = _ffn_out(hid, w2, x, norm_ffn_post[depth - 1], None, 0, rows_p)
    y_s, _ = _ffn_out(hid, w2, x, norm_ffn_post[depth - 1], None, rows_p, nb_s * seq)
    return (y_p.reshape(nb_p, seq, d), y_s.reshape(nb_s, seq, d))
```

```python
import functools
import math

import jax
import jax.numpy as jnp
from jax import lax
from jax.experimental import pallas as pl
from jax.experimental.pallas import tpu as pltpu

F32 = jnp.float32
BF16 = jnp.bfloat16

HEAD_DIM = 128
A_HEADS = 8
A_KV = 2
B_HEADS = 8
B_KV = 2
C_WIDTH = 1024
C_BLOCKS = 8
CONV_W = 4
LRU_C = 8.0
MIX_W = 1024
N_BRANCH = 3
WINDOW = 128
GRID_W = 64
ROPE_THETA = 10000.0
EPS = 1e-6
NEG_BIG = -1e30
ATTN_QSCALE = HEAD_DIM ** -0.5 * math.log2(math.e)
ATTN_Q_TILES_PER_STEP = 4

VMEM_LIMIT_BYTES = 56 * 1024 * 1024


def _params(*sem):
    return pltpu.CompilerParams(dimension_semantics=sem, vmem_limit_bytes=VMEM_LIMIT_BYTES)


def _tile(n, pref):
    t = min(n, pref)
    assert n % t == 0, (n, t)
    return t


def _rms(x, g):
    return x * lax.rsqrt(jnp.mean(x * x, axis=-1, keepdims=True) + EPS) * g


def _rope(x, cos, sin_signed):
    return x * cos + pltpu.roll(x, HEAD_DIM // 2, axis=1) * sin_signed


def _two_part_rows(xa, xb, tm):
    d = xa.shape[1]
    na = xa.shape[0] // tm
    spec_a = pl.BlockSpec((tm, d), lambda i: (jnp.minimum(i, na - 1), 0))
    spec_b = pl.BlockSpec((tm, d), lambda i: (jnp.maximum(i - na, 0), 0))
    return na, spec_a, spec_b


def _prenorm_kernel(xa_ref, xb_ref, g_ref, h_ref, *, na):
    x = jnp.where(pl.program_id(0) < na, xa_ref[...], xb_ref[...])
    h_ref[...] = _rms(x, g_ref[...]).astype(BF16)


def _prenorm(xa, xb, g):
    d = xa.shape[1]
    t = xa.shape[0] + xb.shape[0]
    tm = _tile(math.gcd(xa.shape[0], xb.shape[0]), 512)
    na, spec_a, spec_b = _two_part_rows(xa, xb, tm)
    return pl.pallas_call(
        functools.partial(_prenorm_kernel, na=na),
        grid=(t // tm,),
        in_specs=[spec_a, spec_b, pl.BlockSpec((1, d), lambda i: (0, 0))],
        out_specs=pl.BlockSpec((tm, d), lambda i: (i, 0)),
        out_shape=jax.ShapeDtypeStruct((t, d), BF16),
        compiler_params=_params("parallel"),
        name="prenorm",
    )(xa, xb, g.reshape(1, d))


def _proj_rope_kernel(h_ref, w_ref, g_ref, cos_ref, sin_ref, o_ref, *, normed, nsplit):
    rows_per = h_ref.shape[0] // nsplit
    for r in range(nsplit):
        rows = slice(r * rows_per, (r + 1) * rows_per)
        acc = jnp.dot(h_ref[rows, :], w_ref[...], preferred_element_type=F32)
        cos = cos_ref[rows, :]
        sin = sin_ref[rows, :]
        for hd in range(acc.shape[1] // HEAD_DIM):
            sl = slice(hd * HEAD_DIM, (hd + 1) * HEAD_DIM)
            xh = acc[:, sl]
            xh = _rms(xh, g_ref[:, sl]) if normed else xh * g_ref[:, sl]
            o_ref[rows, sl] = _rope(xh, cos, sin).astype(BF16)


def _proj_rope(h, w, gains, cos, sin, seq, normed):
    t, d = h.shape
    n = w.shape[1]
    tm = _tile(seq, 1024)
    nsb = seq // tm
    return pl.pallas_call(
        functools.partial(_proj_rope_kernel, normed=normed, nsplit=tm // 128 if tm % 512 == 0 else 1),
        grid=(t // tm,),
        in_specs=[pl.BlockSpec((tm, d), lambda i: (i, 0)),
                  pl.BlockSpec((d, n), lambda i: (0, 0)),
                  pl.BlockSpec((1, n), lambda i: (0, 0)),
                  pl.BlockSpec((tm, HEAD_DIM), lambda i: (i % nsb, 0)),
                  pl.BlockSpec((tm, HEAD_DIM), lambda i: (i % nsb, 0))],
        out_specs=pl.BlockSpec((tm, n), lambda i: (i, 0)),
        out_shape=jax.ShapeDtypeStruct((t, n), BF16),
        compiler_params=_params("parallel"),
        name="proj_rope_normed" if normed else "proj_rope",
    )(h, w, gains, cos, sin)


def _proj_kernel(h_ref, w_ref, o_ref, *, act):
    acc = jnp.dot(h_ref[...], w_ref[...], preferred_element_type=F32)
    if act == "sigmoid":
        acc = 0.5 * jnp.tanh(0.5 * acc) + 0.5
    elif act == "relu2":
        acc = jnp.square(jnp.maximum(acc, 0.0))
    o_ref[...] = acc.astype(o_ref.dtype)


def _proj(h, w, act, out_dtype, name):
    t, d = h.shape
    n = w.shape[1]
    tm = _tile(t, 1024)
    tn = _tile(n, 2048)
    return pl.pallas_call(
        functools.partial(_proj_kernel, act=act),
        grid=(n // tn, t // tm),
        in_specs=[pl.BlockSpec((tm, d), lambda j, i: (i, 0)),
                  pl.BlockSpec((d, tn), lambda j, i: (0, j))],
        out_specs=pl.BlockSpec((tm, tn), lambda j, i: (i, j)),
        out_shape=jax.ShapeDtypeStruct((t, n), out_dtype),
        compiler_params=_params("parallel", "parallel"),
        name=name,
    )(h, w)


VT_ONES = 16
VT_ROWS = HEAD_DIM + VT_ONES


WIN_KEY_TILE = HEAD_DIM


def _proj_v_kernel(h_ref, w_ref, vta_ref, vtb_ref):
    acc = jnp.dot(h_ref[...], w_ref[...], preferred_element_type=F32)
    rows = acc.shape[0]
    for hh in range(A_KV):
        vta_ref[hh * VT_ROWS:hh * VT_ROWS + HEAD_DIM, :] = acc[:, hh * HEAD_DIM:(hh + 1) * HEAD_DIM].T.astype(BF16)
        vta_ref[hh * VT_ROWS + HEAD_DIM:(hh + 1) * VT_ROWS, :] = jnp.ones((VT_ONES, rows), BF16)
    for hh in range(B_KV):
        vt = acc[:, (A_KV + hh) * HEAD_DIM:(A_KV + hh + 1) * HEAD_DIM].T.astype(BF16)
        for j in range(rows // WIN_KEY_TILE):
            vtb_ref[j, hh * VT_ROWS:hh * VT_ROWS + HEAD_DIM, :] = vt[:, j * WIN_KEY_TILE:(j + 1) * WIN_KEY_TILE]
            vtb_ref[j, hh * VT_ROWS + HEAD_DIM:(hh + 1) * VT_ROWS, :] = jnp.ones((VT_ONES, WIN_KEY_TILE), BF16)


def _proj_v(h, w, tk):
    t, d = h.shape
    n = w.shape[1]
    per_step = tk // WIN_KEY_TILE
    return pl.pallas_call(
        _proj_v_kernel,
        grid=(t // tk,),
        in_specs=[pl.BlockSpec((tk, d), lambda i: (i, 0)),
                  pl.BlockSpec((d, n), lambda i: (0, 0))],
        out_specs=[pl.BlockSpec((None, A_KV * VT_ROWS, tk), lambda i: (i, 0, 0)),
                   pl.BlockSpec((per_step, B_KV * VT_ROWS, WIN_KEY_TILE), lambda i: (i, 0, 0))],
        out_shape=[jax.ShapeDtypeStruct((t // tk, A_KV * VT_ROWS, tk), BF16),
                   jax.ShapeDtypeStruct((t // WIN_KEY_TILE, B_KV * VT_ROWS, WIN_KEY_TILE), BF16)],
        compiler_params=_params("parallel"),
        name="proj_v",
    )(h, w)


def _attn_global_kernel(q_ref, k_ref, vt_ref, o_ref, m_ref, acc_ref, s0_ref, s1_ref, *, tq, tk, n_rep):
    seq = k_ref.shape[0]
    nk = seq // tk
    ntile = q_ref.shape[0] // tq

    def stacked_q(t):
        return jnp.concatenate([q_ref[t * tq:(t + 1) * tq, g * HEAD_DIM:(g + 1) * HEAD_DIM]
                                for g in range(n_rep)], axis=0)

    def scores(c, q, s_ref):
        off = pl.multiple_of(c * tk, tk)
        s_ref[...] = lax.dot_general(k_ref[pl.ds(off, tk), :], q, (((1,), (1,)), ((), ())),
                                     preferred_element_type=F32)

    def update(c, s_ref):
        s = s_ref[...]
        m_prev = m_ref[...]
        m_new = jnp.maximum(m_prev, jnp.max(s, axis=0, keepdims=True))
        alpha = jnp.exp2(m_prev - m_new)
        p = jnp.exp2(s - jnp.concatenate([m_new] * (tk // 8), axis=0))
        acc_ref[...] = (jnp.concatenate([alpha] * (VT_ROWS // 8), axis=0) * acc_ref[...]
                        + jnp.dot(vt_ref[c], p.astype(BF16), preferred_element_type=F32))
        m_ref[...] = m_new

    q = stacked_q(0)
    scores(0, q, s0_ref)
    for t in range(ntile):
        q_next = stacked_q(t + 1) if t + 1 < ntile else q
        wrap_to = 0 if t + 1 < ntile else nk - 1
        m_ref[...] = jnp.full(m_ref.shape, NEG_BIG, F32)
        acc_ref[...] = jnp.zeros(acc_ref.shape, F32)

        def pair(i, carry, q=q, q_next=q_next, wrap_to=wrap_to):
            c = 2 * i
            scores(c + 1, q, s1_ref)
            update(c, s0_ref)
            wrap = c + 2 >= nk
            scores(jnp.where(wrap, wrap_to, c + 2), jnp.where(wrap, q_next, q), s0_ref)
            update(c + 1, s1_ref)
            return carry

        lax.fori_loop(0, nk // 2, pair, 0, unroll=min(4, nk // 2))
        acc = acc_ref[...]
        denom = jnp.concatenate([acc[HEAD_DIM:HEAD_DIM + 8]] * (HEAD_DIM // 8), axis=0)
        o = (acc[:HEAD_DIM] / denom).T
        for g in range(n_rep):
            o_ref[t * tq:(t + 1) * tq, g * HEAD_DIM:(g + 1) * HEAD_DIM] = o[g * tq:(g + 1) * tq].astype(BF16)
        q = q_next


def _attn_key_tile(seq):
    tk = _tile(seq // 2, 512)
    assert (seq // tk) % 2 == 0
    return tk


def _attn_global(qk, vt, nseq, seq):
    t = qk.shape[0]
    n_rep = A_HEADS // A_KV
    tq = _tile(seq, 256)
    tk = _attn_key_tile(seq)
    tb = _tile(seq, ATTN_Q_TILES_PER_STEP * tq)
    nqb = seq // tb
    gw = n_rep * HEAD_DIM
    return pl.pallas_call(
        functools.partial(_attn_global_kernel, tq=tq, tk=tk, n_rep=n_rep),
        grid=(nseq, A_KV, nqb),
        in_specs=[pl.BlockSpec((tb, gw), lambda b, h, i: (b * nqb + i, h)),
                  pl.BlockSpec((seq, HEAD_DIM), lambda b, h, i: (b, A_HEADS + h)),
                  pl.BlockSpec((seq // tk, VT_ROWS, tk), lambda b, h, i: (b, h, 0))],
        out_specs=pl.BlockSpec((tb, gw), lambda b, h, i: (b * nqb + i, h)),
        out_shape=jax.ShapeDtypeStruct((t, A_HEADS * HEAD_DIM), BF16),
        scratch_shapes=[pltpu.VMEM((8, n_rep * tq), F32),
                        pltpu.VMEM((VT_ROWS, n_rep * tq), F32),
                        pltpu.VMEM((tk, n_rep * tq), F32),
                        pltpu.VMEM((tk, n_rep * tq), F32)],
        compiler_params=_params("parallel", "parallel", "parallel"),
        name="attn_global",
    )(qk, qk, vt)


def _attn_window_kernel(sink_ref, q_ref, k_ref, vt_ref, o_ref, *, tq, nsub, n_rep):
    seq = k_ref.shape[0]
    wk = min(seq, tq + 2 * WINDOW)
    h = pl.program_id(1)
    i = pl.program_id(2)
    sink = jnp.concatenate([jnp.full((8, tq), sink_ref[h * n_rep + g] * math.log2(math.e), F32)
                            for g in range(n_rep)], axis=1)
    key = lax.broadcasted_iota(jnp.int32, (wk, tq), 0)
    qry = lax.broadcasted_iota(jnp.int32, (wk, tq), 1)
    for j in range(nsub):
        q0 = (i * nsub + j) * tq
        start = pl.multiple_of(jnp.clip(q0 - WINDOW, 0, seq - wk), WIN_KEY_TILE)
        k = k_ref[pl.ds(start, wk), :]
        kt0 = start // WIN_KEY_TILE
        vt = jnp.concatenate([vt_ref[kt0 + u] for u in range(wk // WIN_KEY_TILE)], axis=1)
        band = jnp.where(jnp.abs(qry - key + (q0 - start)) <= WINDOW, 0.0, NEG_BIG)
        q = jnp.concatenate([q_ref[j * tq:(j + 1) * tq, g * HEAD_DIM:(g + 1) * HEAD_DIM]
                             for g in range(n_rep)], axis=0)
        s = lax.dot_general(k, q, (((1,), (1,)), ((), ())), preferred_element_type=F32)
        s = s + jnp.concatenate([band] * n_rep, axis=1)
        m = jnp.maximum(jnp.max(s, axis=0, keepdims=True), sink)
        p = jnp.exp2(s - jnp.concatenate([m] * (wk // 8), axis=0))
        acc = jnp.dot(vt, p.astype(BF16), preferred_element_type=F32)
        denom = acc[HEAD_DIM:HEAD_DIM + 8] + jnp.exp2(sink - m)
        o = (acc[:HEAD_DIM] / jnp.concatenate([denom] * (HEAD_DIM // 8), axis=0)).T
        for g in range(n_rep):
            o_ref[j * tq:(j + 1) * tq, g * HEAD_DIM:(g + 1) * HEAD_DIM] = o[g * tq:(g + 1) * tq].astype(BF16)


def _attn_window(qk, vt, sink, nseq, seq):
    t = qk.shape[0]
    n_rep = B_HEADS // B_KV
    tq = _tile(seq, 256)
    nsub = 2 if seq % (2 * tq) == 0 else 1
    tb = nsub * tq
    nqb = seq // tb
    gw = n_rep * HEAD_DIM
    return pl.pallas_call(
        functools.partial(_attn_window_kernel, tq=tq, nsub=nsub, n_rep=n_rep),
        grid=(nseq, B_KV, nqb),
        in_specs=[pl.BlockSpec(memory_space=pltpu.SMEM),
                  pl.BlockSpec((tb, gw), lambda b, h, i: (b * nqb + i, h)),
                  pl.BlockSpec((seq, HEAD_DIM), lambda b, h, i: (b, B_HEADS + h)),
                  pl.BlockSpec((seq // WIN_KEY_TILE, VT_ROWS, WIN_KEY_TILE), lambda b, h, i: (b, h, 0))],
        out_specs=pl.BlockSpec((tb, gw), lambda b, h, i: (b * nqb + i, h)),
        out_shape=jax.ShapeDtypeStruct((t, B_HEADS * HEAD_DIM), BF16),
        compiler_params=_params("parallel", "parallel", "parallel"),
        name="attn_window",
    )(sink, qk, qk, vt)


N_SEG = 8
LIN_PAD = 8


def _segment_plan(seq):
    m = -(-seq // (4 * N_SEG))
    m += 1 - m % 2
    while True:
        for dv in range(1, m + 1):
            if m % dv == 0 and (16 <= 4 * dv <= 64 or (dv == m and 4 * dv <= 96)):
                assert (N_SEG - 1) * 4 * m < seq <= N_SEG * 4 * m
                return 4 * m, 4 * dv
        m += 2


def _rglru_kernel(x_ref, y_ref, cw_ref, cb_ref, wr_ref, br_ref, wi_ref, bi_ref, lam_ref, o_ref,
                  lin_ref, hf_ref, pf_ref, hb_ref, pb_ref, *, pitch, chunk, lchunk):
    seq, bw = x_ref.shape
    nchunk = pitch // chunk
    rows = chunk * N_SEG
    left = CONV_W // 2
    last_valid = seq - (N_SEG - 1) * pitch
    lin_rows = lin_ref.shape[0]

    lin_ref[0:LIN_PAD, :] = jnp.zeros((LIN_PAD, bw), F32)
    lin_ref[LIN_PAD + seq:lin_rows, :] = jnp.zeros((lin_rows - LIN_PAD - seq, bw), F32)

    def copy_in(c, carry):
        t0 = pl.multiple_of(c * lchunk, lchunk)
        lin_ref[pl.ds(LIN_PAD + t0, lchunk), :] = x_ref[pl.ds(t0, lchunk), :]
        return carry

    lax.fori_loop(0, seq // lchunk, copy_in, 0)

    lam = lam_ref[...]
    log_sig = jnp.minimum(lam, 0.0) - jnp.log1p(jnp.exp(-jnp.abs(lam)))

    def conv(g0):
        u = cb_ref[...]
        for j in range(CONV_W):
            xt = jnp.concatenate(
                [lin_ref[pl.ds(LIN_PAD + g0 + g + j - left, N_SEG, stride=pitch), :] for g in range(chunk)],
                axis=0)
            u = u + xt * cw_ref[j:j + 1, :]
        return u

    half_c_log_sig = (0.5 * LRU_C) * log_sig

    def recurrence_terms(u, d):
        ub = u.astype(BF16)
        tr = jnp.tanh(jnp.dot(ub, wr_ref[d], preferred_element_type=F32) + br_ref[d:d + 1, :])
        ti = jnp.tanh(jnp.dot(ub, wi_ref[d], preferred_element_type=F32) + bi_ref[d:d + 1, :])
        k = half_c_log_sig[d:d + 1, :]
        log_a = tr * k + k
        a = jnp.exp(log_a)
        th = jnp.tanh(log_a)
        z = -2.0 * th / (1.0 - th)
        uh = 0.5 * u
        drive = z * lax.rsqrt(jnp.maximum(z, 1e-30)) * (ti * uh + uh)
        return a, drive

    def scan(a, b, h, p, reverse):
        hs = [None] * chunk
        ps = [None] * chunk
        for g in (range(chunk - 1, -1, -1) if reverse else range(chunk)):
            ag = a[g * N_SEG:(g + 1) * N_SEG]
            h = ag * h + b[g * N_SEG:(g + 1) * N_SEG]
            p = ag * p
            hs[g] = h
            ps[g] = p
        return jnp.concatenate(hs, axis=0), jnp.concatenate(ps, axis=0), h, p

    row = lax.broadcasted_iota(jnp.int32, (rows, bw), 0)
    in_last_segment = (row & (N_SEG - 1)) == N_SEG - 1
    group = row >> 3

    def step(c, carry):
        hf, pf, hb, pb = carry
        g0 = c * chunk
        a, drive = recurrence_terms(conv(g0), 0)
        hh, pp, hf, pf = scan(a, drive, hf, pf, reverse=False)
        r0 = pl.multiple_of(g0 * N_SEG, N_SEG)
        hf_ref[pl.ds(r0, rows), :] = hh
        pf_ref[pl.ds(r0, rows), :] = pp

        g1 = (nchunk - 1 - c) * chunk
        a, drive = recurrence_terms(conv(g1), 1)
        drive = jnp.where(in_last_segment & (group >= last_valid - g1), 0.0, drive)
        hh, pp, hb, pb = scan(a, drive, hb, pb, reverse=True)
        r1 = pl.multiple_of(g1 * N_SEG, N_SEG)
        hb_ref[pl.ds(r1, rows), :] = hh
        pb_ref[pl.ds(r1, rows), :] = pp
        return hf, pf, hb, pb

    zero = jnp.zeros((N_SEG, bw), F32)
    one = jnp.ones((N_SEG, bw), F32)
    hf, pf, hb, pb = lax.fori_loop(0, nchunk, step, (zero, one, zero, one), unroll=2)

    cin = [zero[0:1]]
    for r in range(1, N_SEG):
        cin.append(hf[r - 1:r] + pf[r - 1:r] * cin[r - 1])
    cinb = [zero[0:1]]
    for r in range(N_SEG - 2, -1, -1):
        cinb.append(hb[r + 1:r + 2] + pb[r + 1:r + 2] * cinb[-1])
    cin_f = jnp.concatenate([jnp.concatenate(cin, axis=0)] * chunk, axis=0)
    cin_b = jnp.concatenate([jnp.concatenate(cinb[::-1], axis=0)] * chunk, axis=0)

    def combine(c, carry):
        g0 = c * chunk
        r0 = pl.multiple_of(g0 * N_SEG, N_SEG)
        hsum = ((hf_ref[pl.ds(r0, rows), :] + pf_ref[pl.ds(r0, rows), :] * cin_f)
                + (hb_ref[pl.ds(r0, rows), :] + pb_ref[pl.ds(r0, rows), :] * cin_b))
        for g in range(chunk):
            lin_ref[pl.ds(LIN_PAD + g0 + g, N_SEG, stride=pitch), :] = hsum[g * N_SEG:(g + 1) * N_SEG]
        return carry

    lax.fori_loop(0, nchunk, combine, 0)

    def gate_out(c, carry):
        t0 = pl.multiple_of(c * lchunk, lchunk)
        gate = jax.nn.gelu(y_ref[pl.ds(t0, lchunk), :], approximate=True)
        o_ref[pl.ds(t0, lchunk), :] = (lin_ref[pl.ds(LIN_PAD + t0, lchunk), :] * gate).astype(BF16)
        return carry

    lax.fori_loop(0, seq // lchunk, gate_out, 0)


def _rglru(xy, conv_w, conv_b, wr, br, wi, bi, lam, nseq, seq):
    t = xy.shape[0]
    bw = C_WIDTH // C_BLOCKS
    pitch, chunk = _segment_plan(seq)
    lchunk = _tile(seq, 512)
    vec = lambda rows: pl.BlockSpec((rows, bw), lambda b, n: (0, n))
    mat = pl.BlockSpec((2, None, bw, bw), lambda b, n: (0, n, 0, 0))
    seg_major = pltpu.VMEM((pitch * N_SEG, bw), F32)
    return pl.pallas_call(
        functools.partial(_rglru_kernel, pitch=pitch, chunk=chunk, lchunk=lchunk),
        grid=(nseq, C_BLOCKS),
        in_specs=[pl.BlockSpec((seq, bw), lambda b, n: (b, n)),
                  pl.BlockSpec((seq, bw), lambda b, n: (b, C_BLOCKS + n)),
                  vec(CONV_W), vec(1), mat, vec(2), mat, vec(2), vec(2)],
        out_specs=pl.BlockSpec((seq, bw), lambda b, n: (b, n)),
        out_shape=jax.ShapeDtypeStruct((t, C_WIDTH), BF16),
        scratch_shapes=[pltpu.VMEM((LIN_PAD + pitch * N_SEG + 8, bw), F32),
                        seg_major, seg_major, seg_major, seg_major],
        compiler_params=_params("parallel", "parallel"),
        name="rglru",
    )(xy, xy, conv_w, conv_b.reshape(1, C_WIDTH), wr, br, wi, bi, lam)


def _merge_kernel(oa_ref, ob_ref, oc_ref, g0_ref, g1_ref, g2_ref, wb_ref, o_ref):
    acc = g0_ref[...] * jnp.dot(oa_ref[...], wb_ref[0], preferred_element_type=F32)
    acc = acc + g1_ref[...] * jnp.dot(ob_ref[...], wb_ref[1], preferred_element_type=F32)
    acc = acc + g2_ref[...] * jnp.dot(oc_ref[...], wb_ref[2], preferred_element_type=F32)
    o_ref[...] = acc.astype(BF16)


def _merge(oa, ob, oc, gates, wb):
    t, kw = oa.shape
    d = wb.shape[2]
    tm = _tile(t, 256)
    tn = _tile(d, 2048)
    ncb = d // tn
    o_spec = pl.BlockSpec((tm, kw), lambda j, i: (i, 0))
    g_spec = lambda br: pl.BlockSpec((tm, tn), lambda j, i: (i, br * ncb + j))
    return pl.pallas_call(
        _merge_kernel,
        grid=(ncb, t // tm),
        in_specs=[o_spec, o_spec, o_spec, g_spec(0), g_spec(1), g_spec(2),
                  pl.BlockSpec((N_BRANCH, kw, tn), lambda j, i: (0, 0, j))],
        out_specs=pl.BlockSpec((tm, tn), lambda j, i: (i, j)),
        out_shape=jax.ShapeDtypeStruct((t, d), BF16),
        compiler_params=_params("parallel", "parallel"),
        name="merge",
    )(oa, ob, oc, gates, gates, gates, wb)


def _out_kernel(m_ref, w_ref, gpost_ref, gnext_ref, *rest, nsplit, na):
    x_refs, (xo_ref, h_ref) = rest[:-2], rest[-2:]
    rows_per = m_ref.shape[0] // nsplit
    for r in range(nsplit):
        rows = slice(r * rows_per, (r + 1) * rows_per)
        y = jnp.dot(m_ref[rows, :], w_ref[...], preferred_element_type=F32)
        if len(x_refs) == 2:
            x = jnp.where(pl.program_id(0) < na, x_refs[0][rows, :], x_refs[1][rows, :])
        else:
            x = x_refs[0][rows, :]
        xn = x + _rms(y, gpost_ref[...])
        xo_ref[rows, :] = xn
        h_ref[rows, :] = _rms(xn, gnext_ref[...]).astype(BF16)


def _out_proj(merged, w, x, gpost, gnext):
    t, d = merged.shape
    row = lambda tm: pl.BlockSpec((tm, d), lambda i: (i, 0))
    if isinstance(x, tuple):
        tm = _tile(math.gcd(x[0].shape[0], x[1].shape[0]), 512)
        na, spec_a, spec_b = _two_part_rows(x[0], x[1], tm)
        x_args, x_specs = list(x), [spec_a, spec_b]
    else:
        tm = _tile(t, 512)
        na, x_args, x_specs = 0, [x], [row(tm)]
    gain = pl.BlockSpec((1, d), lambda i: (0, 0))
    return pl.pallas_call(
        functools.partial(_out_kernel, nsplit=4 if tm % 512 == 0 else 1, na=na),
        grid=(t // tm,),
        in_specs=[row(tm), pl.BlockSpec((d, d), lambda i: (0, 0)), gain, gain] + x_specs,
        out_specs=[row(tm), row(tm)],
        out_shape=[jax.ShapeDtypeStruct((t, d), F32), jax.ShapeDtypeStruct((t, d), BF16)],
        compiler_params=_params("parallel"),
        name="out_proj",
    )(merged, w, gpost.reshape(1, d), gnext.reshape(1, d), *x_args)


def _ffn_out_kernel(hid_ref, w_ref, x_ref, gpost_ref, gnext_ref, xo_ref, *rest, nsplit):
    h_ref = rest[0] if len(rest) == 2 else None
    xs_ref = rest[-1]
    k = pl.program_id(1)
    nk, _, slab = xs_ref.shape
    d = xo_ref.shape[1]

    xs_ref[k] = x_ref[...]

    @pl.when((k == 0) & (nk > 1))
    def _():
        xo_ref[...] = jnp.dot(hid_ref[...], w_ref[...], preferred_element_type=F32)

    @pl.when((k > 0) & (k < nk - 1))
    def _():
        xo_ref[...] += jnp.dot(hid_ref[...], w_ref[...], preferred_element_type=F32)

    @pl.when(k == nk - 1)
    def _():
        slabs = [slice(j * slab, (j + 1) * slab) for j in range(nk)]
        rows_per = xo_ref.shape[0] // nsplit
        for r in range(nsplit):
            rows = slice(r * rows_per, (r + 1) * rows_per)
            part = jnp.dot(hid_ref[rows, :], w_ref[...], preferred_element_type=F32)
            y = part if nk == 1 else xo_ref[rows, :] + part
            rs = lax.rsqrt(jnp.sum(jnp.square(y), axis=-1, keepdims=True) * (1.0 / d) + EPS)
            ssq = 0.0
            for j, sl in enumerate(slabs):
                xn = xs_ref[j, rows, :] + y[:, sl] * rs * gpost_ref[:, sl]
                xo_ref[rows, sl] = xn
                ssq = ssq + jnp.sum(jnp.square(xn), axis=-1, keepdims=True)
            if h_ref is not None:
                rs = lax.rsqrt(ssq * (1.0 / d) + EPS)
                for sl in slabs:
                    h_ref[rows, sl] = (xo_ref[rows, sl] * rs * gnext_ref[:, sl]).astype(BF16)


def _ffn_out(hid, w, x, gpost, gnext, row_start=0, nrows=None):
    d = x.shape[1]
    t = x.shape[0] if nrows is None else nrows
    ff = hid.shape[1]
    tm = _tile(math.gcd(row_start, t), 1024)
    tk = _tile(ff, 1024)
    nk = ff // tk
    slab = d // nk
    assert slab % HEAD_DIM == 0
    i0 = row_start // tm
    row_in = pl.BlockSpec((tm, slab), lambda i, k: (i0 + i, k))
    row = pl.BlockSpec((tm, d), lambda i, k: (i, 0))
    gain = pl.BlockSpec((1, d), lambda i, k: (0, 0))
    with_next = gnext is not None
    out_specs = [row, row] if with_next else [row]
    out_shape = [jax.ShapeDtypeStruct((t, d), F32)] + ([jax.ShapeDtypeStruct((t, d), BF16)] if with_next else [])
    gn = gnext if with_next else gpost
    outs = pl.pallas_call(
        functools.partial(_ffn_out_kernel, nsplit=4 if tm % 512 == 0 else 1),
        grid=(t // tm, ff // tk),
        in_specs=[pl.BlockSpec((tm, tk), lambda i, k: (i0 + i, k)),
                  pl.BlockSpec((tk, d), lambda i, k: (k, 0)),
                  row_in, gain, gain],
        out_specs=out_specs,
        out_shape=out_shape,
        scratch_shapes=[pltpu.VMEM((nk, tm, slab), F32)],
        compiler_params=_params("parallel", "arbitrary"),
        name="ffn_out",
    )(hid, w, x, gpost.reshape(1, d), gn.reshape(1, d))
    return (outs[0], outs[1]) if with_next else (outs[0], None)


def _rope_tables(ang):
    cos = jnp.cos(ang)
    sin = jnp.sin(ang)
    return jnp.concatenate([cos, cos], axis=-1), jnp.concatenate([-sin, sin], axis=-1)


def _angles_1d(s):
    t = jnp.arange(s, dtype=F32)
    inv = ROPE_THETA ** (-jnp.arange(0, HEAD_DIM, 2, dtype=F32) / HEAD_DIM)
    return t[:, None] * inv[None, :]


def _angles_axial(s):
    rows = s // GRID_W
    row = jnp.repeat(jnp.arange(rows, dtype=F32), GRID_W)
    col = jnp.tile(jnp.arange(GRID_W, dtype=F32), rows)
    half = HEAD_DIM // 2
    inv = ROPE_THETA ** (-jnp.arange(0, half, 2, dtype=F32) / half)
    return jnp.concatenate([row[:, None] * inv[None, :], col[:, None] * inv[None, :]], axis=-1)


def kernel(x_prompt, x_sample, norm_mix_pre, norm_mix_post, norm_ffn_pre, norm_ffn_post, w_in, q_norm_a, k_norm_a, sink_b, conv_w, conv_b, gate_r_w, gate_r_b, gate_i_w, gate_i_b, lru_lambda, w_branch, w_out, w_ffn_in, w_ffn_out):
    nb_p, seq, d = x_prompt.shape
    nb_s, seq_s, _ = x_sample.shape
    assert seq == seq_s
    nseq = nb_p + nb_s
    depth = w_in.shape[0]
    x = (x_prompt.reshape(nb_p * seq, d), x_sample.reshape(nb_s * seq, d))

    cos_ax, sin_ax = _rope_tables(_angles_axial(seq))
    cos_1d, sin_1d = _rope_tables(_angles_1d(seq))

    qa0 = 0
    ka0 = qa0 + A_HEADS * HEAD_DIM
    va0 = ka0 + A_KV * HEAD_DIM
    qb0 = va0 + A_KV * HEAD_DIM
    kb0 = qb0 + B_HEADS * HEAD_DIM
    vb0 = kb0 + B_KV * HEAD_DIM
    xc0 = vb0 + B_KV * HEAD_DIM
    g0 = xc0 + 2 * C_WIDTH
    cols = lambda w, a, b: w[:, a:b]

    h = _prenorm(x[0], x[1], norm_mix_pre[0])
    for l in range(depth):
        wl = w_in[l]
        w_qka = jnp.concatenate([cols(wl, qa0, ka0), cols(wl, ka0, va0)], axis=1).astype(BF16)
        w_qkb = jnp.concatenate([cols(wl, qb0, kb0), cols(wl, kb0, vb0)], axis=1).astype(BF16)
        w_v = jnp.concatenate([cols(wl, va0, qb0), cols(wl, vb0, xc0)], axis=1).astype(BF16)
        w_xy = cols(wl, xc0, g0).astype(BF16)
        w_g = wl[:, g0:].astype(BF16)
        gains_a = jnp.concatenate([jnp.tile(q_norm_a[l] * ATTN_QSCALE, A_HEADS),
                                   jnp.tile(k_norm_a[l], A_KV)]).reshape(1, -1)
        gains_b = jnp.concatenate([jnp.full((B_HEADS * HEAD_DIM,), ATTN_QSCALE, F32),
                                   jnp.ones((B_KV * HEAD_DIM,), F32)]).reshape(1, -1)

        qk_a = _proj_rope(h, w_qka, gains_a, cos_ax, sin_ax, seq, normed=True)
        qk_b = _proj_rope(h, w_qkb, gains_b, cos_1d, sin_1d, seq, normed=False)
        vt_a, vt_b = _proj_v(h, w_v, _attn_key_tile(seq))
        xy = _proj(h, w_xy, None, F32, "proj_xy")
        gates = _proj(h, w_g, "sigmoid", F32, "proj_gates")

        oa = _attn_global(qk_a, vt_a, nseq, seq)
        ob = _attn_window(qk_b, vt_b, sink_b[l], nseq, seq)
        oc = _rglru(xy, conv_w[l], conv_b[l], (0.5 * gate_r_w[l]).astype(BF16), 0.5 * gate_r_b[l],
                    (0.5 * gate_i_w[l]).astype(BF16), 0.5 * gate_i_b[l], lru_lambda[l], nseq, seq)

        merged = _merge(oa, ob, oc, gates, w_branch[l].astype(BF16))
        x, h2 = _out_proj(merged, w_out[l].astype(BF16), x, norm_mix_post[l], norm_ffn_pre[l])
        hid = _proj(h2, w_ffn_in[l].astype(BF16), "relu2", BF16, "ffn_in")
        w2 = w_ffn_out[l].astype(BF16)
        if l + 1 < depth:
            x, h = _ffn_out(hid, w2, x, norm_ffn_post[l], norm_mix_pre[l + 1])

    rows_p = nb_p * seq
    y_p, _ = _ffn_out(hid, w2, x, norm_ffn_post[depth - 1], None, 0, rows_p)
    y_s, _ = _ffn_out(hid, w2, x, norm_ffn_post[depth - 1], None, rows_p, nb_s * seq)
    return (y_p.reshape(nb_p, seq, d), y_s.reshape(nb_s, seq, d))
```

```python
import functools
import math

import jax
import jax.numpy as jnp
from jax import lax
from jax.experimental import pallas as pl
from jax.experimental.pallas import tpu as pltpu

F32 = jnp.float32
BF16 = jnp.bfloat16

HEAD_DIM = 128
A_HEADS = 8
A_KV = 2
B_HEADS = 8
B_KV = 2
C_WIDTH = 1024
C_BLOCKS = 8
CONV_W = 4
LRU_C = 8.0
MIX_W = 1024
N_BRANCH = 3
WINDOW = 128
GRID_W = 64
ROPE_THETA = 10000.0
EPS = 1e-6
NEG_BIG = -1e30
ATTN_QSCALE = HEAD_DIM ** -0.5 * math.log2(math.e)
ATTN_Q_TILES_PER_STEP = 4

VMEM_LIMIT_BYTES = 56 * 1024 * 1024


def _params(*sem):
    return pltpu.CompilerParams(dimension_semantics=sem, vmem_limit_bytes=VMEM_LIMIT_BYTES)


def _tile(n, pref):
    t = min(n, pref)
    assert n % t == 0, (n, t)
    return t


def _rms(x, g):
    return x * lax.rsqrt(jnp.mean(x * x, axis=-1, keepdims=True) + EPS) * g


def _rope(x, cos, sin_signed):
    return x * cos + pltpu.roll(x, HEAD_DIM // 2, axis=1) * sin_signed


def _two_part_rows(xa, xb, tm):
    d = xa.shape[1]
    na = xa.shape[0] // tm
    spec_a = pl.BlockSpec((tm, d), lambda i: (jnp.minimum(i, na - 1), 0))
    spec_b = pl.BlockSpec((tm, d), lambda i: (jnp.maximum(i - na, 0), 0))
    return na, spec_a, spec_b


def _prenorm_kernel(xa_ref, xb_ref, g_ref, h_ref, *, na):
    x = jnp.where(pl.program_id(0) < na, xa_ref[...], xb_ref[...])
    h_ref[...] = _rms(x, g_ref[...]).astype(BF16)


def _prenorm(xa, xb, g):
    d = xa.shape[1]
    t = xa.shape[0] + xb.shape[0]
    tm = _tile(math.gcd(xa.shape[0], xb.shape[0]), 512)
    na, spec_a, spec_b = _two_part_rows(xa, xb, tm)
    return pl.pallas_call(
        functools.partial(_prenorm_kernel, na=na),
        grid=(t // tm,),
        in_specs=[spec_a, spec_b, pl.BlockSpec((1, d), lambda i: (0, 0))],
        out_specs=pl.BlockSpec((tm, d), lambda i: (i, 0)),
        out_shape=jax.ShapeDtypeStruct((t, d), BF16),
        compiler_params=_params("parallel"),
        name="prenorm",
    )(xa, xb, g.reshape(1, d))


def _proj_rope_kernel(h_ref, w_ref, g_ref, cos_ref, sin_ref, o_ref, *, normed, nsplit):
    rows_per = h_ref.shape[0] // nsplit
    for r in range(nsplit):
        rows = slice(r * rows_per, (r + 1) * rows_per)
        acc = jnp.dot(h_ref[rows, :], w_ref[...], preferred_element_type=F32)
        cos = cos_ref[rows, :]
        sin = sin_ref[rows, :]
        for hd in range(acc.shape[1] // HEAD_DIM):
            sl = slice(hd * HEAD_DIM, (hd + 1) * HEAD_DIM)
            xh = acc[:, sl]
            xh = _rms(xh, g_ref[:, sl]) if normed else xh * g_ref[:, sl]
            o_ref[rows, sl] = _rope(xh, cos, sin).astype(BF16)


def _proj_rope(h, w, gains, cos, sin, seq, normed):
    t, d = h.shape
    n = w.shape[1]
    tm = _tile(seq, 1024)
    nsb = seq // tm
    return pl.pallas_call(
        functools.partial(_proj_rope_kernel, normed=normed, nsplit=tm // 128 if tm % 512 == 0 else 1),
        grid=(t // tm,),
        in_specs=[pl.BlockSpec((tm, d), lambda i: (i, 0)),
                  pl.BlockSpec((d, n), lambda i: (0, 0)),
                  pl.BlockSpec((1, n), lambda i: (0, 0)),
                  pl.BlockSpec((tm, HEAD_DIM), lambda i: (i % nsb, 0)),
                  pl.BlockSpec((tm, HEAD_DIM), lambda i: (i % nsb, 0))],
        out_specs=pl.BlockSpec((tm, n), lambda i: (i, 0)),
        out_shape=jax.ShapeDtypeStruct((t, n), BF16),
        compiler_params=_params("parallel"),
        name="proj_rope_normed" if normed else "proj_rope",
    )(h, w, gains, cos, sin)


def _proj_kernel(h_ref, w_ref, o_ref, *, act):
    acc = jnp.dot(h_ref[...], w_ref[...], preferred_element_type=F32)
    if act == "sigmoid":
        acc = 0.5 * jnp.tanh(0.5 * acc) + 0.5
    elif act == "relu2":
        acc = jnp.square(jnp.maximum(acc, 0.0))
    o_ref[...] = acc.astype(o_ref.dtype)


def _proj(h, w, act, out_dtype, name):
    t, d = h.shape
    n = w.shape[1]
    tm = _tile(t, 1024)
    tn = _tile(n, 2048)
    return pl.pallas_call(
        functools.partial(_proj_kernel, act=act),
        grid=(n // tn, t // tm),
        in_specs=[pl.BlockSpec((tm, d), lambda j, i: (i, 0)),
                  pl.BlockSpec((d, tn), lambda j, i: (0, j))],
        out_specs=pl.BlockSpec((tm, tn), lambda j, i: (i, j)),
        out_shape=jax.ShapeDtypeStruct((t, n), out_dtype),
        compiler_params=_params("parallel", "parallel"),
        name=name,
    )(h, w)


VT_ONES = 16
VT_ROWS = HEAD_DIM + VT_ONES


WIN_KEY_TILE = HEAD_DIM


def _proj_v_kernel(h_ref, w_ref, vta_ref, vtb_ref):
    acc = jnp.dot(h_ref[...], w_ref[...], preferred_element_type=F32)
    rows = acc.shape[0]
    for hh in range(A_KV):
        vta_ref[hh * VT_ROWS:hh * VT_ROWS + HEAD_DIM, :] = acc[:, hh * HEAD_DIM:(hh + 1) * HEAD_DIM].T.astype(BF16)
        vta_ref[hh * VT_ROWS + HEAD_DIM:(hh + 1) * VT_ROWS, :] = jnp.ones((VT_ONES, rows), BF16)
    for hh in range(B_KV):
        vt = acc[:, (A_KV + hh) * HEAD_DIM:(A_KV + hh + 1) * HEAD_DIM].T.astype(BF16)
        for j in range(rows // WIN_KEY_TILE):
            vtb_ref[j, hh * VT_ROWS:hh * VT_ROWS + HEAD_DIM, :] = vt[:, j * WIN_KEY_TILE:(j + 1) * WIN_KEY_TILE]
            vtb_ref[j, hh * VT_ROWS + HEAD_DIM:(hh + 1) * VT_ROWS, :] = jnp.ones((VT_ONES, WIN_KEY_TILE), BF16)


def _proj_v(h, w, tk):
    t, d = h.shape
    n = w.shape[1]
    per_step = tk // WIN_KEY_TILE
    return pl.pallas_call(
        _proj_v_kernel,
        grid=(t // tk,),
        in_specs=[pl.BlockSpec((tk, d), lambda i: (i, 0)),
                  pl.BlockSpec((d, n), lambda i: (0, 0))],
        out_specs=[pl.BlockSpec((None, A_KV * VT_ROWS, tk), lambda i: (i, 0, 0)),
                   pl.BlockSpec((per_step, B_KV * VT_ROWS, WIN_KEY_TILE), lambda i: (i, 0, 0))],
        out_shape=[jax.ShapeDtypeStruct((t // tk, A_KV * VT_ROWS, tk), BF16),
                   jax.ShapeDtypeStruct((t // WIN_KEY_TILE, B_KV * VT_ROWS, WIN_KEY_TILE), BF16)],
        compiler_params=_params("parallel"),
        name="proj_v",
    )(h, w)


def _attn_global_kernel(q_ref, k_ref, vt_ref, o_ref, m_ref, acc_ref, s0_ref, s1_ref, *, tq, tk, n_rep):
    seq = k_ref.shape[0]
    nk = seq // tk
    ntile = q_ref.shape[0] // tq

    def stacked_q(t):
        return jnp.concatenate([q_ref[t * tq:(t + 1) * tq, g * HEAD_DIM:(g + 1) * HEAD_DIM]
                                for g in range(n_rep)], axis=0)

    def scores(c, q, s_ref):
        off = pl.multiple_of(c * tk, tk)
        s_ref[...] = lax.dot_general(k_ref[pl.ds(off, tk), :], q, (((1,), (1,)), ((), ())),
                                     preferred_element_type=F32)

    def update(c, s_ref):
        s = s_ref[...]
        m_prev = m_ref[...]
        m_new = jnp.maximum(m_prev, jnp.max(s, axis=0, keepdims=True))
        alpha = jnp.exp2(m_prev - m_new)
        p = jnp.exp2(s - jnp.concatenate([m_new] * (tk // 8), axis=0))
        acc_ref[...] = (jnp.concatenate([alpha] * (VT_ROWS // 8), axis=0) * acc_ref[...]
                        + jnp.dot(vt_ref[c], p.astype(BF16), preferred_element_type=F32))
        m_ref[...] = m_new

    q = stacked_q(0)
    scores(0, q, s0_ref)
    for t in range(ntile):
        q_next = stacked_q(t + 1) if t + 1 < ntile else q
        wrap_to = 0 if t + 1 < ntile else nk - 1
        m_ref[...] = jnp.full(m_ref.shape, NEG_BIG, F32)
        acc_ref[...] = jnp.zeros(acc_ref.shape, F32)

        def pair(i, carry, q=q, q_next=q_next, wrap_to=wrap_to):
            c = 2 * i
            scores(c + 1, q, s1_ref)
            update(c, s0_ref)
            wrap = c + 2 >= nk
            scores(jnp.where(wrap, wrap_to, c + 2), jnp.where(wrap, q_next, q), s0_ref)
            update(c + 1, s1_ref)
            return carry

        lax.fori_loop(0, nk // 2, pair, 0, unroll=min(4, nk // 2))
        acc = acc_ref[...]
        denom = jnp.concatenate([acc[HEAD_DIM:HEAD_DIM + 8]] * (HEAD_DIM // 8), axis=0)
        o = (acc[:HEAD_DIM] / denom).T
        for g in range(n_rep):
            o_ref[t * tq:(t + 1) * tq, g * HEAD_DIM:(g + 1) * HEAD_DIM] = o[g * tq:(g + 1) * tq].astype(BF16)
        q = q_next


def _attn_key_tile(seq):
    tk = _tile(seq // 2, 512)
    assert (seq // tk) % 2 == 0
    return tk


def _attn_global(qk, vt, nseq, seq):
    t = qk.shape[0]
    n_rep = A_HEADS // A_KV
    tq = _tile(seq, 256)
    tk = _attn_key_tile(seq)
    tb = _tile(seq, ATTN_Q_TILES_PER_STEP * tq)
    nqb = seq // tb
    gw = n_rep * HEAD_DIM
    return pl.pallas_call(
        functools.partial(_attn_global_kernel, tq=tq, tk=tk, n_rep=n_rep),
        grid=(nseq, A_KV, nqb),
        in_specs=[pl.BlockSpec((tb, gw), lambda b, h, i: (b * nqb + i, h)),
                  pl.BlockSpec((seq, HEAD_DIM), lambda b, h, i: (b, A_HEADS + h)),
                  pl.BlockSpec((seq // tk, VT_ROWS, tk), lambda b, h, i: (b, h, 0))],
        out_specs=pl.BlockSpec((tb, gw), lambda b, h, i: (b * nqb + i, h)),
        out_shape=jax.ShapeDtypeStruct((t, A_HEADS * HEAD_DIM), BF16),
        scratch_shapes=[pltpu.VMEM((8, n_rep * tq), F32),
                        pltpu.VMEM((VT_ROWS, n_rep * tq), F32),
                        pltpu.VMEM((tk, n_rep * tq), F32),
                        pltpu.VMEM((tk, n_rep * tq), F32)],
        compiler_params=_params("parallel", "parallel", "parallel"),
        name="attn_global",
    )(qk, qk, vt)


def _attn_window_kernel(sink_ref, q_ref, k_ref, vt_ref, o_ref, *, tq, nsub, n_rep):
    seq = k_ref.shape[0]
    wk = min(seq, tq + 2 * WINDOW)
    h = pl.program_id(1)
    i = pl.program_id(2)
    sink = jnp.concatenate([jnp.full((8, tq), sink_ref[h * n_rep + g] * math.log2(math.e), F32)
                            for g in range(n_rep)], axis=1)
    key = lax.broadcasted_iota(jnp.int32, (wk, tq), 0)
    qry = lax.broadcasted_iota(jnp.int32, (wk, tq), 1)
    for j in range(nsub):
        q0 = (i * nsub + j) * tq
        start = pl.multiple_of(jnp.clip(q0 - WINDOW, 0, seq - wk), WIN_KEY_TILE)
        k = k_ref[pl.ds(start, wk), :]
        kt0 = start // WIN_KEY_TILE
        vt = jnp.concatenate([vt_ref[kt0 + u] for u in range(wk // WIN_KEY_TILE)], axis=1)
        band = jnp.where(jnp.abs(qry - key + (q0 - start)) <= WINDOW, 0.0, NEG_BIG)
        q = jnp.concatenate([q_ref[j * tq:(j + 1) * tq, g * HEAD_DIM:(g + 1) * HEAD_DIM]
                             for g in range(n_rep)], axis=0)
        s = lax.dot_general(k, q, (((1,), (1,)), ((), ())), preferred_element_type=F32)
        s = s + jnp.concatenate([band] * n_rep, axis=1)
        m = jnp.maximum(jnp.max(s, axis=0, keepdims=True), sink)
        p = jnp.exp2(s - jnp.concatenate([m] * (wk // 8), axis=0))
        acc = jnp.dot(vt, p.astype(BF16), preferred_element_type=F32)
        denom = acc[HEAD_DIM:HEAD_DIM + 8] + jnp.exp2(sink - m)
        o = (acc[:HEAD_DIM] / jnp.concatenate([denom] * (HEAD_DIM // 8), axis=0)).T
        for g in range(n_rep):
            o_ref[j * tq:(j + 1) * tq, g * HEAD_DIM:(g + 1) * HEAD_DIM] = o[g * tq:(g + 1) * tq].astype(BF16)


def _attn_window(qk, vt, sink, nseq, seq):
    t = qk.shape[0]
    n_rep = B_HEADS // B_KV
    tq = _tile(seq, 256)
    nsub = max(n for n in (1, 2, 4, 8) if seq % (n * tq) == 0)
    tb = nsub * tq
    nqb = seq // tb
    gw = n_rep * HEAD_DIM
    return pl.pallas_call(
        functools.partial(_attn_window_kernel, tq=tq, nsub=nsub, n_rep=n_rep),
        grid=(nseq, B_KV, nqb),
        in_specs=[pl.BlockSpec(memory_space=pltpu.SMEM),
                  pl.BlockSpec((tb, gw), lambda b, h, i: (b * nqb + i, h)),
                  pl.BlockSpec((seq, HEAD_DIM), lambda b, h, i: (b, B_HEADS + h)),
                  pl.BlockSpec((seq // WIN_KEY_TILE, VT_ROWS, WIN_KEY_TILE), lambda b, h, i: (b, h, 0))],
        out_specs=pl.BlockSpec((tb, gw), lambda b, h, i: (b * nqb + i, h)),
        out_shape=jax.ShapeDtypeStruct((t, B_HEADS * HEAD_DIM), BF16),
        compiler_params=_params("parallel", "parallel", "parallel"),
        name="attn_window",
    )(sink, qk, qk, vt)


N_SEG = 8
LIN_PAD = 8


def _segment_plan(seq):
    m = -(-seq // (4 * N_SEG))
    m += 1 - m % 2
    while True:
        for dv in range(1, m + 1):
            if m % dv == 0 and (16 <= 4 * dv <= 64 or (dv == m and 4 * dv <= 96)):
                assert (N_SEG - 1) * 4 * m < seq <= N_SEG * 4 * m
                return 4 * m, 4 * dv
        m += 2


def _rglru_kernel(x_ref, y_ref, cw_ref, cb_ref, wr_ref, br_ref, wi_ref, bi_ref, lam_ref, o_ref,
                  lin_ref, hf_ref, pf_ref, hb_ref, pb_ref, *, pitch, chunk, lchunk):
    seq, bw = x_ref.shape
    nchunk = pitch // chunk
    rows = chunk * N_SEG
    left = CONV_W // 2
    last_valid = seq - (N_SEG - 1) * pitch
    lin_rows = lin_ref.shape[0]

    lin_ref[0:LIN_PAD, :] = jnp.zeros((LIN_PAD, bw), F32)
    lin_ref[LIN_PAD + seq:lin_rows, :] = jnp.zeros((lin_rows - LIN_PAD - seq, bw), F32)

    def copy_in(c, carry):
        t0 = pl.multiple_of(c * lchunk, lchunk)
        lin_ref[pl.ds(LIN_PAD + t0, lchunk), :] = x_ref[pl.ds(t0, lchunk), :]
        return carry

    lax.fori_loop(0, seq // lchunk, copy_in, 0)

    lam = lam_ref[...]
    log_sig = jnp.minimum(lam, 0.0) - jnp.log1p(jnp.exp(-jnp.abs(lam)))

    def conv(g0):
        u = cb_ref[...]
        for j in range(CONV_W):
            xt = jnp.concatenate(
                [lin_ref[pl.ds(LIN_PAD + g0 + g + j - left, N_SEG, stride=pitch), :] for g in range(chunk)],
                axis=0)
            u = u + xt * cw_ref[j:j + 1, :]
        return u

    half_c_log_sig = (0.5 * LRU_C) * log_sig

    def recurrence_terms(u, d):
        ub = u.astype(BF16)
        tr = jnp.tanh(jnp.dot(ub, wr_ref[d], preferred_element_type=F32) + br_ref[d:d + 1, :])
        ti = jnp.tanh(jnp.dot(ub, wi_ref[d], preferred_element_type=F32) + bi_ref[d:d + 1, :])
        k = half_c_log_sig[d:d + 1, :]
        log_a = tr * k + k
        a = jnp.exp(log_a)
        th = jnp.tanh(log_a)
        z = -2.0 * th / (1.0 - th)
        uh = 0.5 * u
        drive = z * lax.rsqrt(jnp.maximum(z, 1e-30)) * (ti * uh + uh)
        return a, drive

    def scan(a, b, h, p, reverse):
        hs = [None] * chunk
        ps = [None] * chunk
        for g in (range(chunk - 1, -1, -1) if reverse else range(chunk)):
            ag = a[g * N_SEG:(g + 1) * N_SEG]
            h = ag * h + b[g * N_SEG:(g + 1) * N_SEG]
            p = ag * p
            hs[g] = h
            ps[g] = p
        return jnp.concatenate(hs, axis=0), jnp.concatenate(ps, axis=0), h, p

    row = lax.broadcasted_iota(jnp.int32, (rows, bw), 0)
    in_last_segment = (row & (N_SEG - 1)) == N_SEG - 1
    group = row >> 3

    def step(c, carry):
        hf, pf, hb, pb = carry
        g0 = c * chunk
        a, drive = recurrence_terms(conv(g0), 0)
        hh, pp, hf, pf = scan(a, drive, hf, pf, reverse=False)
        r0 = pl.multiple_of(g0 * N_SEG, N_SEG)
        hf_ref[pl.ds(r0, rows), :] = hh
        pf_ref[pl.ds(r0, rows), :] = pp

        g1 = (nchunk - 1 - c) * chunk
        a, drive = recurrence_terms(conv(g1), 1)
        drive = jnp.where(in_last_segment & (group >= last_valid - g1), 0.0, drive)
        hh, pp, hb, pb = scan(a, drive, hb, pb, reverse=True)
        r1 = pl.multiple_of(g1 * N_SEG, N_SEG)
        hb_ref[pl.ds(r1, rows), :] = hh
        pb_ref[pl.ds(r1, rows), :] = pp
        return hf, pf, hb, pb

    zero = jnp.zeros((N_SEG, bw), F32)
    one = jnp.ones((N_SEG, bw), F32)
    hf, pf, hb, pb = lax.fori_loop(0, nchunk, step, (zero, one, zero, one), unroll=2)

    cin = [zero[0:1]]
    for r in range(1, N_SEG):
        cin.append(hf[r - 1:r] + pf[r - 1:r] * cin[r - 1])
    cinb = [zero[0:1]]
    for r in range(N_SEG - 2, -1, -1):
        cinb.append(hb[r + 1:r + 2] + pb[r + 1:r + 2] * cinb[-1])
    cin_f = jnp.concatenate([jnp.concatenate(cin, axis=0)] * chunk, axis=0)
    cin_b = jnp.concatenate([jnp.concatenate(cinb[::-1], axis=0)] * chunk, axis=0)

    def combine(c, carry):
        g0 = c * chunk
        r0 = pl.multiple_of(g0 * N_SEG, N_SEG)
        hsum = ((hf_ref[pl.ds(r0, rows), :] + pf_ref[pl.ds(r0, rows), :] * cin_f)
                + (hb_ref[pl.ds(r0, rows), :] + pb_ref[pl.ds(r0, rows), :] * cin_b))
        for g in range(chunk):
            lin_ref[pl.ds(LIN_PAD + g0 + g, N_SEG, stride=pitch), :] = hsum[g * N_SEG:(g + 1) * N_SEG]
        return carry

    lax.fori_loop(0, nchunk, combine, 0)

    def gate_out(c, carry):
        t0 = pl.multiple_of(c * lchunk, lchunk)
        gate = jax.nn.gelu(y_ref[pl.ds(t0, lchunk), :], approximate=True)
        o_ref[pl.ds(t0, lchunk), :] = (lin_ref[pl.ds(LIN_PAD + t0, lchunk), :] * gate).astype(BF16)
        return carry

    lax.fori_loop(0, seq // lchunk, gate_out, 0)


def _rglru(xy, conv_w, conv_b, wr, br, wi, bi, lam, nseq, seq):
    t = xy.shape[0]
    bw = C_WIDTH // C_BLOCKS
    pitch, chunk = _segment_plan(seq)
    lchunk = _tile(seq, 512)
    vec = lambda rows: pl.BlockSpec((rows, bw), lambda b, n: (0, n))
    mat = pl.BlockSpec((2, None, bw, bw), lambda b, n: (0, n, 0, 0))
    seg_major = pltpu.VMEM((pitch * N_SEG, bw), F32)
    return pl.pallas_call(
        functools.partial(_rglru_kernel, pitch=pitch, chunk=chunk, lchunk=lchunk),
        grid=(nseq, C_BLOCKS),
        in_specs=[pl.BlockSpec((seq, bw), lambda b, n: (b, n)),
                  pl.BlockSpec((seq, bw), lambda b, n: (b, C_BLOCKS + n)),
                  vec(CONV_W), vec(1), mat, vec(2), mat, vec(2), vec(2)],
        out_specs=pl.BlockSpec((seq, bw), lambda b, n: (b, n)),
        out_shape=jax.ShapeDtypeStruct((t, C_WIDTH), BF16),
        scratch_shapes=[pltpu.VMEM((LIN_PAD + pitch * N_SEG + 8, bw), F32),
                        seg_major, seg_major, seg_major, seg_major],
        compiler_params=_params("parallel", "parallel"),
        name="rglru",
    )(xy, xy, conv_w, conv_b.reshape(1, C_WIDTH), wr, br, wi, bi, lam)


def _merge_kernel(oa_ref, ob_ref, oc_ref, g0_ref, g1_ref, g2_ref, wb_ref, o_ref):
    acc = g0_ref[...] * jnp.dot(oa_ref[...], wb_ref[0], preferred_element_type=F32)
    acc = acc + g1_ref[...] * jnp.dot(ob_ref[...], wb_ref[1], preferred_element_type=F32)
    acc = acc + g2_ref[...] * jnp.dot(oc_ref[...], wb_ref[2], preferred_element_type=F32)
    o_ref[...] = acc.astype(BF16)


def _merge(oa, ob, oc, gates, wb):
    t, kw = oa.shape
    d = wb.shape[2]
    tm = _tile(t, 256)
    tn = _tile(d, 2048)
    ncb = d // tn
    o_spec = pl.BlockSpec((tm, kw), lambda j, i: (i, 0))
    g_spec = lambda br: pl.BlockSpec((tm, tn), lambda j, i: (i, br * ncb + j))
    return pl.pallas_call(
        _merge_kernel,
        grid=(ncb, t // tm),
        in_specs=[o_spec, o_spec, o_spec, g_spec(0), g_spec(1), g_spec(2),
                  pl.BlockSpec((N_BRANCH, kw, tn), lambda j, i: (0, 0, j))],
        out_specs=pl.BlockSpec((tm, tn), lambda j, i: (i, j)),
        out_shape=jax.ShapeDtypeStruct((t, d), BF16),
        compiler_params=_params("parallel", "parallel"),
        name="merge",
    )(oa, ob, oc, gates, gates, gates, wb)


def _out_kernel(m_ref, w_ref, gpost_ref, gnext_ref, *rest, nsplit, na):
    x_refs, (xo_ref, h_ref) = rest[:-2], rest[-2:]
    rows_per = m_ref.shape[0] // nsplit
    for r in range(nsplit):
        rows = slice(r * rows_per, (r + 1) * rows_per)
        y = jnp.dot(m_ref[rows, :], w_ref[...], preferred_element_type=F32)
        if len(x_refs) == 2:
            x = jnp.where(pl.program_id(0) < na, x_refs[0][rows, :], x_refs[1][rows, :])
        else:
            x = x_refs[0][rows, :]
        xn = x + _rms(y, gpost_ref[...])
        xo_ref[rows, :] = xn
        h_ref[rows, :] = _rms(xn, gnext_ref[...]).astype(BF16)


def _out_proj(merged, w, x, gpost, gnext):
    t, d = merged.shape
    row = lambda tm: pl.BlockSpec((tm, d), lambda i: (i, 0))
    if isinstance(x, tuple):
        tm = _tile(math.gcd(x[0].shape[0], x[1].shape[0]), 512)
        na, spec_a, spec_b = _two_part_rows(x[0], x[1], tm)
        x_args, x_specs = list(x), [spec_a, spec_b]
    else:
        tm = _tile(t, 512)
        na, x_args, x_specs = 0, [x], [row(tm)]
    gain = pl.BlockSpec((1, d), lambda i: (0, 0))
    return pl.pallas_call(
        functools.partial(_out_kernel, nsplit=4 if tm % 512 == 0 else 1, na=na),
        grid=(t // tm,),
        in_specs=[row(tm), pl.BlockSpec((d, d), lambda i: (0, 0)), gain, gain] + x_specs,
        out_specs=[row(tm), row(tm)],
        out_shape=[jax.ShapeDtypeStruct((t, d), F32), jax.ShapeDtypeStruct((t, d), BF16)],
        compiler_params=_params("parallel"),
        name="out_proj",
    )(merged, w, gpost.reshape(1, d), gnext.reshape(1, d), *x_args)


def _ffn_out_kernel(hid_ref, w_ref, x_ref, gpost_ref, gnext_ref, xo_ref, *rest, nsplit):
    h_ref = rest[0] if len(rest) == 2 else None
    xs_ref = rest[-1]
    k = pl.program_id(1)
    nk, _, slab = xs_ref.shape
    d = xo_ref.shape[1]

    xs_ref[k] = x_ref[...]

    @pl.when((k == 0) & (nk > 1))
    def _():
        xo_ref[...] = jnp.dot(hid_ref[...], w_ref[...], preferred_element_type=F32)

    @pl.when((k > 0) & (k < nk - 1))
    def _():
        xo_ref[...] += jnp.dot(hid_ref[...], w_ref[...], preferred_element_type=F32)

    @pl.when(k == nk - 1)
    def _():
        slabs = [slice(j * slab, (j + 1) * slab) for j in range(nk)]
        rows_per = xo_ref.shape[0] // nsplit
        for r in range(nsplit):
            rows = slice(r * rows_per, (r + 1) * rows_per)
            part = jnp.dot(hid_ref[rows, :], w_ref[...], preferred_element_type=F32)
            y = part if nk == 1 else xo_ref[rows, :] + part
            rs = lax.rsqrt(jnp.sum(jnp.square(y), axis=-1, keepdims=True) * (1.0 / d) + EPS)
            ssq = 0.0
            for j, sl in enumerate(slabs):
                xn = xs_ref[j, rows, :] + y[:, sl] * rs * gpost_ref[:, sl]
                xo_ref[rows, sl] = xn
                ssq = ssq + jnp.sum(jnp.square(xn), axis=-1, keepdims=True)
            if h_ref is not None:
                rs = lax.rsqrt(ssq * (1.0 / d) + EPS)
                for sl in slabs:
                    h_ref[rows, sl] = (xo_ref[rows, sl] * rs * gnext_ref[:, sl]).astype(BF16)


def _ffn_out(hid, w, x, gpost, gnext, row_start=0, nrows=None):
    d = x.shape[1]
    t = x.shape[0] if nrows is None else nrows
    ff = hid.shape[1]
    tm = _tile(math.gcd(row_start, t), 1024)
    tk = _tile(ff, 1024)
    nk = ff // tk
    slab = d // nk
    assert slab % HEAD_DIM == 0
    i0 = row_start // tm
    row_in = pl.BlockSpec((tm, slab), lambda i, k: (i0 + i, k))
    row = pl.BlockSpec((tm, d), lambda i, k: (i, 0))
    gain = pl.BlockSpec((1, d), lambda i, k: (0, 0))
    with_next = gnext is not None
    out_specs = [row, row] if with_next else [row]
    out_shape = [jax.ShapeDtypeStruct((t, d), F32)] + ([jax.ShapeDtypeStruct((t, d), BF16)] if with_next else [])
    gn = gnext if with_next else gpost
    outs = pl.pallas_call(
        functools.partial(_ffn_out_kernel, nsplit=4 if tm % 512 == 0 else 1),
        grid=(t // tm, ff // tk),
        in_specs=[pl.BlockSpec((tm, tk), lambda i, k: (i0 + i, k)),
                  pl.BlockSpec((tk, d), lambda i, k: (k, 0)),
                  row_in, gain, gain],
        out_specs=out_specs,
        out_shape=out_shape,
        scratch_shapes=[pltpu.VMEM((nk, tm, slab), F32)],
        compiler_params=_params("parallel", "arbitrary"),
        name="ffn_out",
    )(hid, w, x, gpost.reshape(1, d), gn.reshape(1, d))
    return (outs[0], outs[1]) if with_next else (outs[0], None)


def _rope_tables(ang):
    cos = jnp.cos(ang)
    sin = jnp.sin(ang)
    return jnp.concatenate([cos, cos], axis=-1), jnp.concatenate([-sin, sin], axis=-1)


def _angles_1d(s):
    t = jnp.arange(s, dtype=F32)
    inv = ROPE_THETA ** (-jnp.arange(0, HEAD_DIM, 2, dtype=F32) / HEAD_DIM)
    return t[:, None] * inv[None, :]


def _angles_axial(s):
    rows = s // GRID_W
    row = jnp.repeat(jnp.arange(rows, dtype=F32), GRID_W)
    col = jnp.tile(jnp.arange(GRID_W, dtype=F32), rows)
    half = HEAD_DIM // 2
    inv = ROPE_THETA ** (-jnp.arange(0, half, 2, dtype=F32) / half)
    return jnp.concatenate([row[:, None] * inv[None, :], col[:, None] * inv[None, :]], axis=-1)


def kernel(x_prompt, x_sample, norm_mix_pre, norm_mix_post, norm_ffn_pre, norm_ffn_post, w_in, q_norm_a, k_norm_a, sink_b, conv_w, conv_b, gate_r_w, gate_r_b, gate_i_w, gate_i_b, lru_lambda, w_branch, w_out, w_ffn_in, w_ffn_out):
    nb_p, seq, d = x_prompt.shape
    nb_s, seq_s, _ = x_sample.shape
    assert seq == seq_s
    nseq = nb_p + nb_s
    depth = w_in.shape[0]
    x = (x_prompt.reshape(nb_p * seq, d), x_sample.reshape(nb_s * seq, d))

    cos_ax, sin_ax = _rope_tables(_angles_axial(seq))
    cos_1d, sin_1d = _rope_tables(_angles_1d(seq))

    qa0 = 0
    ka0 = qa0 + A_HEADS * HEAD_DIM
    va0 = ka0 + A_KV * HEAD_DIM
    qb0 = va0 + A_KV * HEAD_DIM
    kb0 = qb0 + B_HEADS * HEAD_DIM
    vb0 = kb0 + B_KV * HEAD_DIM
    xc0 = vb0 + B_KV * HEAD_DIM
    g0 = xc0 + 2 * C_WIDTH
    cols = lambda w, a, b: w[:, a:b]

    h = _prenorm(x[0], x[1], norm_mix_pre[0])
    for l in range(depth):
        wl = w_in[l]
        w_qka = jnp.concatenate([cols(wl, qa0, ka0), cols(wl, ka0, va0)], axis=1).astype(BF16)
        w_qkb = jnp.concatenate([cols(wl, qb0, kb0), cols(wl, kb0, vb0)], axis=1).astype(BF16)
        w_v = jnp.concatenate([cols(wl, va0, qb0), cols(wl, vb0, xc0)], axis=1).astype(BF16)
        w_xy = cols(wl, xc0, g0).astype(BF16)
        w_g = wl[:, g0:].astype(BF16)
        gains_a = jnp.concatenate([jnp.tile(q_norm_a[l] * ATTN_QSCALE, A_HEADS),
                                   jnp.tile(k_norm_a[l], A_KV)]).reshape(1, -1)
        gains_b = jnp.concatenate([jnp.full((B_HEADS * HEAD_DIM,), ATTN_QSCALE, F32),
                                   jnp.ones((B_KV * HEAD_DIM,), F32)]).reshape(1, -1)

        qk_a = _proj_rope(h, w_qka, gains_a, cos_ax, sin_ax, seq, normed=True)
        qk_b = _proj_rope(h, w_qkb, gains_b, cos_1d, sin_1d, seq, normed=False)
        vt_a, vt_b = _proj_v(h, w_v, _attn_key_tile(seq))
        xy = _proj(h, w_xy, None, F32, "proj_xy")
        gates = _proj(h, w_g, "sigmoid", F32, "proj_gates")

        oa = _attn_global(qk_a, vt_a, nseq, seq)
        ob = _attn_window(qk_b, vt_b, sink_b[l], nseq, seq)
        oc = _rglru(xy, conv_w[l], conv_b[l], (0.5 * gate_r_w[l]).astype(BF16), 0.5 * gate_r_b[l],
                    (0.5 * gate_i_w[l]).astype(BF16), 0.5 * gate_i_b[l], lru_lambda[l], nseq, seq)

        merged = _merge(oa, ob, oc, gates, w_branch[l].astype(BF16))
        x, h2 = _out_proj(merged, w_out[l].astype(BF16), x, norm_mix_post[l], norm_ffn_pre[l])
        hid = _proj(h2, w_ffn_in[l].astype(BF16), "relu2", BF16, "ffn_in")
        w2 = w_ffn_out[l].astype(BF16)
        if l + 1 < depth:
            x, h = _ffn_out(hid, w2, x, norm_ffn_post[l], norm_mix_pre[l + 1])

    rows_p = nb_p * seq
    y_p, _ = _ffn_out(hid, w2, x, norm_ffn_post[depth - 1], None, 0, rows_p)
    y_s, _ = _ffn_out(hid, w2, x, norm_ffn_post[depth - 1], None, rows_p, nb_s * seq)
    return (y_p.reshape(nb_p, seq, d), y_s.reshape(nb_s, seq, d))
```

```python
import functools
import math

import jax
import jax.numpy as jnp
from jax import lax
from jax.experimental import pallas as pl
from jax.experimental.pallas import tpu as pltpu

F32 = jnp.float32
BF16 = jnp.bfloat16

HEAD_DIM = 128
A_HEADS = 8
A_KV = 2
B_HEADS = 8
B_KV = 2
C_WIDTH = 1024
C_BLOCKS = 8
CONV_W = 4
LRU_C = 8.0
MIX_W = 1024
N_BRANCH = 3
WINDOW = 128
GRID_W = 64
ROPE_THETA = 10000.0
EPS = 1e-6
NEG_BIG = -1e30
ATTN_QSCALE = HEAD_DIM ** -0.5 * math.log2(math.e)
ATTN_Q_TILES_PER_STEP = 8

VMEM_LIMIT_BYTES = 56 * 1024 * 1024


def _params(*sem):
    return pltpu.CompilerParams(dimension_semantics=sem, vmem_limit_bytes=VMEM_LIMIT_BYTES)


def _tile(n, pref):
    t = min(n, pref)
    assert n % t == 0, (n, t)
    return t


def _rms(x, g):
    return x * lax.rsqrt(jnp.mean(x * x, axis=-1, keepdims=True) + EPS) * g


def _rope(x, cos, sin_signed):
    return x * cos + pltpu.roll(x, HEAD_DIM // 2, axis=1) * sin_signed


def _two_part_rows(xa, xb, tm):
    d = xa.shape[1]
    na = xa.shape[0] // tm
    spec_a = pl.BlockSpec((tm, d), lambda i: (jnp.minimum(i, na - 1), 0))
    spec_b = pl.BlockSpec((tm, d), lambda i: (jnp.maximum(i - na, 0), 0))
    return na, spec_a, spec_b


def _prenorm_kernel(xa_ref, xb_ref, g_ref, h_ref, *, na):
    x = jnp.where(pl.program_id(0) < na, xa_ref[...], xb_ref[...])
    h_ref[...] = _rms(x, g_ref[...]).astype(BF16)


def _prenorm(xa, xb, g):
    d = xa.shape[1]
    t = xa.shape[0] + xb.shape[0]
    tm = _tile(math.gcd(xa.shape[0], xb.shape[0]), 512)
    na, spec_a, spec_b = _two_part_rows(xa, xb, tm)
    return pl.pallas_call(
        functools.partial(_prenorm_kernel, na=na),
        grid=(t // tm,),
        in_specs=[spec_a, spec_b, pl.BlockSpec((1, d), lambda i: (0, 0))],
        out_specs=pl.BlockSpec((tm, d), lambda i: (i, 0)),
        out_shape=jax.ShapeDtypeStruct((t, d), BF16),
        compiler_params=_params("parallel"),
        name="prenorm",
    )(xa, xb, g.reshape(1, d))


def _proj_rope_kernel(h_ref, w_ref, g_ref, cos_ref, sin_ref, o_ref, *, normed, nsplit):
    rows_per = h_ref.shape[0] // nsplit
    for r in range(nsplit):
        rows = slice(r * rows_per, (r + 1) * rows_per)
        acc = jnp.dot(h_ref[rows, :], w_ref[...], preferred_element_type=F32)
        cos = cos_ref[rows, :]
        sin = sin_ref[rows, :]
        for hd in range(acc.shape[1] // HEAD_DIM):
            sl = slice(hd * HEAD_DIM, (hd + 1) * HEAD_DIM)
            xh = acc[:, sl]
            xh = _rms(xh, g_ref[:, sl]) if normed else xh * g_ref[:, sl]
            o_ref[rows, sl] = _rope(xh, cos, sin).astype(BF16)


def _proj_rope(h, w, gains, cos, sin, seq, normed):
    t, d = h.shape
    n = w.shape[1]
    tm = _tile(seq, 1024)
    nsb = seq // tm
    return pl.pallas_call(
        functools.partial(_proj_rope_kernel, normed=normed, nsplit=tm // 128 if tm % 512 == 0 else 1),
        grid=(t // tm,),
        in_specs=[pl.BlockSpec((tm, d), lambda i: (i, 0)),
                  pl.BlockSpec((d, n), lambda i: (0, 0)),
                  pl.BlockSpec((1, n), lambda i: (0, 0)),
                  pl.BlockSpec((tm, HEAD_DIM), lambda i: (i % nsb, 0)),
                  pl.BlockSpec((tm, HEAD_DIM), lambda i: (i % nsb, 0))],
        out_specs=pl.BlockSpec((tm, n), lambda i: (i, 0)),
        out_shape=jax.ShapeDtypeStruct((t, n), BF16),
        compiler_params=_params("parallel"),
        name="proj_rope_normed" if normed else "proj_rope",
    )(h, w, gains, cos, sin)


def _proj_kernel(h_ref, w_ref, o_ref, *, act):
    acc = jnp.dot(h_ref[...], w_ref[...], preferred_element_type=F32)
    if act == "sigmoid":
        acc = 0.5 * jnp.tanh(0.5 * acc) + 0.5
    elif act == "relu2":
        acc = jnp.square(jnp.maximum(acc, 0.0))
    o_ref[...] = acc.astype(o_ref.dtype)


def _proj(h, w, act, out_dtype, name):
    t, d = h.shape
    n = w.shape[1]
    tm = _tile(t, 1024)
    tn = _tile(n, 2048)
    return pl.pallas_call(
        functools.partial(_proj_kernel, act=act),
        grid=(n // tn, t // tm),
        in_specs=[pl.BlockSpec((tm, d), lambda j, i: (i, 0)),
                  pl.BlockSpec((d, tn), lambda j, i: (0, j))],
        out_specs=pl.BlockSpec((tm, tn), lambda j, i: (i, j)),
        out_shape=jax.ShapeDtypeStruct((t, n), out_dtype),
        compiler_params=_params("parallel", "parallel"),
        name=name,
    )(h, w)


VT_ONES = 16
VT_ROWS = HEAD_DIM + VT_ONES


WIN_KEY_TILE = HEAD_DIM


def _proj_v_kernel(h_ref, w_ref, vta_ref, vtb_ref):
    acc = jnp.dot(h_ref[...], w_ref[...], preferred_element_type=F32)
    rows = acc.shape[0]
    for hh in range(A_KV):
        vta_ref[hh * VT_ROWS:hh * VT_ROWS + HEAD_DIM, :] = acc[:, hh * HEAD_DIM:(hh + 1) * HEAD_DIM].T.astype(BF16)
        vta_ref[hh * VT_ROWS + HEAD_DIM:(hh + 1) * VT_ROWS, :] = jnp.ones((VT_ONES, rows), BF16)
    for hh in range(B_KV):
        vt = acc[:, (A_KV + hh) * HEAD_DIM:(A_KV + hh + 1) * HEAD_DIM].T.astype(BF16)
        for j in range(rows // WIN_KEY_TILE):
            vtb_ref[j, hh * VT_ROWS:hh * VT_ROWS + HEAD_DIM, :] = vt[:, j * WIN_KEY_TILE:(j + 1) * WIN_KEY_TILE]
            vtb_ref[j, hh * VT_ROWS + HEAD_DIM:(hh + 1) * VT_ROWS, :] = jnp.ones((VT_ONES, WIN_KEY_TILE), BF16)


def _proj_v(h, w, tk):
    t, d = h.shape
    n = w.shape[1]
    per_step = tk // WIN_KEY_TILE
    return pl.pallas_call(
        _proj_v_kernel,
        grid=(t // tk,),
        in_specs=[pl.BlockSpec((tk, d), lambda i: (i, 0)),
                  pl.BlockSpec((d, n), lambda i: (0, 0))],
        out_specs=[pl.BlockSpec((None, A_KV * VT_ROWS, tk), lambda i: (i, 0, 0)),
                   pl.BlockSpec((per_step, B_KV * VT_ROWS, WIN_KEY_TILE), lambda i: (i, 0, 0))],
        out_shape=[jax.ShapeDtypeStruct((t // tk, A_KV * VT_ROWS, tk), BF16),
                   jax.ShapeDtypeStruct((t // WIN_KEY_TILE, B_KV * VT_ROWS, WIN_KEY_TILE), BF16)],
        compiler_params=_params("parallel"),
        name="proj_v",
    )(h, w)


def _attn_global_kernel(q_ref, k_ref, vt_ref, o_ref, m_ref, acc_ref, s0_ref, s1_ref, *, tq, tk, n_rep):
    seq = k_ref.shape[0]
    nk = seq // tk
    ntile = q_ref.shape[0] // tq

    def stacked_q(t):
        return jnp.concatenate([q_ref[t * tq:(t + 1) * tq, g * HEAD_DIM:(g + 1) * HEAD_DIM]
                                for g in range(n_rep)], axis=0)

    def scores(c, q, s_ref):
        off = pl.multiple_of(c * tk, tk)
        s_ref[...] = lax.dot_general(k_ref[pl.ds(off, tk), :], q, (((1,), (1,)), ((), ())),
                                     preferred_element_type=F32)

    def update(c, s_ref):
        s = s_ref[...]
        m_prev = m_ref[...]
        m_new = jnp.maximum(m_prev, jnp.max(s, axis=0, keepdims=True))
        alpha = jnp.exp2(m_prev - m_new)
        p = jnp.exp2(s - jnp.concatenate([m_new] * (tk // 8), axis=0))
        acc_ref[...] = (jnp.concatenate([alpha] * (VT_ROWS // 8), axis=0) * acc_ref[...]
                        + jnp.dot(vt_ref[c], p.astype(BF16), preferred_element_type=F32))
        m_ref[...] = m_new

    q = stacked_q(0)
    scores(0, q, s0_ref)
    for t in range(ntile):
        q_next = stacked_q(t + 1) if t + 1 < ntile else q
        wrap_to = 0 if t + 1 < ntile else nk - 1
        m_ref[...] = jnp.full(m_ref.shape, NEG_BIG, F32)
        acc_ref[...] = jnp.zeros(acc_ref.shape, F32)

        def pair(i, carry, q=q, q_next=q_next, wrap_to=wrap_to):
            c = 2 * i
            scores(c + 1, q, s1_ref)
            update(c, s0_ref)
            wrap = c + 2 >= nk
            scores(jnp.where(wrap, wrap_to, c + 2), jnp.where(wrap, q_next, q), s0_ref)
            update(c + 1, s1_ref)
            return carry

        lax.fori_loop(0, nk // 2, pair, 0, unroll=min(4, nk // 2))
        acc = acc_ref[...]
        denom = jnp.concatenate([acc[HEAD_DIM:HEAD_DIM + 8]] * (HEAD_DIM // 8), axis=0)
        o = (acc[:HEAD_DIM] / denom).T
        for g in range(n_rep):
            o_ref[t * tq:(t + 1) * tq, g * HEAD_DIM:(g + 1) * HEAD_DIM] = o[g * tq:(g + 1) * tq].astype(BF16)
        q = q_next


def _attn_key_tile(seq):
    tk = _tile(seq // 2, 512)
    assert (seq // tk) % 2 == 0
    return tk


def _attn_global(qk, vt, nseq, seq):
    t = qk.shape[0]
    n_rep = A_HEADS // A_KV
    tq = _tile(seq, 256)
    tk = _attn_key_tile(seq)
    tb = _tile(seq, ATTN_Q_TILES_PER_STEP * tq)
    nqb = seq // tb
    gw = n_rep * HEAD_DIM
    return pl.pallas_call(
        functools.partial(_attn_global_kernel, tq=tq, tk=tk, n_rep=n_rep),
        grid=(nseq, A_KV, nqb),
        in_specs=[pl.BlockSpec((tb, gw), lambda b, h, i: (b * nqb + i, h)),
                  pl.BlockSpec((seq, HEAD_DIM), lambda b, h, i: (b, A_HEADS + h)),
                  pl.BlockSpec((seq // tk, VT_ROWS, tk), lambda b, h, i: (b, h, 0))],
        out_specs=pl.BlockSpec((tb, gw), lambda b, h, i: (b * nqb + i, h)),
        out_shape=jax.ShapeDtypeStruct((t, A_HEADS * HEAD_DIM), BF16),
        scratch_shapes=[pltpu.VMEM((8, n_rep * tq), F32),
                        pltpu.VMEM((VT_ROWS, n_rep * tq), F32),
                        pltpu.VMEM((tk, n_rep * tq), F32),
                        pltpu.VMEM((tk, n_rep * tq), F32)],
        compiler_params=_params("parallel", "parallel", "parallel"),
        name="attn_global",
    )(qk, qk, vt)


def _attn_window_kernel(sink_ref, q_ref, k_ref, vt_ref, o_ref, *, tq, nsub, n_rep):
    seq = k_ref.shape[0]
    wk = min(seq, tq + 2 * WINDOW)
    h = pl.program_id(1)
    i = pl.program_id(2)
    sink = jnp.concatenate([jnp.full((8, tq), sink_ref[h * n_rep + g] * math.log2(math.e), F32)
                            for g in range(n_rep)], axis=1)
    key = lax.broadcasted_iota(jnp.int32, (wk, tq), 0)
    qry = lax.broadcasted_iota(jnp.int32, (wk, tq), 1)
    for j in range(nsub):
        q0 = (i * nsub + j) * tq
        start = pl.multiple_of(jnp.clip(q0 - WINDOW, 0, seq - wk), WIN_KEY_TILE)
        k = k_ref[pl.ds(start, wk), :]
        kt0 = start // WIN_KEY_TILE
        vt = jnp.concatenate([vt_ref[kt0 + u] for u in range(wk // WIN_KEY_TILE)], axis=1)
        band = jnp.where(jnp.abs(qry - key + (q0 - start)) <= WINDOW, 0.0, NEG_BIG)
        q = jnp.concatenate([q_ref[j * tq:(j + 1) * tq, g * HEAD_DIM:(g + 1) * HEAD_DIM]
                             for g in range(n_rep)], axis=0)
        s = lax.dot_general(k, q, (((1,), (1,)), ((), ())), preferred_element_type=F32)
        s = s + jnp.concatenate([band] * n_rep, axis=1)
        m = jnp.maximum(jnp.max(s, axis=0, keepdims=True), sink)
        p = jnp.exp2(s - jnp.concatenate([m] * (wk // 8), axis=0))
        acc = jnp.dot(vt, p.astype(BF16), preferred_element_type=F32)
        denom = acc[HEAD_DIM:HEAD_DIM + 8] + jnp.exp2(sink - m)
        o = (acc[:HEAD_DIM] / jnp.concatenate([denom] * (HEAD_DIM // 8), axis=0)).T
        for g in range(n_rep):
            o_ref[j * tq:(j + 1) * tq, g * HEAD_DIM:(g + 1) * HEAD_DIM] = o[g * tq:(g + 1) * tq].astype(BF16)


def _attn_window(qk, vt, sink, nseq, seq):
    t = qk.shape[0]
    n_rep = B_HEADS // B_KV
    tq = _tile(seq, 256)
    nsub = max(n for n in (1, 2, 4, 8) if seq % (n * tq) == 0)
    tb = nsub * tq
    nqb = seq // tb
    gw = n_rep * HEAD_DIM
    return pl.pallas_call(
        functools.partial(_attn_window_kernel, tq=tq, nsub=nsub, n_rep=n_rep),
        grid=(nseq, B_KV, nqb),
        in_specs=[pl.BlockSpec(memory_space=pltpu.SMEM),
                  pl.BlockSpec((tb, gw), lambda b, h, i: (b * nqb + i, h)),
                  pl.BlockSpec((seq, HEAD_DIM), lambda b, h, i: (b, B_HEADS + h)),
                  pl.BlockSpec((seq // WIN_KEY_TILE, VT_ROWS, WIN_KEY_TILE), lambda b, h, i: (b, h, 0))],
        out_specs=pl.BlockSpec((tb, gw), lambda b, h, i: (b * nqb + i, h)),
        out_shape=jax.ShapeDtypeStruct((t, B_HEADS * HEAD_DIM), BF16),
        compiler_params=_params("parallel", "parallel", "parallel"),
        name="attn_window",
    )(sink, qk, qk, vt)


N_SEG = 8
LIN_PAD = 8


def _segment_plan(seq):
    m = -(-seq // (4 * N_SEG))
    m += 1 - m % 2
    while True:
        for dv in range(1, m + 1):
            if m % dv == 0 and (16 <= 4 * dv <= 64 or (dv == m and 4 * dv <= 96)):
                assert (N_SEG - 1) * 4 * m < seq <= N_SEG * 4 * m
                return 4 * m, 4 * dv
        m += 2


def _rglru_kernel(x_ref, y_ref, cw_ref, cb_ref, wr_ref, br_ref, wi_ref, bi_ref, lam_ref, o_ref,
                  lin_ref, hf_ref, pf_ref, hb_ref, pb_ref, *, pitch, chunk, lchunk):
    seq, bw = x_ref.shape
    nchunk = pitch // chunk
    rows = chunk * N_SEG
    left = CONV_W // 2
    last_valid = seq - (N_SEG - 1) * pitch
    lin_rows = lin_ref.shape[0]

    lin_ref[0:LIN_PAD, :] = jnp.zeros((LIN_PAD, bw), F32)
    lin_ref[LIN_PAD + seq:lin_rows, :] = jnp.zeros((lin_rows - LIN_PAD - seq, bw), F32)

    def copy_in(c, carry):
        t0 = pl.multiple_of(c * lchunk, lchunk)
        lin_ref[pl.ds(LIN_PAD + t0, lchunk), :] = x_ref[pl.ds(t0, lchunk), :]
        return carry

    lax.fori_loop(0, seq // lchunk, copy_in, 0)

    lam = lam_ref[...]
    log_sig = jnp.minimum(lam, 0.0) - jnp.log1p(jnp.exp(-jnp.abs(lam)))

    def conv(g0):
        u = cb_ref[...]
        for j in range(CONV_W):
            xt = jnp.concatenate(
                [lin_ref[pl.ds(LIN_PAD + g0 + g + j - left, N_SEG, stride=pitch), :] for g in range(chunk)],
                axis=0)
            u = u + xt * cw_ref[j:j + 1, :]
        return u

    half_c_log_sig = (0.5 * LRU_C) * log_sig

    def recurrence_terms(u, d):
        ub = u.astype(BF16)
        tr = jnp.tanh(jnp.dot(ub, wr_ref[d], preferred_element_type=F32) + br_ref[d:d + 1, :])
        ti = jnp.tanh(jnp.dot(ub, wi_ref[d], preferred_element_type=F32) + bi_ref[d:d + 1, :])
        k = half_c_log_sig[d:d + 1, :]
        log_a = tr * k + k
        a = jnp.exp(log_a)
        th = jnp.tanh(log_a)
        z = -2.0 * th / (1.0 - th)
        uh = 0.5 * u
        drive = z * lax.rsqrt(jnp.maximum(z, 1e-30)) * (ti * uh + uh)
        return a, drive

    def scan(a, b, h, p, reverse):
        hs = [None] * chunk
        ps = [None] * chunk
        for g in (range(chunk - 1, -1, -1) if reverse else range(chunk)):
            ag = a[g * N_SEG:(g + 1) * N_SEG]
            h = ag * h + b[g * N_SEG:(g + 1) * N_SEG]
            p = ag * p
            hs[g] = h
            ps[g] = p
        return jnp.concatenate(hs, axis=0), jnp.concatenate(ps, axis=0), h, p

    row = lax.broadcasted_iota(jnp.int32, (rows, bw), 0)
    in_last_segment = (row & (N_SEG - 1)) == N_SEG - 1
    group = row >> 3

    def step(c, carry):
        hf, pf, hb, pb = carry
        g0 = c * chunk
        a, drive = recurrence_terms(conv(g0), 0)
        hh, pp, hf, pf = scan(a, drive, hf, pf, reverse=False)
        r0 = pl.multiple_of(g0 * N_SEG, N_SEG)
        hf_ref[pl.ds(r0, rows), :] = hh
        pf_ref[pl.ds(r0, rows), :] = pp

        g1 = (nchunk - 1 - c) * chunk
        a, drive = recurrence_terms(conv(g1), 1)
        drive = jnp.where(in_last_segment & (group >= last_valid - g1), 0.0, drive)
        hh, pp, hb, pb = scan(a, drive, hb, pb, reverse=True)
        r1 = pl.multiple_of(g1 * N_SEG, N_SEG)
        hb_ref[pl.ds(r1, rows), :] = hh
        pb_ref[pl.ds(r1, rows), :] = pp
        return hf, pf, hb, pb

    zero = jnp.zeros((N_SEG, bw), F32)
    one = jnp.ones((N_SEG, bw), F32)
    hf, pf, hb, pb = lax.fori_loop(0, nchunk, step, (zero, one, zero, one), unroll=2)

    cin = [zero[0:1]]
    for r in range(1, N_SEG):
        cin.append(hf[r - 1:r] + pf[r - 1:r] * cin[r - 1])
    cinb = [zero[0:1]]
    for r in range(N_SEG - 2, -1, -1):
        cinb.append(hb[r + 1:r + 2] + pb[r + 1:r + 2] * cinb[-1])
    cin_f = jnp.concatenate([jnp.concatenate(cin, axis=0)] * chunk, axis=0)
    cin_b = jnp.concatenate([jnp.concatenate(cinb[::-1], axis=0)] * chunk, axis=0)

    def combine(c, carry):
        g0 = c * chunk
        r0 = pl.multiple_of(g0 * N_SEG, N_SEG)
        hsum = ((hf_ref[pl.ds(r0, rows), :] + pf_ref[pl.ds(r0, rows), :] * cin_f)
                + (hb_ref[pl.ds(r0, rows), :] + pb_ref[pl.ds(r0, rows), :] * cin_b))
        for g in range(chunk):
            lin_ref[pl.ds(LIN_PAD + g0 + g, N_SEG, stride=pitch), :] = hsum[g * N_SEG:(g + 1) * N_SEG]
        return carry

    lax.fori_loop(0, nchunk, combine, 0)

    def gate_out(c, carry):
        t0 = pl.multiple_of(c * lchunk, lchunk)
        gate = jax.nn.gelu(y_ref[pl.ds(t0, lchunk), :], approximate=True)
        o_ref[pl.ds(t0, lchunk), :] = (lin_ref[pl.ds(LIN_PAD + t0, lchunk), :] * gate).astype(BF16)
        return carry

    lax.fori_loop(0, seq // lchunk, gate_out, 0)


def _rglru(xy, conv_w, conv_b, wr, br, wi, bi, lam, nseq, seq):
    t = xy.shape[0]
    bw = C_WIDTH // C_BLOCKS
    pitch, chunk = _segment_plan(seq)
    lchunk = _tile(seq, 512)
    vec = lambda rows: pl.BlockSpec((rows, bw), lambda b, n: (0, n))
    mat = pl.BlockSpec((2, None, bw, bw), lambda b, n: (0, n, 0, 0))
    seg_major = pltpu.VMEM((pitch * N_SEG, bw), F32)
    return pl.pallas_call(
        functools.partial(_rglru_kernel, pitch=pitch, chunk=chunk, lchunk=lchunk),
        grid=(nseq, C_BLOCKS),
        in_specs=[pl.BlockSpec((seq, bw), lambda b, n: (b, n)),
                  pl.BlockSpec((seq, bw), lambda b, n: (b, C_BLOCKS + n)),
                  vec(CONV_W), vec(1), mat, vec(2), mat, vec(2), vec(2)],
        out_specs=pl.BlockSpec((seq, bw), lambda b, n: (b, n)),
        out_shape=jax.ShapeDtypeStruct((t, C_WIDTH), BF16),
        scratch_shapes=[pltpu.VMEM((LIN_PAD + pitch * N_SEG + 8, bw), F32),
                        seg_major, seg_major, seg_major, seg_major],
        compiler_params=_params("parallel", "parallel"),
        name="rglru",
    )(xy, xy, conv_w, conv_b.reshape(1, C_WIDTH), wr, br, wi, bi, lam)


def _merge_kernel(oa_ref, ob_ref, oc_ref, g0_ref, g1_ref, g2_ref, wb_ref, o_ref):
    acc = g0_ref[...] * jnp.dot(oa_ref[...], wb_ref[0], preferred_element_type=F32)
    acc = acc + g1_ref[...] * jnp.dot(ob_ref[...], wb_ref[1], preferred_element_type=F32)
    acc = acc + g2_ref[...] * jnp.dot(oc_ref[...], wb_ref[2], preferred_element_type=F32)
    o_ref[...] = acc.astype(BF16)


def _merge(oa, ob, oc, gates, wb):
    t, kw = oa.shape
    d = wb.shape[2]
    tm = _tile(t, 256)
    tn = _tile(d, 2048)
    ncb = d // tn
    o_spec = pl.BlockSpec((tm, kw), lambda j, i: (i, 0))
    g_spec = lambda br: pl.BlockSpec((tm, tn), lambda j, i: (i, br * ncb + j))
    return pl.pallas_call(
        _merge_kernel,
        grid=(ncb, t // tm),
        in_specs=[o_spec, o_spec, o_spec, g_spec(0), g_spec(1), g_spec(2),
                  pl.BlockSpec((N_BRANCH, kw, tn), lambda j, i: (0, 0, j))],
        out_specs=pl.BlockSpec((tm, tn), lambda j, i: (i, j)),
        out_shape=jax.ShapeDtypeStruct((t, d), BF16),
        compiler_params=_params("parallel", "parallel"),
        name="merge",
    )(oa, ob, oc, gates, gates, gates, wb)


def _out_kernel(m_ref, w_ref, gpost_ref, gnext_ref, *rest, nsplit, na):
    x_refs, (xo_ref, h_ref) = rest[:-2], rest[-2:]
    rows_per = m_ref.shape[0] // nsplit
    for r in range(nsplit):
        rows = slice(r * rows_per, (r + 1) * rows_per)
        y = jnp.dot(m_ref[rows, :], w_ref[...], preferred_element_type=F32)
        if len(x_refs) == 2:
            x = jnp.where(pl.program_id(0) < na, x_refs[0][rows, :], x_refs[1][rows, :])
        else:
            x = x_refs[0][rows, :]
        xn = x + _rms(y, gpost_ref[...])
        xo_ref[rows, :] = xn
        h_ref[rows, :] = _rms(xn, gnext_ref[...]).astype(BF16)


def _out_proj(merged, w, x, gpost, gnext):
    t, d = merged.shape
    row = lambda tm: pl.BlockSpec((tm, d), lambda i: (i, 0))
    if isinstance(x, tuple):
        tm = _tile(math.gcd(x[0].shape[0], x[1].shape[0]), 512)
        na, spec_a, spec_b = _two_part_rows(x[0], x[1], tm)
        x_args, x_specs = list(x), [spec_a, spec_b]
    else:
        tm = _tile(t, 512)
        na, x_args, x_specs = 0, [x], [row(tm)]
    gain = pl.BlockSpec((1, d), lambda i: (0, 0))
    return pl.pallas_call(
        functools.partial(_out_kernel, nsplit=4 if tm % 512 == 0 else 1, na=na),
        grid=(t // tm,),
        in_specs=[row(tm), pl.BlockSpec((d, d), lambda i: (0, 0)), gain, gain] + x_specs,
        out_specs=[row(tm), row(tm)],
        out_shape=[jax.ShapeDtypeStruct((t, d), F32), jax.ShapeDtypeStruct((t, d), BF16)],
        compiler_params=_params("parallel"),
        name="out_proj",
    )(merged, w, gpost.reshape(1, d), gnext.reshape(1, d), *x_args)


def _ffn_out_kernel(hid_ref, w_ref, x_ref, gpost_ref, gnext_ref, xo_ref, *rest, nsplit):
    h_ref = rest[0] if len(rest) == 2 else None
    xs_ref = rest[-1]
    k = pl.program_id(1)
    nk, _, slab = xs_ref.shape
    d = xo_ref.shape[1]

    xs_ref[k] = x_ref[...]

    @pl.when((k == 0) & (nk > 1))
    def _():
        xo_ref[...] = jnp.dot(hid_ref[...], w_ref[...], preferred_element_type=F32)

    @pl.when((k > 0) & (k < nk - 1))
    def _():
        xo_ref[...] += jnp.dot(hid_ref[...], w_ref[...], preferred_element_type=F32)

    @pl.when(k == nk - 1)
    def _():
        slabs = [slice(j * slab, (j + 1) * slab) for j in range(nk)]
        rows_per = xo_ref.shape[0] // nsplit
        for r in range(nsplit):
            rows = slice(r * rows_per, (r + 1) * rows_per)
            part = jnp.dot(hid_ref[rows, :], w_ref[...], preferred_element_type=F32)
            y = part if nk == 1 else xo_ref[rows, :] + part
            rs = lax.rsqrt(jnp.sum(jnp.square(y), axis=-1, keepdims=True) * (1.0 / d) + EPS)
            ssq = 0.0
            for j, sl in enumerate(slabs):
                xn = xs_ref[j, rows, :] + y[:, sl] * rs * gpost_ref[:, sl]
                xo_ref[rows, sl] = xn
                ssq = ssq + jnp.sum(jnp.square(xn), axis=-1, keepdims=True)
            if h_ref is not None:
                rs = lax.rsqrt(ssq * (1.0 / d) + EPS)
                for sl in slabs:
                    h_ref[rows, sl] = (xo_ref[rows, sl] * rs * gnext_ref[:, sl]).astype(BF16)


def _ffn_out(hid, w, x, gpost, gnext, row_start=0, nrows=None):
    d = x.shape[1]
    t = x.shape[0] if nrows is None else nrows
    ff = hid.shape[1]
    tm = _tile(math.gcd(row_start, t), 1024)
    tk = _tile(ff, 1024)
    nk = ff // tk
    slab = d // nk
    assert slab % HEAD_DIM == 0
    i0 = row_start // tm
    row_in = pl.BlockSpec((tm, slab), lambda i, k: (i0 + i, k))
    row = pl.BlockSpec((tm, d), lambda i, k: (i, 0))
    gain = pl.BlockSpec((1, d), lambda i, k: (0, 0))
    with_next = gnext is not None
    out_specs = [row, row] if with_next else [row]
    out_shape = [jax.ShapeDtypeStruct((t, d), F32)] + ([jax.ShapeDtypeStruct((t, d), BF16)] if with_next else [])
    gn = gnext if with_next else gpost
    outs = pl.pallas_call(
        functools.partial(_ffn_out_kernel, nsplit=4 if tm % 512 == 0 else 1),
        grid=(t // tm, ff // tk),
        in_specs=[pl.BlockSpec((tm, tk), lambda i, k: (i0 + i, k)),
                  pl.BlockSpec((tk, d), lambda i, k: (k, 0)),
                  row_in, gain, gain],
        out_specs=out_specs,
        out_shape=out_shape,
        scratch_shapes=[pltpu.VMEM((nk, tm, slab), F32)],
        compiler_params=_params("parallel", "arbitrary"),
        name="ffn_out",
    )(hid, w, x, gpost.reshape(1, d), gn.reshape(1, d))
    return (outs[0], outs[1]) if with_next else (outs[0], None)


def _rope_tables(ang):
    cos = jnp.cos(ang)
    sin = jnp.sin(ang)
    return jnp.concatenate([cos, cos], axis=-1), jnp.concatenate([-sin, sin], axis=-1)


def _angles_1d(s):
    t = jnp.arange(s, dtype=F32)
    inv = ROPE_THETA ** (-jnp.arange(0, HEAD_DIM, 2, dtype=F32) / HEAD_DIM)
    return t[:, None] * inv[None, :]


def _angles_axial(s):
    rows = s // GRID_W
    row = jnp.repeat(jnp.arange(rows, dtype=F32), GRID_W)
    col = jnp.tile(jnp.arange(GRID_W, dtype=F32), rows)
    half = HEAD_DIM // 2
    inv = ROPE_THETA ** (-jnp.arange(0, half, 2, dtype=F32) / half)
    return jnp.concatenate([row[:, None] * inv[None, :], col[:, None] * inv[None, :]], axis=-1)


def kernel(x_prompt, x_sample, norm_mix_pre, norm_mix_post, norm_ffn_pre, norm_ffn_post, w_in, q_norm_a, k_norm_a, sink_b, conv_w, conv_b, gate_r_w, gate_r_b, gate_i_w, gate_i_b, lru_lambda, w_branch, w_out, w_ffn_in, w_ffn_out):
    nb_p, seq, d = x_prompt.shape
    nb_s, seq_s, _ = x_sample.shape
    assert seq == seq_s
    nseq = nb_p + nb_s
    depth = w_in.shape[0]
    x = (x_prompt.reshape(nb_p * seq, d), x_sample.reshape(nb_s * seq, d))

    cos_ax, sin_ax = _rope_tables(_angles_axial(seq))
    cos_1d, sin_1d = _rope_tables(_angles_1d(seq))

    qa0 = 0
    ka0 = qa0 + A_HEADS * HEAD_DIM
    va0 = ka0 + A_KV * HEAD_DIM
    qb0 = va0 + A_KV * HEAD_DIM
    kb0 = qb0 + B_HEADS * HEAD_DIM
    vb0 = kb0 + B_KV * HEAD_DIM
    xc0 = vb0 + B_KV * HEAD_DIM
    g0 = xc0 + 2 * C_WIDTH
    cols = lambda w, a, b: w[:, a:b]

    h = _prenorm(x[0], x[1], norm_mix_pre[0])
    for l in range(depth):
        wl = w_in[l]
        w_qka = jnp.concatenate([cols(wl, qa0, ka0), cols(wl, ka0, va0)], axis=1).astype(BF16)
        w_qkb = jnp.concatenate([cols(wl, qb0, kb0), cols(wl, kb0, vb0)], axis=1).astype(BF16)
        w_v = jnp.concatenate([cols(wl, va0, qb0), cols(wl, vb0, xc0)], axis=1).astype(BF16)
        w_xy = cols(wl, xc0, g0).astype(BF16)
        w_g = wl[:, g0:].astype(BF16)
        gains_a = jnp.concatenate([jnp.tile(q_norm_a[l] * ATTN_QSCALE, A_HEADS),
                                   jnp.tile(k_norm_a[l], A_KV)]).reshape(1, -1)
        gains_b = jnp.concatenate([jnp.full((B_HEADS * HEAD_DIM,), ATTN_QSCALE, F32),
                                   jnp.ones((B_KV * HEAD_DIM,), F32)]).reshape(1, -1)

        qk_a = _proj_rope(h, w_qka, gains_a, cos_ax, sin_ax, seq, normed=True)
        qk_b = _proj_rope(h, w_qkb, gains_b, cos_1d, sin_1d, seq, normed=False)
        vt_a, vt_b = _proj_v(h, w_v, _attn_key_tile(seq))
        xy = _proj(h, w_xy, None, F32, "proj_xy")
        gates = _proj(h, w_g, "sigmoid", F32, "proj_gates")

        oa = _attn_global(qk_a, vt_a, nseq, seq)
        ob = _attn_window(qk_b, vt_b, sink_b[l], nseq, seq)
        oc = _rglru(xy, conv_w[l], conv_b[l], (0.5 * gate_r_w[l]).astype(BF16), 0.5 * gate_r_b[l],
                    (0.5 * gate_i_w[l]).astype(BF16), 0.5 * gate_i_b[l], lru_lambda[l], nseq, seq)

        merged = _merge(oa, ob, oc, gates, w_branch[l].astype(BF16))
        x, h2 = _out_proj(merged, w_out[l].astype(BF16), x, norm_mix_post[l], norm_ffn_pre[l])
        hid = _proj(h2, w_ffn_in[l].astype(BF16), "relu2", BF16, "ffn_in")
        w2 = w_ffn_out[l].astype(BF16)
        if l + 1 < depth:
            x, h = _ffn_out(hid, w2, x, norm_ffn_post[l], norm_mix_pre[l + 1])

    rows_p = nb_p * seq
    y_p, _ = _ffn_out(hid, w2, x, norm_ffn_post[depth - 1], None, 0, rows_p)
    y_s, _ = _ffn_out(hid, w2, x, norm_ffn_post[depth - 1], None, rows_p, nb_s * seq)
    return (y_p.reshape(nb_p, seq, d), y_s.reshape(nb_s, seq, d))
```
